```python
import math
import jax, jax.numpy as jnp
from jax import lax
import numpy as np

D_MODEL = 1024
BATCH = 16
SEQ = 2048
DEPTH = 1

CTX_LEN = 256
GRID_W = 64

N_HEADS = 8
QK_NOPE_DIM = 128
QK_ROPE_DIM = 64
V_HEAD_DIM = 128
Q_RANK = 384
KV_RANK = 256
ROPE_AXIS_DIM = QK_ROPE_DIM // 2
ROPE_BASE = 10000.0
Q_BLOCK = 128

S5_WIDTH = D_MODEL // 2
S5_GROUP = 16
S5_GROUPS = S5_WIDTH // S5_GROUP
S5_STATE = 64
S5_DT_MIN = 1e-3
S5_DT_MAX = 1e-1

PEER_HEADS = 8
PEER_N_KEYS = 128
PEER_EXPERTS = PEER_N_KEYS * PEER_N_KEYS
PEER_TOPK = 16
PEER_QUERY_DIM = 256
PEER_HALF = PEER_QUERY_DIM // 2
TOKEN_BLOCK = 128

IN_SPLIT_POINTS = (Q_RANK, Q_RANK + KV_RANK, Q_RANK + KV_RANK + QK_ROPE_DIM, Q_RANK + KV_RANK + QK_ROPE_DIM + S5_WIDTH, Q_RANK + KV_RANK + QK_ROPE_DIM + S5_WIDTH + D_MODEL)
IN_WIDTH = Q_RANK + KV_RANK + QK_ROPE_DIM + S5_WIDTH + 2 * D_MODEL

DEEPNORM_ALPHA = (2.0 * DEPTH) ** 0.25
DEEPNORM_BETA = (8.0 * DEPTH) ** -0.25
LN_EPS = 1e-6
N_MOD = 6

kernel_name = 'hybrid_mla_s5_peer_dit_block'


def layer_norm_plain(x):
    xf = x.astype(jnp.float32)
    xc = xf - jnp.mean(xf, axis=-1, keepdims=True)
    var = jnp.mean(xc * xc, axis=-1, keepdims=True)
    return (xc * lax.rsqrt(var + LN_EPS)).astype(x.dtype)


def layer_norm(x, g, b):
    return layer_norm_plain(x) * g + b


def rms_norm(x, g):
    xf = x.astype(jnp.float32)
    y = xf * lax.rsqrt(jnp.mean(xf * xf, axis=-1, keepdims=True) + LN_EPS)
    return y.astype(x.dtype) * g


def modulation(cond, w_mod, b_mod):
    m = jax.nn.silu(cond) @ w_mod + b_mod
    return jnp.split(m[..., None, :], N_MOD, axis=-1)


def modulate(h, shift, scale):
    return layer_norm_plain(h) * (1.0 + scale) + shift


def axial_rope_tables(rows, dtype):
    row = jnp.broadcast_to(jnp.arange(rows, dtype=jnp.float32)[:, None], (rows, GRID_W)).reshape(-1)
    col = jnp.broadcast_to(jnp.arange(GRID_W, dtype=jnp.float32)[None, :], (rows, GRID_W)).reshape(-1)
    inv_freq = jnp.power(ROPE_BASE, -jnp.arange(0, ROPE_AXIS_DIM, 2, dtype=jnp.float32) / ROPE_AXIS_DIM)
    ang_r = row[:, None, None] * inv_freq
    ang_c = col[:, None, None] * inv_freq
    return (jnp.cos(ang_r).astype(dtype), jnp.sin(ang_r).astype(dtype),
            jnp.cos(ang_c).astype(dtype), jnp.sin(ang_c).astype(dtype))


def rotate_half_pairs(x, cos, sin):
    x1, x2 = jnp.split(x, 2, axis=-1)
    return jnp.concatenate([x1 * cos - x2 * sin, x2 * cos + x1 * sin], axis=-1)


def axial_rope(x, rope):
    cos_r, sin_r, cos_c, sin_c = rope
    x_row, x_col = jnp.split(x, 2, axis=-1)
    return jnp.concatenate([rotate_half_pairs(x_row, cos_r, sin_r), rotate_half_pairs(x_col, cos_c, sin_c)], axis=-1)


def mla_project(cq, ckv, kpe, q_norm_g, kv_norm_g, w_uq, w_ukv, rope):
    b, l, _ = cq.shape
    q = (rms_norm(cq, q_norm_g) @ w_uq).reshape(b, l, N_HEADS, QK_NOPE_DIM + QK_ROPE_DIM)
    kv = (rms_norm(ckv, kv_norm_g) @ w_ukv).reshape(b, l, N_HEADS, QK_NOPE_DIM + V_HEAD_DIM)
    q_nope, q_pe = jnp.split(q, [QK_NOPE_DIM], axis=-1)
    k_nope, v = jnp.split(kv, [QK_NOPE_DIM], axis=-1)
    k_pe = kpe[:, :, None, :]
    if rope is not None:
        q_pe = axial_rope(q_pe, rope)
        k_pe = axial_rope(k_pe, rope)
    k_pe = jnp.broadcast_to(k_pe, (b, l, N_HEADS, QK_ROPE_DIM))
    q = jnp.concatenate([q_nope, q_pe], axis=-1)
    k = jnp.concatenate([k_nope, k_pe], axis=-1)
    return q.transpose(0, 2, 1, 3), k.transpose(0, 2, 1, 3), v.transpose(0, 2, 1, 3)


def block_attention(q, k, v):
    b, h, lq, dk = q.shape
    nblk = lq // Q_BLOCK
    scale = dk ** -0.5
    qb = jnp.moveaxis(q.reshape(b, h, nblk, Q_BLOCK, dk), 2, 0)

    def attend(qi):
        s = jnp.einsum('bhqd,bhkd->bhqk', qi, k).astype(jnp.float32) * scale
        p = jax.nn.softmax(s, axis=-1).astype(v.dtype)
        return jnp.einsum('bhqk,bhkd->bhqd', p, v)

    o = lax.map(attend, qb)
    return jnp.moveaxis(o, 0, 2).reshape(b, h, lq, v.shape[-1])


def s5_discretize(a_re, a_im, log_dt, b_re, b_im):
    a_re = a_re.astype(jnp.float32)
    a_im = a_im.astype(jnp.float32)
    b_re = b_re.astype(jnp.float32)
    b_im = b_im.astype(jnp.float32)
    dt = jnp.exp(log_dt.astype(jnp.float32))[:, None]
    mag = jnp.exp(a_re * dt)
    ab_re = mag * jnp.cos(a_im * dt)
    ab_im = mag * jnp.sin(a_im * dt)
    den = a_re * a_re + a_im * a_im
    f_re = ((ab_re - 1.0) * a_re + ab_im * a_im) / den
    f_im = (ab_im * a_re - (ab_re - 1.0) * a_im) / den
    bb_re = f_re[..., None] * b_re - f_im[..., None] * b_im
    bb_im = f_re[..., None] * b_im + f_im[..., None] * b_re
    return ab_re, ab_im, bb_re, bb_im


def s5_scan(u, ab_re, ab_im, bb_re, bb_im, h0_re, h0_im):
    bu_re = jnp.einsum('blgc,gpc->blgp', u, bb_re)
    bu_im = jnp.einsum('blgc,gpc->blgp', u, bb_im)
    bu_re = bu_re.at[:, 0].add(ab_re * h0_re - ab_im * h0_im)
    bu_im = bu_im.at[:, 0].add(ab_re * h0_im + ab_im * h0_re)
    shape = (1, u.shape[1]) + ab_re.shape
    a_re = jnp.broadcast_to(ab_re, shape)
    a_im = jnp.broadcast_to(ab_im, shape)

    def combine(e1, e2):
        a1r, a1i, b1r, b1i = e1
        a2r, a2i, b2r, b2i = e2
        return (a2r * a1r - a2i * a1i, a2r * a1i + a2i * a1r,
                a2r * b1r - a2i * b1i + b2r, a2r * b1i + a2i * b1r + b2i)

    _, _, h_re, h_im = lax.associative_scan(combine, (a_re, a_im, bu_re, bu_im), axis=1)
    return h_re, h_im


def s5_readout(h_re, h_im, c_re, c_im):
    return jnp.einsum('blgp,gcp->blgc', h_re, c_re) - jnp.einsum('blgp,gcp->blgc', h_im, c_im)


def s5_mixer(u_lat, u_ctx, a_re, a_im, log_dt, b_re, b_im, c_re, c_im, d_skip, need_ctx):
    bsz, l_lat, _ = u_lat.shape
    l_ctx = u_ctx.shape[1]
    ul = u_lat.astype(jnp.float32).reshape(bsz, l_lat, S5_GROUPS, S5_GROUP)
    uc = u_ctx.astype(jnp.float32).reshape(bsz, l_ctx, S5_GROUPS, S5_GROUP)
    dk = d_skip.astype(jnp.float32).reshape(S5_GROUPS, S5_GROUP)
    zeros = jnp.zeros((bsz, S5_GROUPS, S5_STATE), jnp.float32)
    y_lat = dk * ul
    y_ctx = dk * uc if need_ctx else None
    for direction in range(2):
        ab_re, ab_im, bb_re, bb_im = s5_discretize(a_re[direction], a_im[direction], log_dt[direction],
                                                   b_re[direction], b_im[direction])
        cr = c_re[direction].astype(jnp.float32)
        ci = c_im[direction].astype(jnp.float32)
        flip = direction == 1
        uc_d = jnp.flip(uc, axis=1) if flip else uc
        ul_d = jnp.flip(ul, axis=1) if flip else ul
        hc_re, hc_im = s5_scan(uc_d, ab_re, ab_im, bb_re, bb_im, zeros, zeros)
        hl_re, hl_im = s5_scan(ul_d, ab_re, ab_im, bb_re, bb_im, hc_re[:, -1], hc_im[:, -1])
        yl = s5_readout(hl_re, hl_im, cr, ci)
        y_lat = y_lat + (jnp.flip(yl, axis=1) if flip else yl)
        if need_ctx:
            yc = s5_readout(hc_re, hc_im, cr, ci)
            y_ctx = y_ctx + (jnp.flip(yc, axis=1) if flip else yc)
    y_lat = y_lat.reshape(bsz, l_lat, S5_WIDTH).astype(u_lat.dtype)
    if need_ctx:
        y_ctx = y_ctx.reshape(bsz, l_ctx, S5_WIDTH).astype(u_ctx.dtype)
    return y_lat, y_ctx


def branch_merge(att, s5_y, gate_mla, gate_s5, w_glu, w_out):
    b, h, l, dv = att.shape
    att = att.transpose(0, 2, 1, 3).reshape(b, l, h * dv)
    glu_a, glu_b = jnp.split(jax.nn.gelu(s5_y) @ w_glu, 2, axis=-1)
    s5_out = glu_a * jax.nn.sigmoid(glu_b)
    merged = jax.nn.sigmoid(gate_mla) * att + jax.nn.sigmoid(gate_s5) * s5_out
    return merged @ w_out


def peer(h, wq, keys, u_tab, v_tab):
    b, l, d = h.shape
    blocks = h.reshape(-1, TOKEN_BLOCK, d)

    def retrieve(xb):
        q = (xb @ wq).reshape(TOKEN_BLOCK, PEER_HEADS, 2, PEER_HALF)
        s = jnp.einsum('thzd,hznd->thzn', q, keys).astype(jnp.float32)
        sv, si = lax.top_k(s, PEER_TOPK)
        cand_s = (sv[:, :, 0, :, None] + sv[:, :, 1, None, :]).reshape(TOKEN_BLOCK, PEER_HEADS, PEER_TOPK * PEER_TOPK)
        cand_e = (si[:, :, 0, :, None] * PEER_N_KEYS + si[:, :, 1, None, :]).reshape(TOKEN_BLOCK, PEER_HEADS, PEER_TOPK * PEER_TOPK)
        top_s, top_pos = lax.top_k(cand_s, PEER_TOPK)
        experts = jnp.take_along_axis(cand_e, top_pos, axis=-1)
        g = jax.nn.softmax(top_s, axis=-1).astype(xb.dtype)
        act = jax.nn.gelu(jnp.einsum('thkd,td->thk', u_tab[experts], xb))
        return jnp.einsum('thk,thkd->td', g * act, v_tab[experts])

    return lax.map(retrieve, blocks).reshape(b, l, d)


def setup_inputs(seed: int = 0) -> dict:
    key = jax.random.key(seed)
    ks = jax.random.split(key, 29)
    f32 = jnp.float32

    def nrm(i, shape, scale):
        return jax.random.normal(ks[i], shape, f32) * scale

    G, P, C = S5_GROUPS, S5_STATE, S5_GROUP
    n_idx = jnp.arange(P, dtype=f32)
    return {
        'x': nrm(0, (BATCH, SEQ, D_MODEL), 1.0),
        'c': nrm(1, (BATCH, D_MODEL), 1.0),
        'ctx': nrm(2, (BATCH, CTX_LEN, D_MODEL), 1.0),
        'c_ctx': nrm(3, (D_MODEL,), 1.0),
        'w_mod': nrm(4, (DEPTH, D_MODEL, N_MOD * D_MODEL), 0.5 * D_MODEL ** -0.5),
        'b_mod': nrm(5, (DEPTH, N_MOD * D_MODEL), 0.01),
        'w_in': nrm(6, (DEPTH, D_MODEL, IN_WIDTH), D_MODEL ** -0.5),
        'q_norm_g': 1.0 + nrm(7, (DEPTH, Q_RANK), 0.02),
        'kv_norm_g': 1.0 + nrm(8, (DEPTH, KV_RANK), 0.02),
        'w_uq': nrm(9, (DEPTH, Q_RANK, N_HEADS * (QK_NOPE_DIM + QK_ROPE_DIM)), Q_RANK ** -0.5),
        'w_ukv': nrm(10, (DEPTH, KV_RANK, N_HEADS * (QK_NOPE_DIM + V_HEAD_DIM)), KV_RANK ** -0.5),
        's5_a_re': -0.5 + nrm(11, (DEPTH, 2, G, P), 0.01),
        's5_a_im': math.pi * n_idx + nrm(12, (DEPTH, 2, G, P), 0.01),
        's5_log_dt': jax.random.uniform(ks[13], (DEPTH, 2, G), f32, math.log(S5_DT_MIN), math.log(S5_DT_MAX)),
        's5_b_re': nrm(14, (DEPTH, 2, G, P, C), (2.0 * C) ** -0.5),
        's5_b_im': nrm(15, (DEPTH, 2, G, P, C), (2.0 * C) ** -0.5),
        's5_c_re': nrm(16, (DEPTH, 2, G, C, P), 2.0 ** -0.5),
        's5_c_im': nrm(17, (DEPTH, 2, G, C, P), 2.0 ** -0.5),
        's5_d': nrm(18, (DEPTH, S5_WIDTH), 1.0),
        'w_glu': nrm(19, (DEPTH, S5_WIDTH, 2 * D_MODEL), S5_WIDTH ** -0.5),
        'w_out': nrm(20, (DEPTH, D_MODEL, D_MODEL), DEEPNORM_BETA * D_MODEL ** -0.5),
        'ln1_g': 1.0 + nrm(21, (DEPTH, D_MODEL), 0.02),
        'ln1_b': nrm(22, (DEPTH, D_MODEL), 0.02),
        'peer_wq': nrm(23, (DEPTH, D_MODEL, PEER_HEADS * PEER_QUERY_DIM), D_MODEL ** -0.5),
        'peer_keys': nrm(24, (DEPTH, PEER_HEADS, 2, PEER_N_KEYS, PEER_HALF), PEER_HALF ** -0.5),
        'peer_u': nrm(25, (DEPTH, PEER_EXPERTS, D_MODEL), D_MODEL ** -0.5),
        'peer_v': nrm(26, (DEPTH, PEER_EXPERTS, D_MODEL), DEEPNORM_BETA),
        'ln2_g': 1.0 + nrm(27, (DEPTH, D_MODEL), 0.02),
        'ln2_b': nrm(28, (DEPTH, D_MODEL), 0.02),
    }


def reference(x, c, ctx, c_ctx, w_mod, b_mod, w_in, q_norm_g, kv_norm_g, w_uq, w_ukv,
              s5_a_re, s5_a_im, s5_log_dt, s5_b_re, s5_b_im, s5_c_re, s5_c_im, s5_d, w_glu,
              w_out, ln1_g, ln1_b, peer_wq, peer_keys, peer_u, peer_v, ln2_g, ln2_b):
    rows = x.shape[1] // GRID_W
    rope = axial_rope_tables(rows, x.dtype)
    h_lat, h_ctx = x, ctx
    for layer in range(DEPTH):
        need_ctx = layer < DEPTH - 1
        sh_a, sc_a, g_a, sh_f, sc_f, g_f = modulation(c, w_mod[layer], b_mod[layer])
        csh_a, csc_a, cg_a, csh_f, csc_f, cg_f = modulation(c_ctx, w_mod[layer], b_mod[layer])

        z_lat = modulate(h_lat, sh_a, sc_a) @ w_in[layer]
        z_ctx = modulate(h_ctx, csh_a, csc_a) @ w_in[layer]
        cq_l, ckv_l, kpe_l, su_l, gm_l, gs_l = jnp.split(z_lat, IN_SPLIT_POINTS, axis=-1)
        cq_c, ckv_c, kpe_c, su_c, gm_c, gs_c = jnp.split(z_ctx, IN_SPLIT_POINTS, axis=-1)

        q_l, k_l, v_l = mla_project(cq_l, ckv_l, kpe_l, q_norm_g[layer], kv_norm_g[layer], w_uq[layer], w_ukv[layer], rope)
        q_c, k_c, v_c = mla_project(cq_c, ckv_c, kpe_c, q_norm_g[layer], kv_norm_g[layer], w_uq[layer], w_ukv[layer], None)
        att_lat = block_attention(q_l, jnp.concatenate([k_c, k_l], axis=2), jnp.concatenate([v_c, v_l], axis=2))

        s5_lat, s5_ctx = s5_mixer(su_l, su_c, s5_a_re[layer], s5_a_im[layer], s5_log_dt[layer],
                                  s5_b_re[layer], s5_b_im[layer], s5_c_re[layer], s5_c_im[layer],
                                  s5_d[layer], need_ctx)

        out_lat = branch_merge(att_lat, s5_lat, gm_l, gs_l, w_glu[layer], w_out[layer])
        h_lat = layer_norm(DEEPNORM_ALPHA * h_lat + g_a * out_lat, ln1_g[layer], ln1_b[layer])

        if need_ctx:
            att_ctx = block_attention(q_c, k_c, v_c)
            out_ctx = branch_merge(att_ctx, s5_ctx, gm_c, gs_c, w_glu[layer], w_out[layer])
            h_ctx = layer_norm(DEEPNORM_ALPHA * h_ctx + cg_a * out_ctx, ln1_g[layer], ln1_b[layer])
            f_ctx = peer(modulate(h_ctx, csh_f, csc_f), peer_wq[layer], peer_keys[layer], peer_u[layer], peer_v[layer])
            h_ctx = layer_norm(DEEPNORM_ALPHA * h_ctx + cg_f * f_ctx, ln2_g[layer], ln2_b[layer])

        f_lat = peer(modulate(h_lat, sh_f, sc_f), peer_wq[layer], peer_keys[layer], peer_u[layer], peer_v[layer])
        h_lat = layer_norm(DEEPNORM_ALPHA * h_lat + g_f * f_lat, ln2_g[layer], ln2_b[layer])
    return h_lat
```

```python
import functools
import math

import jax
import jax.numpy as jnp
from jax import lax
from jax.experimental import pallas as pl
from jax.experimental.pallas import tpu as pltpu
from jax.experimental.pallas import tpu_sc as plsc

F32 = jnp.float32
BF16 = jnp.bfloat16

D_MODEL = 1024
DEPTH = 1
GRID_W = 64
N_HEADS = 8
QK_NOPE_DIM = 128
QK_ROPE_DIM = 64
V_HEAD_DIM = 128
Q_RANK = 384
KV_RANK = 256
ROPE_AXIS_DIM = QK_ROPE_DIM // 2
ROPE_BASE = 10000.0
S5_WIDTH = D_MODEL // 2
S5_GROUP = 16
S5_GROUPS = S5_WIDTH // S5_GROUP
S5_STATE = 64
PEER_HEADS = 8
PEER_N_KEYS = 128
PEER_TOPK = 16
PEER_HALF = 128
PEER_PAIRS = PEER_HEADS * PEER_TOPK
DEEPNORM_ALPHA = (2.0 * DEPTH) ** 0.25
LN_EPS = 1e-6
N_MOD = 6
ATT_SCALE = (QK_NOPE_DIM + QK_ROPE_DIM) ** -0.5

HEAD_PAD = 256
S5_COLS = S5_GROUPS * S5_STATE
S5_CHUNKS = 2
SC_WORKERS = 32
VMEM_LIMIT = 48 * 1024 * 1024


def _dot(a, b):
    return jnp.dot(a, b, preferred_element_type=F32)


def _dot_nt(a, b):
    return lax.dot_general(a, b, (((1,), (1,)), ((), ())), preferred_element_type=F32)


def _gelu(x):
    return 0.5 * x * (1.0 + jnp.tanh(0.7978845608028654 * (x + 0.044715 * (x * x * x))))


def _ln_plain(x):
    mu = jnp.mean(x, axis=-1, keepdims=True)
    xc = x - mu
    var = jnp.mean(xc * xc, axis=-1, keepdims=True)
    return xc * lax.rsqrt(var + LN_EPS)


def _rms(x, g):
    return x * lax.rsqrt(jnp.mean(x * x, axis=-1, keepdims=True) + LN_EPS) * g


def _params(*sem):
    return pltpu.CompilerParams(dimension_semantics=sem, vmem_limit_bytes=VMEM_LIMIT)


def _mod_kernel(cond_ref, w_ref, b_ref, o_ref):
    a = cond_ref[...]
    a = a * jax.nn.sigmoid(a)
    a_hi = a.astype(BF16)
    a_lo = (a - a_hi.astype(F32)).astype(BF16)
    w = w_ref[...]
    w_hi = w.astype(BF16)
    w_lo = (w - w_hi.astype(F32)).astype(BF16)
    o_ref[...] = _dot(a_hi, w_hi) + _dot(a_lo, w_hi) + _dot(a_hi, w_lo) + b_ref[...]


def _modulation(cond, w_mod, b_mod):
    rows = cond.shape[0]
    n = w_mod.shape[1]
    blk = D_MODEL
    return pl.pallas_call(
        _mod_kernel,
        grid=(n // blk,),
        in_specs=[
            pl.BlockSpec((rows, D_MODEL), lambda j: (0, 0)),
            pl.BlockSpec((D_MODEL, blk), lambda j: (0, j)),
            pl.BlockSpec((1, blk), lambda j: (0, j)),
        ],
        out_specs=pl.BlockSpec((rows, blk), lambda j: (0, j)),
        out_shape=jax.ShapeDtypeStruct((rows, n), F32),
        compiler_params=_params("arbitrary"),
        name="modulation",
    )(cond, w_mod, b_mod.reshape(1, n))


def _rope128(t, c_tab, s_tab):
    return t * c_tab + pltpu.roll(t, 64, axis=1) * s_tab


def _inproj_kernel(x_ref, sh_ref, sc_ref, c_ref, s_ref, wcq_ref, wckv_ref, wsu_ref, wgm_ref, wgs_ref,
                   qg_ref, kvg_ref, wuq_ref, wukv_ref, *out_refs, is_ctx):
    if is_ctx:
        k_ref, v_ref, su_ref = out_refs
    else:
        q_ref, k_ref, v_ref, su_ref, gm_ref, gs_ref = out_refs
    xm = _ln_plain(x_ref[0]) * (1.0 + sc_ref[0]) + sh_ref[0]
    xb = xm.astype(BF16)
    c_tab = c_ref[...]
    s_tab = s_ref[...]

    ckvpe = _dot(xb, wckv_ref[...])
    ckv_n = _rms(ckvpe[:, :KV_RANK], kvg_ref[...]).astype(BF16)
    kpe = _rope128(ckvpe[:, KV_RANK:], c_tab, s_tab).astype(BF16)
    kv = _dot(ckv_n, wukv_ref[...])
    for h in range(N_HEADS):
        k_ref[0, h, :, 0:128] = kv[:, h * 256:h * 256 + 128].astype(BF16)
        k_ref[0, h, :, 128:256] = kpe
        v_ref[0, h] = kv[:, h * 256 + 128:(h + 1) * 256].astype(BF16)

    su_ref[...] = _dot(xb, wsu_ref[...])

    if not is_ctx:
        cq_n = _rms(_dot(xb, wcq_ref[...]), qg_ref[...]).astype(BF16)
        q = _dot(cq_n, wuq_ref[...])
        for h in range(N_HEADS):
            q_ref[0, h, :, 0:128] = (q[:, h * 256:h * 256 + 128] * ATT_SCALE).astype(BF16)
            q_ref[0, h, :, 128:256] = (_rope128(q[:, h * 256 + 128:(h + 1) * 256], c_tab, s_tab) * ATT_SCALE).astype(BF16)
        gm_ref[0] = jax.nn.sigmoid(_dot(xb, wgm_ref[...])).astype(BF16)
        gs_ref[0] = jax.nn.sigmoid(_dot(xb, wgs_ref[...])).astype(BF16)


def _inproj(x, shift, scale, c_tab, s_tab, wts, *, is_ctx, tm):
    bsz, length, _ = x.shape
    mod_map = (lambda b, t: (0, 0, 0)) if shift.shape[0] == 1 else (lambda b, t: (b, 0, 0))
    const2 = lambda b, t: (0, 0)
    wcq, wckv, wsu, wgm, wgs, qg, kvg, wuq, wukv = wts
    in_specs = [
        pl.BlockSpec((1, tm, D_MODEL), lambda b, t: (b, t, 0)),
        pl.BlockSpec((1, 1, D_MODEL), mod_map),
        pl.BlockSpec((1, 1, D_MODEL), mod_map),
        pl.BlockSpec((tm, 128), lambda b, t: (t, 0)),
        pl.BlockSpec((tm, 128), lambda b, t: (t, 0)),
    ] + [pl.BlockSpec(w.shape, const2) for w in (wcq, wckv, wsu, wgm, wgs, qg, kvg, wuq, wukv)]
    head_spec = lambda width: pl.BlockSpec((1, N_HEADS, tm, width), lambda b, t: (b, 0, t, 0))
    row_spec = pl.BlockSpec((1, tm, D_MODEL), lambda b, t: (b, t, 0))
    su_spec = pl.BlockSpec((tm, S5_WIDTH), lambda b, t: (t, b))
    k_shape = jax.ShapeDtypeStruct((bsz, N_HEADS, length, HEAD_PAD), BF16)
    v_shape = jax.ShapeDtypeStruct((bsz, N_HEADS, length, V_HEAD_DIM), BF16)
    su_shape = jax.ShapeDtypeStruct((length, bsz * S5_WIDTH), F32)
    g_shape = jax.ShapeDtypeStruct((bsz, length, D_MODEL), BF16)
    if is_ctx:
        out_specs = [head_spec(HEAD_PAD), head_spec(V_HEAD_DIM), su_spec]
        out_shape = [k_shape, v_shape, su_shape]
    else:
        out_specs = [head_spec(HEAD_PAD), head_spec(HEAD_PAD), head_spec(V_HEAD_DIM), su_spec, row_spec, row_spec]
        out_shape = [k_shape, k_shape, v_shape, su_shape, g_shape, g_shape]
    return pl.pallas_call(
        functools.partial(_inproj_kernel, is_ctx=is_ctx),
        grid=(bsz, length // tm),
        in_specs=in_specs,
        out_specs=out_specs,
        out_shape=out_shape,
        compiler_params=_params("parallel", "parallel"),
        name="inproj_ctx" if is_ctx else "inproj_lat",
    )(x, shift, scale, c_tab, s_tab, wcq, wckv, wsu, wgm, wgs, qg, kvg, wuq, wukv)


def _attn_kernel(q_ref, kc_ref, vc_ref, kl_ref, vl_ref, o_ref):
    q = q_ref[0, 0]
    s_c = _dot_nt(q, kc_ref[0, 0])
    s_l = _dot_nt(q, kl_ref[0, 0])
    m = jnp.maximum(jnp.max(s_c, axis=-1, keepdims=True), jnp.max(s_l, axis=-1, keepdims=True))
    p_c = jnp.exp(s_c - m)
    p_l = jnp.exp(s_l - m)
    denom = jnp.sum(p_c, axis=-1, keepdims=True) + jnp.sum(p_l, axis=-1, keepdims=True)
    o = _dot(p_c.astype(BF16), vc_ref[0, 0]) + _dot(p_l.astype(BF16), vl_ref[0, 0])
    o_ref[0] = (o / denom).astype(BF16)


def _attention(q, k_ctx, v_ctx, k_lat, v_lat, *, tq):
    bsz, heads, length, _ = q.shape
    l_ctx = k_ctx.shape[2]
    full = lambda b, h, i: (b, h, 0, 0)
    return pl.pallas_call(
        _attn_kernel,
        grid=(bsz, heads, length // tq),
        in_specs=[
            pl.BlockSpec((1, 1, tq, HEAD_PAD), lambda b, h, i: (b, h, i, 0)),
            pl.BlockSpec((1, 1, l_ctx, HEAD_PAD), full),
            pl.BlockSpec((1, 1, l_ctx, V_HEAD_DIM), full),
            pl.BlockSpec((1, 1, length, HEAD_PAD), full),
            pl.BlockSpec((1, 1, length, V_HEAD_DIM), full),
        ],
        out_specs=pl.BlockSpec((1, tq, V_HEAD_DIM), lambda b, h, i: (b, i, h)),
        out_shape=jax.ShapeDtypeStruct((bsz, length, heads * V_HEAD_DIM), BF16),
        compiler_params=_params("parallel", "parallel", "arbitrary"),
        name="attention",
    )(q, k_ctx, v_ctx, k_lat, v_lat)


def _s5_kernel(u_ref, are_ref, aim_ref, bre_ref, bim_ref, cre_ref, cim_ref, y_ref,
               bu_re, bu_im, h_re, h_im, *, tt, bsz, n_ctx_tiles):
    d = pl.program_id(0)
    i = pl.program_id(1)

    @pl.when(i == 0)
    def _():
        h_re[...] = jnp.zeros_like(h_re)
        h_im[...] = jnp.zeros_like(h_im)

    u = u_ref[...].astype(BF16)
    half = S5_COLS // S5_CHUNKS
    cw = S5_WIDTH // S5_CHUNKS
    for c in range(S5_CHUNKS):
        uc = u[:, c * cw:(c + 1) * cw]
        bu_re[:, c * half:(c + 1) * half] = _dot(uc, bre_ref[0, c])
        bu_im[:, c * half:(c + 1) * half] = _dot(uc, bim_ref[0, c])

    col_w = 512
    for cc in range(S5_COLS // col_w):
        cols = slice(cc * col_w, (cc + 1) * col_w)
        a_r = are_ref[0, :, cols]
        a_i = aim_ref[0, :, cols]

        def step(t, carry, cols=cols, a_r=a_r, a_i=a_i):
            hr, hi = carry
            pos = jnp.where(d == 0, t, tt - 1 - t)
            r = pl.multiple_of(pos * bsz, bsz)
            nr = a_r * hr - a_i * hi + bu_re[pl.ds(r, bsz), cols]
            ni = a_r * hi + a_i * hr + bu_im[pl.ds(r, bsz), cols]
            bu_re[pl.ds(r, bsz), cols] = nr
            bu_im[pl.ds(r, bsz), cols] = ni
            return nr, ni

        hr, hi = lax.fori_loop(0, tt, step, (h_re[:, cols], h_im[:, cols]))
        h_re[:, cols] = hr
        h_im[:, cols] = hi

    @pl.when(i >= n_ctx_tiles)
    def _():
        for c in range(S5_CHUNKS):
            sl = slice(c * half, (c + 1) * half)
            y = _dot(bu_re[:, sl].astype(BF16), cre_ref[0, c]) + _dot(bu_im[:, sl].astype(BF16), cim_ref[0, c])
            y_ref[0, :, c * cw:(c + 1) * cw] = y


def _s5_scan(u_all, mats, *, bsz, l_ctx, tt):
    a_re_b, a_im_b, b_re_bd, b_im_bd, c_re_bd, c_im_bd = mats
    rows = tt * bsz
    n_tiles = u_all.shape[0] // rows
    n_ctx_tiles = l_ctx // tt
    n_lat_tiles = n_tiles - n_ctx_tiles

    def u_map(d, i):
        bwd = jnp.where(i < n_ctx_tiles, n_ctx_tiles - 1 - i, n_tiles - 1 - (i - n_ctx_tiles))
        return (jnp.where(d == 0, i, bwd), 0)

    def y_map(d, i):
        j = jnp.maximum(i - n_ctx_tiles, 0)
        return (d, jnp.where(d == 0, j, n_lat_tiles - 1 - j), 0)

    dmap3 = lambda d, i: (d, 0, 0)
    dmap4 = lambda d, i: (d, 0, 0, 0)
    return pl.pallas_call(
        functools.partial(_s5_kernel, tt=tt, bsz=bsz, n_ctx_tiles=n_ctx_tiles),
        grid=(2, n_tiles),
        in_specs=[
            pl.BlockSpec((rows, S5_WIDTH), u_map),
            pl.BlockSpec((1,) + a_re_b.shape[1:], dmap3),
            pl.BlockSpec((1,) + a_im_b.shape[1:], dmap3),
            pl.BlockSpec((1,) + b_re_bd.shape[1:], dmap4),
            pl.BlockSpec((1,) + b_im_bd.shape[1:], dmap4),
            pl.BlockSpec((1,) + c_re_bd.shape[1:], dmap4),
            pl.BlockSpec((1,) + c_im_bd.shape[1:], dmap4),
        ],
        out_specs=pl.BlockSpec((1, rows, S5_WIDTH), y_map),
        out_shape=jax.ShapeDtypeStruct((2, n_lat_tiles * rows, S5_WIDTH), F32),
        scratch_shapes=[
            pltpu.VMEM((rows, S5_COLS), F32),
            pltpu.VMEM((rows, S5_COLS), F32),
            pltpu.VMEM((bsz, S5_COLS), F32),
            pltpu.VMEM((bsz, S5_COLS), F32),
        ],
        compiler_params=_params("arbitrary", "arbitrary"),
        name="s5_scan",
    )(u_all, a_re_b, a_im_b, b_re_bd, b_im_bd, c_re_bd, c_im_bd)


def _s5_matrices(a_re, a_im, log_dt, b_re, b_im, c_re, c_im, bsz):
    dt = jnp.exp(log_dt)[..., None]
    mag = jnp.exp(a_re * dt)
    ab_re = mag * jnp.cos(a_im * dt)
    ab_im = mag * jnp.sin(a_im * dt)
    den = a_re * a_re + a_im * a_im
    f_re = ((ab_re - 1.0) * a_re + ab_im * a_im) / den
    f_im = (ab_im * a_re - (ab_re - 1.0) * a_im) / den
    bb_re = f_re[..., None] * b_re - f_im[..., None] * b_im
    bb_im = f_re[..., None] * b_im + f_im[..., None] * b_re
    gl = S5_GROUPS // S5_CHUNKS
    eye = jnp.eye(gl, dtype=F32)

    def in_bd(bb):
        t = bb.reshape(2, S5_CHUNKS, gl, S5_STATE, S5_GROUP)
        t = jnp.einsum('dkgpc,gh->dkgchp', t, eye)
        return t.reshape(2, S5_CHUNKS, gl * S5_GROUP, gl * S5_STATE).astype(BF16)

    def out_bd(cc):
        t = cc.reshape(2, S5_CHUNKS, gl, S5_GROUP, S5_STATE)
        t = jnp.einsum('dkgcp,gh->dkgphc', t, eye)
        return t.reshape(2, S5_CHUNKS, gl * S5_STATE, gl * S5_GROUP).astype(BF16)

    bcast = lambda a: jnp.broadcast_to(a.reshape(2, 1, S5_COLS), (2, bsz, S5_COLS))
    return bcast(ab_re), bcast(ab_im), in_bd(bb_re), in_bd(bb_im), out_bd(c_re), out_bd(-c_im)


def _merge_kernel(x_ref, att_ref, gm_ref, gs_ref, su_ref, y_ref, dsk_ref, wglu_ref, wout_ref, ga_ref,
                  l1g_ref, l1b_ref, shf_ref, scf_ref, wq_ref, keys_ref, h1_ref, xm_ref, st_ref):
    y = su_ref[...] * dsk_ref[...] + y_ref[0] + y_ref[1]
    gl = _dot(_gelu(y).astype(BF16), wglu_ref[...])
    s5_out = gl[:, :D_MODEL] * jax.nn.sigmoid(gl[:, D_MODEL:])
    merged = gm_ref[0].astype(F32) * att_ref[0].astype(F32) + gs_ref[0].astype(F32) * s5_out
    out = _dot(merged.astype(BF16), wout_ref[...])
    h1 = _ln_plain(DEEPNORM_ALPHA * x_ref[0] + ga_ref[0] * out) * l1g_ref[...] + l1b_ref[...]
    h1_ref[0] = h1
    xm = (_ln_plain(h1) * (1.0 + scf_ref[0]) + shf_ref[0]).astype(BF16)
    xm_ref[0] = xm
    qp = _dot(xm, wq_ref[...]).astype(BF16)
    for j in range(2 * PEER_HEADS):
        st_ref[j] = _dot_nt(keys_ref[j], qp[:, j * PEER_HALF:(j + 1) * PEER_HALF])


def _merge(x, att, gm, gs, su_all2d, y2d, dsk, wglu, wout, g_a, l1g, l1b, sh_f, sc_f, wq, keys, *, tm, ctx_tiles):
    bsz, length, _ = x.shape
    nt = length // tm
    row = pl.BlockSpec((1, tm, D_MODEL), lambda b, t: (b, t, 0))
    modb = pl.BlockSpec((1, 1, D_MODEL), lambda b, t: (b, 0, 0))
    const = lambda a: pl.BlockSpec(a.shape, lambda b, t: (0,) * a.ndim)
    return pl.pallas_call(
        _merge_kernel,
        grid=(bsz, nt),
        in_specs=[
            row, row, row, row,
            pl.BlockSpec((tm, S5_WIDTH), lambda b, t: (ctx_tiles + t, b)),
            pl.BlockSpec((2, tm, S5_WIDTH), lambda b, t: (0, t, b)),
            const(dsk), const(wglu), const(wout), modb, const(l1g), const(l1b), modb, modb, const(wq), const(keys),
        ],
        out_specs=[row, row, pl.BlockSpec((2 * PEER_HEADS, PEER_N_KEYS, tm), lambda b, t: (0, 0, b * nt + t))],
        out_shape=[
            jax.ShapeDtypeStruct((bsz, length, D_MODEL), F32),
            jax.ShapeDtypeStruct((bsz, length, D_MODEL), BF16),
            jax.ShapeDtypeStruct((2 * PEER_HEADS, PEER_N_KEYS, bsz * length), F32),
        ],
        compiler_params=_params("parallel", "parallel"),
        name="merge_peer_query",
    )(x, att, gm, gs, su_all2d, y2d, dsk, wglu, wout, g_a, l1g, l1b, sh_f, sc_f, wq, keys)


def _take_top(vals, codes, payload, k):
    rows = lax.broadcasted_iota(jnp.int32, (k, vals.shape[1]), 0)
    top_v = jnp.zeros((k, vals.shape[1]), F32)
    top_p = jnp.zeros((k, vals.shape[1]), jnp.int32)
    big = jnp.int32(1 << 30)
    for r in range(k):
        m = jnp.max(vals, axis=0, keepdims=True)
        cm = jnp.min(jnp.where(vals == m, codes, big), axis=0, keepdims=True)
        sel = codes == cm
        if payload is None:
            p = cm
        else:
            p = jnp.max(jnp.where(sel, payload, -1), axis=0, keepdims=True)
        vals = jnp.where(sel, -jnp.inf, vals)
        top_v = jnp.where(rows == r, m, top_v)
        top_p = jnp.where(rows == r, p, top_p)
    return top_v, top_p


def _topk_kernel(st_ref, e_ref, g_ref):
    lanes = st_ref.shape[2]
    key_idx = lax.broadcasted_iota(jnp.int32, (PEER_N_KEYS, lanes), 0)
    sub_idx = lax.broadcasted_iota(jnp.int32, (PEER_TOPK, lanes), 0)
    for h in range(PEER_HEADS):
        sv0, si0 = _take_top(st_ref[2 * h], key_idx, None, PEER_TOPK)
        sv1, si1 = _take_top(st_ref[2 * h + 1], key_idx, None, PEER_TOPK)
        cs, ce, cc = [], [], []
        for i in range(PEER_TOPK):
            nj = PEER_TOPK if i == 0 else PEER_TOPK // 2
            cs.append(sv0[i:i + 1, :] + sv1[0:nj, :])
            ce.append(si0[i:i + 1, :] * PEER_N_KEYS + si1[0:nj, :])
            cc.append(sub_idx[0:nj, :] + i * PEER_TOPK)
        top_s, top_e = _take_top(jnp.concatenate(cs, axis=0), jnp.concatenate(cc, axis=0),
                                 jnp.concatenate(ce, axis=0), PEER_TOPK)
        p = jnp.exp(top_s - jnp.max(top_s, axis=0, keepdims=True))
        g_ref[h] = p / jnp.sum(p, axis=0, keepdims=True)
        e_ref[h] = top_e


def _peer_topk(st, *, lanes):
    n = st.shape[2]
    out_spec = pl.BlockSpec((PEER_HEADS, PEER_TOPK, lanes), lambda i: (0, 0, i))
    return pl.pallas_call(
        _topk_kernel,
        grid=(n // lanes,),
        in_specs=[pl.BlockSpec((2 * PEER_HEADS, PEER_N_KEYS, lanes), lambda i: (0, 0, i))],
        out_specs=[out_spec, out_spec],
        out_shape=[jax.ShapeDtypeStruct((PEER_HEADS, PEER_TOPK, n), jnp.int32),
                   jax.ShapeDtypeStruct((PEER_HEADS, PEER_TOPK, n), F32)],
        compiler_params=_params("parallel"),
        name="peer_topk",
    )(st)


def _sc_gather(table, idx, *, win):
    n_idx = idx.shape[0]
    width = table.shape[1]
    per_worker = n_idx // SC_WORKERS
    n_win = per_worker // win
    mesh = plsc.VectorSubcoreMesh(core_axis_name="c", subcore_axis_name="s")

    @functools.partial(
        pl.kernel, mesh=mesh,
        out_type=jax.ShapeDtypeStruct((n_idx, width), table.dtype),
        scratch_types=[pltpu.VMEM((win,), jnp.int32), pltpu.VMEM((win, width), table.dtype), pltpu.SemaphoreType.DMA],
    )
    def gather(table_hbm, idx_hbm, out_hbm, idx_v, rows_v, sem):
        wid = lax.axis_index("s") * 2 + lax.axis_index("c")
        base = wid * per_worker

        @pl.loop(0, n_win)
        def _(j):
            off = pl.multiple_of(base + j * win, 8)
            pltpu.sync_copy(idx_hbm.at[pl.ds(off, win)], idx_v)
            pltpu.async_copy(table_hbm.at[idx_v], rows_v, sem).wait()
            pltpu.sync_copy(rows_v, out_hbm.at[pl.ds(off, win)])

    return gather(table, idx)


def _peer_kernel(ug_ref, vg_ref, xm_ref, w_ref, h1_ref, gf_ref, l2g_ref, l2b_ref, o_ref, f_ref, *, tb):
    sub = 8
    span = sub * PEER_PAIRS
    lane_blk = lax.broadcasted_iota(jnp.int32, (sub, span), 1) // PEER_PAIRS
    diag = lane_blk == lax.broadcasted_iota(jnp.int32, (sub, span), 0)
    for s in range(tb // sub):
        rows = slice(s * span, (s + 1) * span)
        toks = slice(s * sub, (s + 1) * sub)
        dots = _dot_nt(xm_ref[toks, :], ug_ref[rows, :].astype(BF16))
        dots = jnp.where(diag, dots, 0.0)
        act = dots[:, 0:PEER_PAIRS]
        for c in range(1, sub):
            act = act + dots[:, c * PEER_PAIRS:(c + 1) * PEER_PAIRS]
        coef = w_ref[toks, :] * _gelu(act)
        coef_bd = jnp.where(diag, jnp.concatenate([coef] * sub, axis=1), 0.0).astype(BF16)
        f_ref[toks, :] = _dot(coef_bd, vg_ref[rows, :].astype(BF16))
    r = DEEPNORM_ALPHA * h1_ref[...] + gf_ref[0] * f_ref[...]
    o_ref[...] = _ln_plain(r) * l2g_ref[...] + l2b_ref[...]


def _peer_eval(ug, vg, xm, w, h1, g_f, l2g, l2b, *, tb, tok0, length):
    n = xm.shape[0]
    tok_blk0 = tok0 // tb
    per_b = length // tb
    tok_spec = lambda width: pl.BlockSpec((tb, width), lambda i: (i, 0))
    return pl.pallas_call(
        functools.partial(_peer_kernel, tb=tb),
        grid=(n // tb,),
        in_specs=[
            pl.BlockSpec((tb * PEER_PAIRS, D_MODEL), lambda i: (i, 0)),
            pl.BlockSpec((tb * PEER_PAIRS, D_MODEL), lambda i: (i, 0)),
            tok_spec(D_MODEL), tok_spec(PEER_PAIRS), tok_spec(D_MODEL),
            pl.BlockSpec((1, 1, D_MODEL), lambda i: ((tok_blk0 + i) // per_b, 0, 0)),
            pl.BlockSpec((1, D_MODEL), lambda i: (0, 0)),
            pl.BlockSpec((1, D_MODEL), lambda i: (0, 0)),
        ],
        out_specs=tok_spec(D_MODEL),
        out_shape=jax.ShapeDtypeStruct((n, D_MODEL), F32),
        scratch_shapes=[pltpu.VMEM((tb, D_MODEL), F32)],
        compiler_params=_params("parallel"),
        name="peer_eval",
    )(ug, vg, xm, w, h1, g_f, l2g, l2b)


def _rope_perm():
    half = ROPE_AXIS_DIM // 2
    base = jnp.arange(QK_ROPE_DIM)
    return jnp.where((base % ROPE_AXIS_DIM) < half, base + half, base - half)


def _rope_tables(length):
    pos = jnp.arange(length)
    row = (pos // GRID_W).astype(F32)
    col = (pos % GRID_W).astype(F32)
    inv_freq = jnp.power(ROPE_BASE, -jnp.arange(0, ROPE_AXIS_DIM, 2, dtype=F32) / ROPE_AXIS_DIM)
    ang_r = row[:, None] * inv_freq
    ang_c = col[:, None] * inv_freq
    zeros = jnp.zeros((length, QK_ROPE_DIM), F32)
    c_tab = jnp.concatenate([jnp.cos(ang_r), jnp.cos(ang_r), jnp.cos(ang_c), jnp.cos(ang_c), zeros], axis=1)
    s_tab = jnp.concatenate([-jnp.sin(ang_r), jnp.sin(ang_r), -jnp.sin(ang_c), jnp.sin(ang_c), zeros], axis=1)
    return c_tab, s_tab


def _identity_tables(length):
    ones = jnp.ones((length, QK_ROPE_DIM), F32)
    zeros = jnp.zeros((length, QK_ROPE_DIM), F32)
    return jnp.concatenate([ones, zeros], axis=1), jnp.zeros((length, 2 * QK_ROPE_DIM), F32)


def _layer_weights(w_in, q_norm_g, kv_norm_g, w_uq, w_ukv):
    perm = _rope_perm()
    p0 = Q_RANK
    p1 = p0 + KV_RANK
    p2 = p1 + QK_ROPE_DIM
    p3 = p2 + S5_WIDTH
    p4 = p3 + D_MODEL
    w_kpe = w_in[:, p1:p2]
    wckv = jnp.concatenate([w_in[:, p0:p1], w_kpe, w_kpe[:, perm]], axis=1)
    uq = w_uq.reshape(Q_RANK, N_HEADS, QK_NOPE_DIM + QK_ROPE_DIM)
    pe = uq[:, :, QK_NOPE_DIM:]
    wuq = jnp.concatenate([uq[:, :, :QK_NOPE_DIM], pe, pe[:, :, perm]], axis=2).reshape(Q_RANK, N_HEADS * HEAD_PAD)
    cast = lambda a: a.astype(BF16)
    return (cast(w_in[:, :p0]), cast(wckv), cast(w_in[:, p2:p3]), cast(w_in[:, p3:p4]), cast(w_in[:, p4:]),
            q_norm_g.reshape(1, Q_RANK), kv_norm_g.reshape(1, KV_RANK), cast(wuq), cast(w_ukv))


def kernel(x, c, ctx, c_ctx, w_mod, b_mod, w_in, q_norm_g, kv_norm_g, w_uq, w_ukv, s5_a_re, s5_a_im, s5_log_dt, s5_b_re, s5_b_im, s5_c_re, s5_c_im, s5_d, w_glu, w_out, ln1_g, ln1_b, peer_wq, peer_keys, peer_u, peer_v, ln2_g, ln2_b):
    bsz, length, _ = x.shape
    l_ctx = ctx.shape[1]
    n_tok = bsz * length
    layer = 0
    tm = 256
    s5_tt = 64
    tok_chunk = 4096
    peer_tb = 16

    cond = jnp.concatenate([c, c_ctx[None, :], jnp.zeros((7, D_MODEL), F32)], axis=0)
    mod = _modulation(cond, w_mod[layer], b_mod[layer]).reshape(cond.shape[0], N_MOD, 1, D_MODEL)
    sh_a, sc_a, g_a, sh_f, sc_f, g_f = (mod[:bsz, i] for i in range(N_MOD))
    csh_a, csc_a = mod[bsz:bsz + 1, 0], mod[bsz:bsz + 1, 1]

    wts = _layer_weights(w_in[layer], q_norm_g[layer], kv_norm_g[layer], w_uq[layer], w_ukv[layer])
    c_lat, s_lat = _rope_tables(length)
    c_id, s_id = _identity_tables(l_ctx)
    k_ctx, v_ctx, su_ctx = _inproj(ctx, csh_a, csc_a, c_id, s_id, wts, is_ctx=True, tm=min(tm, l_ctx))
    q, k_lat, v_lat, su_lat, gm, gs = _inproj(x, sh_a, sc_a, c_lat, s_lat, wts, is_ctx=False, tm=tm)

    att = _attention(q, k_ctx, v_ctx, k_lat, v_lat, tq=min(512, length))

    su_all = jnp.concatenate([su_ctx, su_lat], axis=0)
    mats = _s5_matrices(s5_a_re[layer], s5_a_im[layer], s5_log_dt[layer], s5_b_re[layer], s5_b_im[layer],
                        s5_c_re[layer], s5_c_im[layer], bsz)
    y = _s5_scan(su_all.reshape(-1, S5_WIDTH), mats, bsz=bsz, l_ctx=l_ctx, tt=s5_tt)
    y2d = y.reshape(2, length, bsz * S5_WIDTH)

    keys = peer_keys[layer].reshape(2 * PEER_HEADS, PEER_N_KEYS, PEER_HALF).astype(BF16)
    h1, xm, st = _merge(x, att, gm, gs, su_all, y2d, s5_d[layer].reshape(1, S5_WIDTH),
                        w_glu[layer].astype(BF16), w_out[layer].astype(BF16), g_a,
                        ln1_g[layer].reshape(1, D_MODEL), ln1_b[layer].reshape(1, D_MODEL), sh_f, sc_f,
                        peer_wq[layer].astype(BF16), keys, tm=tm, ctx_tiles=l_ctx // tm)

    experts, gates = _peer_topk(st, lanes=128)
    idx = experts.transpose(2, 0, 1).reshape(n_tok * PEER_PAIRS)
    w = gates.transpose(2, 0, 1).reshape(n_tok, PEER_PAIRS)

    h1f = h1.reshape(n_tok, D_MODEL)
    xmf = xm.reshape(n_tok, D_MODEL)
    l2g = ln2_g[layer].reshape(1, D_MODEL)
    l2b = ln2_b[layer].reshape(1, D_MODEL)
    tok_chunk = min(tok_chunk, n_tok)
    outs = []
    for t0 in range(0, n_tok, tok_chunk):
        ids = idx[t0 * PEER_PAIRS:(t0 + tok_chunk) * PEER_PAIRS]
        ug = _sc_gather(peer_u[layer], ids, win=32)
        vg = _sc_gather(peer_v[layer], ids, win=32)
        outs.append(_peer_eval(ug, vg, xmf[t0:t0 + tok_chunk], w[t0:t0 + tok_chunk], h1f[t0:t0 + tok_chunk],
                               g_f, l2g, l2b, tb=peer_tb, tok0=t0, length=length))
    return jnp.concatenate(outs, axis=0).reshape(bsz, length, D_MODEL)
```

```python
import functools
import math

import jax
import jax.numpy as jnp
from jax import lax
from jax.experimental import pallas as pl
from jax.experimental.pallas import tpu as pltpu
from jax.experimental.pallas import tpu_sc as plsc

F32 = jnp.float32
BF16 = jnp.bfloat16

D_MODEL = 1024
DEPTH = 1
GRID_W = 64
N_HEADS = 8
QK_NOPE_DIM = 128
QK_ROPE_DIM = 64
V_HEAD_DIM = 128
Q_RANK = 384
KV_RANK = 256
ROPE_AXIS_DIM = QK_ROPE_DIM // 2
ROPE_BASE = 10000.0
S5_WIDTH = D_MODEL // 2
S5_GROUP = 16
S5_GROUPS = S5_WIDTH // S5_GROUP
S5_STATE = 64
PEER_HEADS = 8
PEER_N_KEYS = 128
PEER_TOPK = 16
PEER_HALF = 128
PEER_PAIRS = PEER_HEADS * PEER_TOPK
DEEPNORM_ALPHA = (2.0 * DEPTH) ** 0.25
LN_EPS = 1e-6
N_MOD = 6
ATT_SCALE = (QK_NOPE_DIM + QK_ROPE_DIM) ** -0.5

HEAD_PAD = 256
S5_COLS = S5_GROUPS * S5_STATE
S5_CHUNKS = 2
SC_WORKERS = 32
VMEM_LIMIT = 48 * 1024 * 1024


def _dot(a, b):
    return jnp.dot(a, b, preferred_element_type=F32)


def _dot_nt(a, b):
    return lax.dot_general(a, b, (((1,), (1,)), ((), ())), preferred_element_type=F32)


def _gelu(x):
    return 0.5 * x * (1.0 + jnp.tanh(0.7978845608028654 * (x + 0.044715 * (x * x * x))))


def _ln_plain(x):
    mu = jnp.mean(x, axis=-1, keepdims=True)
    xc = x - mu
    var = jnp.mean(xc * xc, axis=-1, keepdims=True)
    return xc * lax.rsqrt(var + LN_EPS)


def _rms(x, g):
    return x * lax.rsqrt(jnp.mean(x * x, axis=-1, keepdims=True) + LN_EPS) * g


def _params(*sem):
    return pltpu.CompilerParams(dimension_semantics=sem, vmem_limit_bytes=VMEM_LIMIT)


def _mod_kernel(cond_ref, w_ref, b_ref, o_ref):
    a = cond_ref[...]
    a = a * jax.nn.sigmoid(a)
    a_hi = a.astype(BF16)
    a_lo = (a - a_hi.astype(F32)).astype(BF16)
    w = w_ref[...]
    w_hi = w.astype(BF16)
    w_lo = (w - w_hi.astype(F32)).astype(BF16)
    o_ref[...] = _dot(a_hi, w_hi) + _dot(a_lo, w_hi) + _dot(a_hi, w_lo) + b_ref[...]


def _modulation(cond, w_mod, b_mod):
    rows = cond.shape[0]
    n = w_mod.shape[1]
    blk = D_MODEL
    return pl.pallas_call(
        _mod_kernel,
        grid=(n // blk,),
        in_specs=[
            pl.BlockSpec((rows, D_MODEL), lambda j: (0, 0)),
            pl.BlockSpec((D_MODEL, blk), lambda j: (0, j)),
            pl.BlockSpec((1, blk), lambda j: (0, j)),
        ],
        out_specs=pl.BlockSpec((rows, blk), lambda j: (0, j)),
        out_shape=jax.ShapeDtypeStruct((rows, n), F32),
        compiler_params=_params("arbitrary"),
        name="modulation",
    )(cond, w_mod, b_mod.reshape(1, n))


def _rope128(t, c_tab, s_tab):
    return t * c_tab + pltpu.roll(t, 64, axis=1) * s_tab


def _inproj_kernel(x_ref, sh_ref, sc_ref, c_ref, s_ref, wcq_ref, wckv_ref, wsu_ref, wgm_ref, wgs_ref,
                   qg_ref, kvg_ref, wuq_ref, wukv_ref, *out_refs, is_ctx):
    if is_ctx:
        k_ref, v_ref, su_ref = out_refs
    else:
        q_ref, k_ref, v_ref, su_ref, gm_ref, gs_ref = out_refs
    xm = _ln_plain(x_ref[0]) * (1.0 + sc_ref[0]) + sh_ref[0]
    xb = xm.astype(BF16)
    c_tab = c_ref[...]
    s_tab = s_ref[...]

    ckvpe = _dot(xb, wckv_ref[...])
    ckv_n = _rms(ckvpe[:, :KV_RANK], kvg_ref[...]).astype(BF16)
    kpe = _rope128(ckvpe[:, KV_RANK:], c_tab, s_tab).astype(BF16)
    kv = _dot(ckv_n, wukv_ref[...])
    for h in range(N_HEADS):
        k_ref[0, h, :, 0:128] = kv[:, h * 256:h * 256 + 128].astype(BF16)
        k_ref[0, h, :, 128:256] = kpe
        v_ref[0, h] = kv[:, h * 256 + 128:(h + 1) * 256].astype(BF16)

    su_ref[...] = _dot(xb, wsu_ref[...])

    if not is_ctx:
        cq_n = _rms(_dot(xb, wcq_ref[...]), qg_ref[...]).astype(BF16)
        q = _dot(cq_n, wuq_ref[...])
        for h in range(N_HEADS):
            q_ref[0, h, :, 0:128] = (q[:, h * 256:h * 256 + 128] * ATT_SCALE).astype(BF16)
            q_ref[0, h, :, 128:256] = (_rope128(q[:, h * 256 + 128:(h + 1) * 256], c_tab, s_tab) * ATT_SCALE).astype(BF16)
        gm_ref[0] = jax.nn.sigmoid(_dot(xb, wgm_ref[...])).astype(BF16)
        gs_ref[0] = jax.nn.sigmoid(_dot(xb, wgs_ref[...])).astype(BF16)


def _inproj(x, shift, scale, c_tab, s_tab, wts, *, is_ctx, tm):
    bsz, length, _ = x.shape
    mod_map = (lambda b, t: (0, 0, 0)) if shift.shape[0] == 1 else (lambda b, t: (b, 0, 0))
    const2 = lambda b, t: (0, 0)
    wcq, wckv, wsu, wgm, wgs, qg, kvg, wuq, wukv = wts
    in_specs = [
        pl.BlockSpec((1, tm, D_MODEL), lambda b, t: (b, t, 0)),
        pl.BlockSpec((1, 1, D_MODEL), mod_map),
        pl.BlockSpec((1, 1, D_MODEL), mod_map),
        pl.BlockSpec((tm, 128), lambda b, t: (t, 0)),
        pl.BlockSpec((tm, 128), lambda b, t: (t, 0)),
    ] + [pl.BlockSpec(w.shape, const2) for w in (wcq, wckv, wsu, wgm, wgs, qg, kvg, wuq, wukv)]
    head_spec = lambda width: pl.BlockSpec((1, N_HEADS, tm, width), lambda b, t: (b, 0, t, 0))
    row_spec = pl.BlockSpec((1, tm, D_MODEL), lambda b, t: (b, t, 0))
    su_spec = pl.BlockSpec((tm, S5_WIDTH), lambda b, t: (t, b))
    k_shape = jax.ShapeDtypeStruct((bsz, N_HEADS, length, HEAD_PAD), BF16)
    v_shape = jax.ShapeDtypeStruct((bsz, N_HEADS, length, V_HEAD_DIM), BF16)
    su_shape = jax.ShapeDtypeStruct((length, bsz * S5_WIDTH), F32)
    g_shape = jax.ShapeDtypeStruct((bsz, length, D_MODEL), BF16)
    if is_ctx:
        out_specs = [head_spec(HEAD_PAD), head_spec(V_HEAD_DIM), su_spec]
        out_shape = [k_shape, v_shape, su_shape]
    else:
        out_specs = [head_spec(HEAD_PAD), head_spec(HEAD_PAD), head_spec(V_HEAD_DIM), su_spec, row_spec, row_spec]
        out_shape = [k_shape, k_shape, v_shape, su_shape, g_shape, g_shape]
    return pl.pallas_call(
        functools.partial(_inproj_kernel, is_ctx=is_ctx),
        grid=(bsz, length // tm),
        in_specs=in_specs,
        out_specs=out_specs,
        out_shape=out_shape,
        compiler_params=_params("parallel", "parallel"),
        name="inproj_ctx" if is_ctx else "inproj_lat",
    )(x, shift, scale, c_tab, s_tab, wcq, wckv, wsu, wgm, wgs, qg, kvg, wuq, wukv)


def _attn_kernel(q_ref, kc_ref, vc_ref, kl_ref, vl_ref, o_ref):
    q = q_ref[0, 0]
    s_c = _dot_nt(q, kc_ref[0, 0])
    s_l = _dot_nt(q, kl_ref[0, 0])
    m = jnp.maximum(jnp.max(s_c, axis=-1, keepdims=True), jnp.max(s_l, axis=-1, keepdims=True))
    p_c = jnp.exp(s_c - m)
    p_l = jnp.exp(s_l - m)
    denom = jnp.sum(p_c, axis=-1, keepdims=True) + jnp.sum(p_l, axis=-1, keepdims=True)
    o = _dot(p_c.astype(BF16), vc_ref[0, 0]) + _dot(p_l.astype(BF16), vl_ref[0, 0])
    o_ref[0] = (o / denom).astype(BF16)


def _attention(q, k_ctx, v_ctx, k_lat, v_lat, *, tq):
    bsz, heads, length, _ = q.shape
    l_ctx = k_ctx.shape[2]
    full = lambda b, h, i: (b, h, 0, 0)
    return pl.pallas_call(
        _attn_kernel,
        grid=(bsz, heads, length // tq),
        in_specs=[
            pl.BlockSpec((1, 1, tq, HEAD_PAD), lambda b, h, i: (b, h, i, 0)),
            pl.BlockSpec((1, 1, l_ctx, HEAD_PAD), full),
            pl.BlockSpec((1, 1, l_ctx, V_HEAD_DIM), full),
            pl.BlockSpec((1, 1, length, HEAD_PAD), full),
            pl.BlockSpec((1, 1, length, V_HEAD_DIM), full),
        ],
        out_specs=pl.BlockSpec((1, tq, V_HEAD_DIM), lambda b, h, i: (b, i, h)),
        out_shape=jax.ShapeDtypeStruct((bsz, length, heads * V_HEAD_DIM), BF16),
        compiler_params=_params("parallel", "parallel", "arbitrary"),
        name="attention",
    )(q, k_ctx, v_ctx, k_lat, v_lat)


def _s5_kernel(u_ref, are_ref, aim_ref, bre_ref, bim_ref, cre_ref, cim_ref, y_ref,
               bu_re, bu_im, h_re, h_im, *, tt, bsz, n_ctx_tiles):
    d = pl.program_id(0)
    i = pl.program_id(1)

    @pl.when(i == 0)
    def _():
        h_re[...] = jnp.zeros_like(h_re)
        h_im[...] = jnp.zeros_like(h_im)

    u = u_ref[...].astype(BF16)
    half = S5_COLS // S5_CHUNKS
    cw = S5_WIDTH // S5_CHUNKS
    for c in range(S5_CHUNKS):
        uc = u[:, c * cw:(c + 1) * cw]
        bu_re[:, c * half:(c + 1) * half] = _dot(uc, bre_ref[0, c])
        bu_im[:, c * half:(c + 1) * half] = _dot(uc, bim_ref[0, c])

    col_w = 512
    for cc in range(S5_COLS // col_w):
        cols = slice(cc * col_w, (cc + 1) * col_w)
        a_r = are_ref[0, :, cols]
        a_i = aim_ref[0, :, cols]

        def step(t, carry, cols=cols, a_r=a_r, a_i=a_i):
            hr, hi = carry
            pos = jnp.where(d == 0, t, tt - 1 - t)
            r = pl.multiple_of(pos * bsz, bsz)
            nr = a_r * hr - a_i * hi + bu_re[pl.ds(r, bsz), cols]
            ni = a_r * hi + a_i * hr + bu_im[pl.ds(r, bsz), cols]
            bu_re[pl.ds(r, bsz), cols] = nr
            bu_im[pl.ds(r, bsz), cols] = ni
            return nr, ni

        hr, hi = lax.fori_loop(0, tt, step, (h_re[:, cols], h_im[:, cols]))
        h_re[:, cols] = hr
        h_im[:, cols] = hi

    @pl.when(i >= n_ctx_tiles)
    def _():
        for c in range(S5_CHUNKS):
            sl = slice(c * half, (c + 1) * half)
            y = _dot(bu_re[:, sl].astype(BF16), cre_ref[0, c]) + _dot(bu_im[:, sl].astype(BF16), cim_ref[0, c])
            y_ref[0, :, c * cw:(c + 1) * cw] = y


def _s5_scan(u_all, mats, *, bsz, l_ctx, tt):
    a_re_b, a_im_b, b_re_bd, b_im_bd, c_re_bd, c_im_bd = mats
    rows = tt * bsz
    n_tiles = u_all.shape[0] // rows
    n_ctx_tiles = l_ctx // tt
    n_lat_tiles = n_tiles - n_ctx_tiles

    def u_map(d, i):
        bwd = jnp.where(i < n_ctx_tiles, n_ctx_tiles - 1 - i, n_tiles - 1 - (i - n_ctx_tiles))
        return (jnp.where(d == 0, i, bwd), 0)

    def y_map(d, i):
        j = jnp.maximum(i - n_ctx_tiles, 0)
        return (d, jnp.where(d == 0, j, n_lat_tiles - 1 - j), 0)

    dmap3 = lambda d, i: (d, 0, 0)
    dmap4 = lambda d, i: (d, 0, 0, 0)
    return pl.pallas_call(
        functools.partial(_s5_kernel, tt=tt, bsz=bsz, n_ctx_tiles=n_ctx_tiles),
        grid=(2, n_tiles),
        in_specs=[
            pl.BlockSpec((rows, S5_WIDTH), u_map),
            pl.BlockSpec((1,) + a_re_b.shape[1:], dmap3),
            pl.BlockSpec((1,) + a_im_b.shape[1:], dmap3),
            pl.BlockSpec((1,) + b_re_bd.shape[1:], dmap4),
            pl.BlockSpec((1,) + b_im_bd.shape[1:], dmap4),
            pl.BlockSpec((1,) + c_re_bd.shape[1:], dmap4),
            pl.BlockSpec((1,) + c_im_bd.shape[1:], dmap4),
        ],
        out_specs=pl.BlockSpec((1, rows, S5_WIDTH), y_map),
        out_shape=jax.ShapeDtypeStruct((2, n_lat_tiles * rows, S5_WIDTH), F32),
        scratch_shapes=[
            pltpu.VMEM((rows, S5_COLS), F32),
            pltpu.VMEM((rows, S5_COLS), F32),
            pltpu.VMEM((bsz, S5_COLS), F32),
            pltpu.VMEM((bsz, S5_COLS), F32),
        ],
        compiler_params=_params("arbitrary", "arbitrary"),
        name="s5_scan",
    )(u_all, a_re_b, a_im_b, b_re_bd, b_im_bd, c_re_bd, c_im_bd)


def _s5_matrices(a_re, a_im, log_dt, b_re, b_im, c_re, c_im, bsz):
    dt = jnp.exp(log_dt)[..., None]
    mag = jnp.exp(a_re * dt)
    ab_re = mag * jnp.cos(a_im * dt)
    ab_im = mag * jnp.sin(a_im * dt)
    den = a_re * a_re + a_im * a_im
    f_re = ((ab_re - 1.0) * a_re + ab_im * a_im) / den
    f_im = (ab_im * a_re - (ab_re - 1.0) * a_im) / den
    bb_re = f_re[..., None] * b_re - f_im[..., None] * b_im
    bb_im = f_re[..., None] * b_im + f_im[..., None] * b_re
    gl = S5_GROUPS // S5_CHUNKS
    eye = jnp.eye(gl, dtype=F32)

    def in_bd(bb):
        t = bb.reshape(2, S5_CHUNKS, gl, S5_STATE, S5_GROUP)
        t = jnp.einsum('dkgpc,gh->dkgchp', t, eye)
        return t.reshape(2, S5_CHUNKS, gl * S5_GROUP, gl * S5_STATE).astype(BF16)

    def out_bd(cc):
        t = cc.reshape(2, S5_CHUNKS, gl, S5_GROUP, S5_STATE)
        t = jnp.einsum('dkgcp,gh->dkgphc', t, eye)
        return t.reshape(2, S5_CHUNKS, gl * S5_STATE, gl * S5_GROUP).astype(BF16)

    bcast = lambda a: jnp.broadcast_to(a.reshape(2, 1, S5_COLS), (2, bsz, S5_COLS))
    return bcast(ab_re), bcast(ab_im), in_bd(bb_re), in_bd(bb_im), out_bd(c_re), out_bd(-c_im)


def _merge_kernel(x_ref, att_ref, gm_ref, gs_ref, su_ref, y_ref, dsk_ref, wglu_ref, wout_ref, ga_ref,
                  l1g_ref, l1b_ref, shf_ref, scf_ref, wq_ref, keys_ref, h1_ref, xm_ref, st_ref):
    y = su_ref[...] * dsk_ref[...] + y_ref[0] + y_ref[1]
    gl = _dot(_gelu(y).astype(BF16), wglu_ref[...])
    s5_out = gl[:, :D_MODEL] * jax.nn.sigmoid(gl[:, D_MODEL:])
    merged = gm_ref[0].astype(F32) * att_ref[0].astype(F32) + gs_ref[0].astype(F32) * s5_out
    out = _dot(merged.astype(BF16), wout_ref[...])
    h1 = _ln_plain(DEEPNORM_ALPHA * x_ref[0] + ga_ref[0] * out) * l1g_ref[...] + l1b_ref[...]
    h1_ref[0] = h1
    xm = (_ln_plain(h1) * (1.0 + scf_ref[0]) + shf_ref[0]).astype(BF16)
    xm_ref[0] = xm
    qp = _dot(xm, wq_ref[...]).astype(BF16)
    for j in range(2 * PEER_HEADS):
        st_ref[j] = _dot_nt(keys_ref[j], qp[:, j * PEER_HALF:(j + 1) * PEER_HALF])


def _merge(x, att, gm, gs, su_all2d, y2d, dsk, wglu, wout, g_a, l1g, l1b, sh_f, sc_f, wq, keys, *, tm, ctx_tiles):
    bsz, length, _ = x.shape
    nt = length // tm
    row = pl.BlockSpec((1, tm, D_MODEL), lambda b, t: (b, t, 0))
    modb = pl.BlockSpec((1, 1, D_MODEL), lambda b, t: (b, 0, 0))
    const = lambda a: pl.BlockSpec(a.shape, lambda b, t: (0,) * a.ndim)
    return pl.pallas_call(
        _merge_kernel,
        grid=(bsz, nt),
        in_specs=[
            row, row, row, row,
            pl.BlockSpec((tm, S5_WIDTH), lambda b, t: (ctx_tiles + t, b)),
            pl.BlockSpec((2, tm, S5_WIDTH), lambda b, t: (0, t, b)),
            const(dsk), const(wglu), const(wout), modb, const(l1g), const(l1b), modb, modb, const(wq), const(keys),
        ],
        out_specs=[row, row, pl.BlockSpec((2 * PEER_HEADS, PEER_N_KEYS, tm), lambda b, t: (0, 0, b * nt + t))],
        out_shape=[
            jax.ShapeDtypeStruct((bsz, length, D_MODEL), F32),
            jax.ShapeDtypeStruct((bsz, length, D_MODEL), BF16),
            jax.ShapeDtypeStruct((2 * PEER_HEADS, PEER_N_KEYS, bsz * length), F32),
        ],
        compiler_params=_params("parallel", "parallel"),
        name="merge_peer_query",
    )(x, att, gm, gs, su_all2d, y2d, dsk, wglu, wout, g_a, l1g, l1b, sh_f, sc_f, wq, keys)


def _take_top(vals, codes, payload, k):
    rows = lax.broadcasted_iota(jnp.int32, (k, vals.shape[1]), 0)
    top_v = jnp.zeros((k, vals.shape[1]), F32)
    top_p = jnp.zeros((k, vals.shape[1]), jnp.int32)
    big = jnp.int32(1 << 30)
    for r in range(k):
        m = jnp.max(vals, axis=0, keepdims=True)
        cm = jnp.min(jnp.where(vals == m, codes, big), axis=0, keepdims=True)
        sel = codes == cm
        if payload is None:
            p = cm
        else:
            p = jnp.max(jnp.where(sel, payload, -1), axis=0, keepdims=True)
        vals = jnp.where(sel, -jnp.inf, vals)
        top_v = jnp.where(rows == r, m, top_v)
        top_p = jnp.where(rows == r, p, top_p)
    return top_v, top_p


def _topk_kernel(st_ref, e_ref, g_ref):
    lanes = st_ref.shape[2]
    key_idx = lax.broadcasted_iota(jnp.int32, (PEER_N_KEYS, lanes), 0)
    sub_idx = lax.broadcasted_iota(jnp.int32, (PEER_TOPK, lanes), 0)
    for h in range(PEER_HEADS):
        sv0, si0 = _take_top(st_ref[2 * h], key_idx, None, PEER_TOPK)
        sv1, si1 = _take_top(st_ref[2 * h + 1], key_idx, None, PEER_TOPK)
        cs, ce, cc = [], [], []
        for i in range(PEER_TOPK):
            nj = PEER_TOPK if i == 0 else PEER_TOPK // 2
            cs.append(sv0[i:i + 1, :] + sv1[0:nj, :])
            ce.append(si0[i:i + 1, :] * PEER_N_KEYS + si1[0:nj, :])
            cc.append(sub_idx[0:nj, :] + i * PEER_TOPK)
        top_s, top_e = _take_top(jnp.concatenate(cs, axis=0), jnp.concatenate(cc, axis=0),
                                 jnp.concatenate(ce, axis=0), PEER_TOPK)
        p = jnp.exp(top_s - jnp.max(top_s, axis=0, keepdims=True))
        g_ref[h] = p / jnp.sum(p, axis=0, keepdims=True)
        e_ref[h] = top_e


def _peer_topk(st, *, lanes):
    n = st.shape[2]
    out_spec = pl.BlockSpec((PEER_HEADS, PEER_TOPK, lanes), lambda i: (0, 0, i))
    return pl.pallas_call(
        _topk_kernel,
        grid=(n // lanes,),
        in_specs=[pl.BlockSpec((2 * PEER_HEADS, PEER_N_KEYS, lanes), lambda i: (0, 0, i))],
        out_specs=[out_spec, out_spec],
        out_shape=[jax.ShapeDtypeStruct((PEER_HEADS, PEER_TOPK, n), jnp.int32),
                   jax.ShapeDtypeStruct((PEER_HEADS, PEER_TOPK, n), F32)],
        compiler_params=_params("parallel"),
        name="peer_topk",
    )(st)


def _sc_gather(table, idx, *, win):
    n_idx = idx.shape[0]
    width = table.shape[1]
    per_worker = n_idx // SC_WORKERS
    n_win = per_worker // win
    mesh = plsc.VectorSubcoreMesh(core_axis_name="c", subcore_axis_name="s")

    @functools.partial(
        pl.kernel, mesh=mesh,
        out_type=jax.ShapeDtypeStruct((n_idx, width), table.dtype),
        scratch_types=[pltpu.VMEM((win,), jnp.int32), pltpu.VMEM((win, width), table.dtype), pltpu.SemaphoreType.DMA],
    )
    def gather(table_hbm, idx_hbm, out_hbm, idx_v, rows_v, sem):
        wid = lax.axis_index("s") * 2 + lax.axis_index("c")
        base = wid * per_worker

        @pl.loop(0, n_win)
        def _(j):
            off = pl.multiple_of(base + j * win, 8)
            pltpu.sync_copy(idx_hbm.at[pl.ds(off, win)], idx_v)
            pltpu.async_copy(table_hbm.at[idx_v], rows_v, sem).wait()
            pltpu.sync_copy(rows_v, out_hbm.at[pl.ds(off, win)])

    return gather(table, idx)


def _peer_kernel(ug_ref, vg_ref, xm_ref, w2_ref, h1_ref, gf_ref, l2g_ref, l2b_ref, o_ref, f_ref, *, tb):
    sub = 8
    span = sub * PEER_PAIRS
    pair_w = 2 * PEER_PAIRS
    half = D_MODEL // 2
    lane = lax.broadcasted_iota(jnp.int32, (sub, 2 * span), 1)
    diag = (lane // pair_w) == lax.broadcasted_iota(jnp.int32, (sub, 2 * span), 0)
    even = (lane % 2) == 0
    even_p = (lax.broadcasted_iota(jnp.int32, (sub, pair_w), 1) % 2) == 0
    for s in range(tb // sub):
        rows = slice(s * span, (s + 1) * span)
        toks = slice(s * sub, (s + 1) * sub)
        x8 = xm_ref[toks, :].astype(F32)
        x2 = jnp.concatenate([x8[:, :half], x8[:, half:]], axis=0).astype(BF16)
        d = _dot_nt(x2, pltpu.bitcast(ug_ref[rows, :], BF16))
        z = jnp.where(diag, jnp.where(even, d[:sub], d[sub:]), 0.0)
        zd = z[:, 0:pair_w]
        for t in range(1, sub):
            zd = zd + z[:, t * pair_w:(t + 1) * pair_w]
        act = zd + jnp.where(even_p, pltpu.roll(zd, pair_w - 1, axis=1), pltpu.roll(zd, 1, axis=1))
        coef = w2_ref[toks, :] * _gelu(act)
        cdiag = jnp.where(diag, jnp.concatenate([coef] * sub, axis=1), 0.0)
        c2 = jnp.concatenate([jnp.where(even, cdiag, 0.0), jnp.where(even, 0.0, cdiag)], axis=0).astype(BF16)
        f2 = _dot(c2, pltpu.bitcast(vg_ref[rows, :], BF16))
        f_ref[toks, :half] = f2[:sub]
        f_ref[toks, half:] = f2[sub:]
    r = DEEPNORM_ALPHA * h1_ref[...] + gf_ref[0] * f_ref[...]
    o_ref[...] = _ln_plain(r) * l2g_ref[...] + l2b_ref[...]


def _peer_eval(ug, vg, xm, w, h1, g_f, l2g, l2b, *, tb, tok0, length):
    n = xm.shape[0]
    tok_blk0 = tok0 // tb
    per_b = length // tb
    tok_spec = lambda width: pl.BlockSpec((tb, width), lambda i: (i, 0))
    return pl.pallas_call(
        functools.partial(_peer_kernel, tb=tb),
        grid=(n // tb,),
        in_specs=[
            pl.BlockSpec((tb * PEER_PAIRS, D_MODEL // 2), lambda i: (i, 0)),
            pl.BlockSpec((tb * PEER_PAIRS, D_MODEL // 2), lambda i: (i, 0)),
            tok_spec(D_MODEL), tok_spec(2 * PEER_PAIRS), tok_spec(D_MODEL),
            pl.BlockSpec((1, 1, D_MODEL), lambda i: ((tok_blk0 + i) // per_b, 0, 0)),
            pl.BlockSpec((1, D_MODEL), lambda i: (0, 0)),
            pl.BlockSpec((1, D_MODEL), lambda i: (0, 0)),
        ],
        out_specs=tok_spec(D_MODEL),
        out_shape=jax.ShapeDtypeStruct((n, D_MODEL), F32),
        scratch_shapes=[pltpu.VMEM((tb, D_MODEL), F32)],
        compiler_params=_params("parallel"),
        name="peer_eval",
    )(ug, vg, xm, w, h1, g_f, l2g, l2b)


def _rope_perm():
    half = ROPE_AXIS_DIM // 2
    base = jnp.arange(QK_ROPE_DIM)
    return jnp.where((base % ROPE_AXIS_DIM) < half, base + half, base - half)


def _rope_tables(length):
    pos = jnp.arange(length)
    row = (pos // GRID_W).astype(F32)
    col = (pos % GRID_W).astype(F32)
    inv_freq = jnp.power(ROPE_BASE, -jnp.arange(0, ROPE_AXIS_DIM, 2, dtype=F32) / ROPE_AXIS_DIM)
    ang_r = row[:, None] * inv_freq
    ang_c = col[:, None] * inv_freq
    zeros = jnp.zeros((length, QK_ROPE_DIM), F32)
    c_tab = jnp.concatenate([jnp.cos(ang_r), jnp.cos(ang_r), jnp.cos(ang_c), jnp.cos(ang_c), zeros], axis=1)
    s_tab = jnp.concatenate([-jnp.sin(ang_r), jnp.sin(ang_r), -jnp.sin(ang_c), jnp.sin(ang_c), zeros], axis=1)
    return c_tab, s_tab


def _identity_tables(length):
    ones = jnp.ones((length, QK_ROPE_DIM), F32)
    zeros = jnp.zeros((length, QK_ROPE_DIM), F32)
    return jnp.concatenate([ones, zeros], axis=1), jnp.zeros((length, 2 * QK_ROPE_DIM), F32)


def _pack_rows(table):
    bits = lax.bitcast_convert_type(table.astype(BF16), jnp.uint16).astype(jnp.uint32)
    half = table.shape[1] // 2
    return bits[:, :half] | (bits[:, half:] << 16)


def _layer_weights(w_in, q_norm_g, kv_norm_g, w_uq, w_ukv):
    perm = _rope_perm()
    p0 = Q_RANK
    p1 = p0 + KV_RANK
    p2 = p1 + QK_ROPE_DIM
    p3 = p2 + S5_WIDTH
    p4 = p3 + D_MODEL
    w_kpe = w_in[:, p1:p2]
    wckv = jnp.concatenate([w_in[:, p0:p1], w_kpe, w_kpe[:, perm]], axis=1)
    uq = w_uq.reshape(Q_RANK, N_HEADS, QK_NOPE_DIM + QK_ROPE_DIM)
    pe = uq[:, :, QK_NOPE_DIM:]
    wuq = jnp.concatenate([uq[:, :, :QK_NOPE_DIM], pe, pe[:, :, perm]], axis=2).reshape(Q_RANK, N_HEADS * HEAD_PAD)
    cast = lambda a: a.astype(BF16)
    return (cast(w_in[:, :p0]), cast(wckv), cast(w_in[:, p2:p3]), cast(w_in[:, p3:p4]), cast(w_in[:, p4:]),
            q_norm_g.reshape(1, Q_RANK), kv_norm_g.reshape(1, KV_RANK), cast(wuq), cast(w_ukv))


def kernel(x, c, ctx, c_ctx, w_mod, b_mod, w_in, q_norm_g, kv_norm_g, w_uq, w_ukv, s5_a_re, s5_a_im, s5_log_dt, s5_b_re, s5_b_im, s5_c_re, s5_c_im, s5_d, w_glu, w_out, ln1_g, ln1_b, peer_wq, peer_keys, peer_u, peer_v, ln2_g, ln2_b):
    bsz, length, _ = x.shape
    l_ctx = ctx.shape[1]
    n_tok = bsz * length
    layer = 0
    tm = 256
    s5_tt = 64
    tok_chunk = 4096
    peer_tb = 16

    cond = jnp.concatenate([c, c_ctx[None, :], jnp.zeros((7, D_MODEL), F32)], axis=0)
    mod = _modulation(cond, w_mod[layer], b_mod[layer]).reshape(cond.shape[0], N_MOD, 1, D_MODEL)
    sh_a, sc_a, g_a, sh_f, sc_f, g_f = (mod[:bsz, i] for i in range(N_MOD))
    csh_a, csc_a = mod[bsz:bsz + 1, 0], mod[bsz:bsz + 1, 1]

    wts = _layer_weights(w_in[layer], q_norm_g[layer], kv_norm_g[layer], w_uq[layer], w_ukv[layer])
    c_lat, s_lat = _rope_tables(length)
    c_id, s_id = _identity_tables(l_ctx)
    k_ctx, v_ctx, su_ctx = _inproj(ctx, csh_a, csc_a, c_id, s_id, wts, is_ctx=True, tm=min(tm, l_ctx))
    q, k_lat, v_lat, su_lat, gm, gs = _inproj(x, sh_a, sc_a, c_lat, s_lat, wts, is_ctx=False, tm=tm)

    att = _attention(q, k_ctx, v_ctx, k_lat, v_lat, tq=min(512, length))

    su_all = jnp.concatenate([su_ctx, su_lat], axis=0)
    mats = _s5_matrices(s5_a_re[layer], s5_a_im[layer], s5_log_dt[layer], s5_b_re[layer], s5_b_im[layer],
                        s5_c_re[layer], s5_c_im[layer], bsz)
    y = _s5_scan(su_all.reshape(-1, S5_WIDTH), mats, bsz=bsz, l_ctx=l_ctx, tt=s5_tt)
    y2d = y.reshape(2, length, bsz * S5_WIDTH)

    keys = peer_keys[layer].reshape(2 * PEER_HEADS, PEER_N_KEYS, PEER_HALF).astype(BF16)
    h1, xm, st = _merge(x, att, gm, gs, su_all, y2d, s5_d[layer].reshape(1, S5_WIDTH),
                        w_glu[layer].astype(BF16), w_out[layer].astype(BF16), g_a,
                        ln1_g[layer].reshape(1, D_MODEL), ln1_b[layer].reshape(1, D_MODEL), sh_f, sc_f,
                        peer_wq[layer].astype(BF16), keys, tm=tm, ctx_tiles=l_ctx // tm)

    experts, gates = _peer_topk(st, lanes=128)
    idx = experts.transpose(2, 0, 1).reshape(n_tok * PEER_PAIRS)
    w2 = jnp.repeat(gates.transpose(2, 0, 1).reshape(n_tok, PEER_PAIRS), 2, axis=1)
    u_pack = _pack_rows(peer_u[layer])
    v_pack = _pack_rows(peer_v[layer])

    h1f = h1.reshape(n_tok, D_MODEL)
    xmf = xm.reshape(n_tok, D_MODEL)
    l2g = ln2_g[layer].reshape(1, D_MODEL)
    l2b = ln2_b[layer].reshape(1, D_MODEL)
    tok_chunk = min(tok_chunk, n_tok)
    outs = []
    for t0 in range(0, n_tok, tok_chunk):
        ids = idx[t0 * PEER_PAIRS:(t0 + tok_chunk) * PEER_PAIRS]
        ug = _sc_gather(u_pack, ids, win=64)
        vg = _sc_gather(v_pack, ids, win=64)
        outs.append(_peer_eval(ug, vg, xmf[t0:t0 + tok_chunk], w2[t0:t0 + tok_chunk], h1f[t0:t0 + tok_chunk],
                               g_f, l2g, l2b, tb=peer_tb, tok0=t0, length=length))
    return jnp.concatenate(outs, axis=0).reshape(bsz, length, D_MODEL)
```

```python
import functools
import math

import jax
import jax.numpy as jnp
from jax import lax
from jax.experimental import pallas as pl
from jax.experimental.pallas import tpu as pltpu
from jax.experimental.pallas import tpu_sc as plsc

F32 = jnp.float32
BF16 = jnp.bfloat16

D_MODEL = 1024
DEPTH = 1
GRID_W = 64
N_HEADS = 8
QK_NOPE_DIM = 128
QK_ROPE_DIM = 64
V_HEAD_DIM = 128
Q_RANK = 384
KV_RANK = 256
ROPE_AXIS_DIM = QK_ROPE_DIM // 2
ROPE_BASE = 10000.0
S5_WIDTH = D_MODEL // 2
S5_GROUP = 16
S5_GROUPS = S5_WIDTH // S5_GROUP
S5_STATE = 64
PEER_HEADS = 8
PEER_N_KEYS = 128
PEER_TOPK = 16
PEER_HALF = 128
PEER_PAIRS = PEER_HEADS * PEER_TOPK
DEEPNORM_ALPHA = (2.0 * DEPTH) ** 0.25
LN_EPS = 1e-6
N_MOD = 6
ATT_SCALE = (QK_NOPE_DIM + QK_ROPE_DIM) ** -0.5

HEAD_PAD = 256
S5_COLS = S5_GROUPS * S5_STATE
S5_CHUNKS = 2
SC_WORKERS = 32
SC_LANES = 16
DOT_WIN = 32
HALF_W = D_MODEL // 2
VMEM_LIMIT = 48 * 1024 * 1024


def _dot(a, b):
    return jnp.dot(a, b, preferred_element_type=F32)


def _dot_nt(a, b):
    return lax.dot_general(a, b, (((1,), (1,)), ((), ())), preferred_element_type=F32)


def _gelu(x):
    return 0.5 * x * (1.0 + jnp.tanh(0.7978845608028654 * (x + 0.044715 * (x * x * x))))


def _ln_plain(x):
    mu = jnp.mean(x, axis=-1, keepdims=True)
    xc = x - mu
    var = jnp.mean(xc * xc, axis=-1, keepdims=True)
    return xc * lax.rsqrt(var + LN_EPS)


def _rms(x, g):
    return x * lax.rsqrt(jnp.mean(x * x, axis=-1, keepdims=True) + LN_EPS) * g


def _params(*sem):
    return pltpu.CompilerParams(dimension_semantics=sem, vmem_limit_bytes=VMEM_LIMIT)


def _mod_kernel(cond_ref, w_ref, b_ref, o_ref):
    a = cond_ref[...]
    a = a * jax.nn.sigmoid(a)
    a_hi = a.astype(BF16)
    a_lo = (a - a_hi.astype(F32)).astype(BF16)
    w = w_ref[...]
    w_hi = w.astype(BF16)
    w_lo = (w - w_hi.astype(F32)).astype(BF16)
    o_ref[...] = _dot(a_hi, w_hi) + _dot(a_lo, w_hi) + _dot(a_hi, w_lo) + b_ref[...]


def _modulation(cond, w_mod, b_mod):
    rows = cond.shape[0]
    n = w_mod.shape[1]
    blk = D_MODEL
    return pl.pallas_call(
        _mod_kernel,
        grid=(n // blk,),
        in_specs=[
            pl.BlockSpec((rows, D_MODEL), lambda j: (0, 0)),
            pl.BlockSpec((D_MODEL, blk), lambda j: (0, j)),
            pl.BlockSpec((1, blk), lambda j: (0, j)),
        ],
        out_specs=pl.BlockSpec((rows, blk), lambda j: (0, j)),
        out_shape=jax.ShapeDtypeStruct((rows, n), F32),
        compiler_params=_params("arbitrary"),
        name="modulation",
    )(cond, w_mod, b_mod.reshape(1, n))


def _rope128(t, c_tab, s_tab):
    return t * c_tab + pltpu.roll(t, 64, axis=1) * s_tab


def _inproj_kernel(x_ref, sh_ref, sc_ref, c_ref, s_ref, wcq_ref, wckv_ref, wsu_ref, wgm_ref, wgs_ref,
                   qg_ref, kvg_ref, wuq_ref, wukv_ref, *out_refs, is_ctx):
    if is_ctx:
        k_ref, v_ref, su_ref = out_refs
    else:
        q_ref, k_ref, v_ref, su_ref, gm_ref, gs_ref = out_refs
    xm = _ln_plain(x_ref[0]) * (1.0 + sc_ref[0]) + sh_ref[0]
    xb = xm.astype(BF16)
    c_tab = c_ref[...]
    s_tab = s_ref[...]

    ckvpe = _dot(xb, wckv_ref[...])
    ckv_n = _rms(ckvpe[:, :KV_RANK], kvg_ref[...]).astype(BF16)
    kpe = _rope128(ckvpe[:, KV_RANK:], c_tab, s_tab).astype(BF16)
    kv = _dot(ckv_n, wukv_ref[...])
    for h in range(N_HEADS):
        k_ref[0, h, :, 0:128] = kv[:, h * 256:h * 256 + 128].astype(BF16)
        k_ref[0, h, :, 128:256] = kpe
        v_ref[0, h] = kv[:, h * 256 + 128:(h + 1) * 256].astype(BF16)

    su_ref[...] = _dot(xb, wsu_ref[...])

    if not is_ctx:
        cq_n = _rms(_dot(xb, wcq_ref[...]), qg_ref[...]).astype(BF16)
        q = _dot(cq_n, wuq_ref[...])
        for h in range(N_HEADS):
            q_ref[0, h, :, 0:128] = (q[:, h * 256:h * 256 + 128] * ATT_SCALE).astype(BF16)
            q_ref[0, h, :, 128:256] = (_rope128(q[:, h * 256 + 128:(h + 1) * 256], c_tab, s_tab) * ATT_SCALE).astype(BF16)
        gm_ref[0] = jax.nn.sigmoid(_dot(xb, wgm_ref[...])).astype(BF16)
        gs_ref[0] = jax.nn.sigmoid(_dot(xb, wgs_ref[...])).astype(BF16)


def _inproj(x, shift, scale, c_tab, s_tab, wts, *, is_ctx, tm):
    bsz, length, _ = x.shape
    mod_map = (lambda b, t: (0, 0, 0)) if shift.shape[0] == 1 else (lambda b, t: (b, 0, 0))
    const2 = lambda b, t: (0, 0)
    wcq, wckv, wsu, wgm, wgs, qg, kvg, wuq, wukv = wts
    in_specs = [
        pl.BlockSpec((1, tm, D_MODEL), lambda b, t: (b, t, 0)),
        pl.BlockSpec((1, 1, D_MODEL), mod_map),
        pl.BlockSpec((1, 1, D_MODEL), mod_map),
        pl.BlockSpec((tm, 128), lambda b, t: (t, 0)),
        pl.BlockSpec((tm, 128), lambda b, t: (t, 0)),
    ] + [pl.BlockSpec(w.shape, const2) for w in (wcq, wckv, wsu, wgm, wgs, qg, kvg, wuq, wukv)]
    head_spec = lambda width: pl.BlockSpec((1, N_HEADS, tm, width), lambda b, t: (b, 0, t, 0))
    row_spec = pl.BlockSpec((1, tm, D_MODEL), lambda b, t: (b, t, 0))
    su_spec = pl.BlockSpec((tm, S5_WIDTH), lambda b, t: (t, b))
    k_shape = jax.ShapeDtypeStruct((bsz, N_HEADS, length, HEAD_PAD), BF16)
    v_shape = jax.ShapeDtypeStruct((bsz, N_HEADS, length, V_HEAD_DIM), BF16)
    su_shape = jax.ShapeDtypeStruct((length, bsz * S5_WIDTH), F32)
    g_shape = jax.ShapeDtypeStruct((bsz, length, D_MODEL), BF16)
    if is_ctx:
        out_specs = [head_spec(HEAD_PAD), head_spec(V_HEAD_DIM), su_spec]
        out_shape = [k_shape, v_shape, su_shape]
    else:
        out_specs = [head_spec(HEAD_PAD), head_spec(HEAD_PAD), head_spec(V_HEAD_DIM), su_spec, row_spec, row_spec]
        out_shape = [k_shape, k_shape, v_shape, su_shape, g_shape, g_shape]
    return pl.pallas_call(
        functools.partial(_inproj_kernel, is_ctx=is_ctx),
        grid=(bsz, length // tm),
        in_specs=in_specs,
        out_specs=out_specs,
        out_shape=out_shape,
        compiler_params=_params("parallel", "parallel"),
        name="inproj_ctx" if is_ctx else "inproj_lat",
    )(x, shift, scale, c_tab, s_tab, wcq, wckv, wsu, wgm, wgs, qg, kvg, wuq, wukv)


def _attn_kernel(q_ref, kc_ref, vc_ref, kl_ref, vl_ref, o_ref):
    q = q_ref[0, 0]
    s_c = _dot_nt(q, kc_ref[0, 0])
    s_l = _dot_nt(q, kl_ref[0, 0])
    m = jnp.maximum(jnp.max(s_c, axis=-1, keepdims=True), jnp.max(s_l, axis=-1, keepdims=True))
    p_c = jnp.exp(s_c - m)
    p_l = jnp.exp(s_l - m)
    denom = jnp.sum(p_c, axis=-1, keepdims=True) + jnp.sum(p_l, axis=-1, keepdims=True)
    o = _dot(p_c.astype(BF16), vc_ref[0, 0]) + _dot(p_l.astype(BF16), vl_ref[0, 0])
    o_ref[0] = (o / denom).astype(BF16)


def _attention(q, k_ctx, v_ctx, k_lat, v_lat, *, tq):
    bsz, heads, length, _ = q.shape
    l_ctx = k_ctx.shape[2]
    full = lambda b, h, i: (b, h, 0, 0)
    return pl.pallas_call(
        _attn_kernel,
        grid=(bsz, heads, length // tq),
        in_specs=[
            pl.BlockSpec((1, 1, tq, HEAD_PAD), lambda b, h, i: (b, h, i, 0)),
            pl.BlockSpec((1, 1, l_ctx, HEAD_PAD), full),
            pl.BlockSpec((1, 1, l_ctx, V_HEAD_DIM), full),
            pl.BlockSpec((1, 1, length, HEAD_PAD), full),
            pl.BlockSpec((1, 1, length, V_HEAD_DIM), full),
        ],
        out_specs=pl.BlockSpec((1, tq, V_HEAD_DIM), lambda b, h, i: (b, i, h)),
        out_shape=jax.ShapeDtypeStruct((bsz, length, heads * V_HEAD_DIM), BF16),
        compiler_params=_params("parallel", "parallel", "arbitrary"),
        name="attention",
    )(q, k_ctx, v_ctx, k_lat, v_lat)


def _s5_kernel(u_ref, are_ref, aim_ref, bre_ref, bim_ref, cre_ref, cim_ref, y_ref,
               bu_re, bu_im, h_re, h_im, *, tt, bsz, n_ctx_tiles):
    d = pl.program_id(0)
    i = pl.program_id(1)

    @pl.when(i == 0)
    def _():
        h_re[...] = jnp.zeros_like(h_re)
        h_im[...] = jnp.zeros_like(h_im)

    u = u_ref[...].astype(BF16)
    half = S5_COLS // S5_CHUNKS
    cw = S5_WIDTH // S5_CHUNKS
    for c in range(S5_CHUNKS):
        uc = u[:, c * cw:(c + 1) * cw]
        bu_re[:, c * half:(c + 1) * half] = _dot(uc, bre_ref[0, c])
        bu_im[:, c * half:(c + 1) * half] = _dot(uc, bim_ref[0, c])

    col_w = 512
    for cc in range(S5_COLS // col_w):
        cols = slice(cc * col_w, (cc + 1) * col_w)
        a_r = are_ref[0, :, cols]
        a_i = aim_ref[0, :, cols]

        def step(t, carry, cols=cols, a_r=a_r, a_i=a_i):
            hr, hi = carry
            pos = jnp.where(d == 0, t, tt - 1 - t)
            r = pl.multiple_of(pos * bsz, bsz)
            nr = a_r * hr - a_i * hi + bu_re[pl.ds(r, bsz), cols]
            ni = a_r * hi + a_i * hr + bu_im[pl.ds(r, bsz), cols]
            bu_re[pl.ds(r, bsz), cols] = nr
            bu_im[pl.ds(r, bsz), cols] = ni
            return nr, ni

        hr, hi = lax.fori_loop(0, tt, step, (h_re[:, cols], h_im[:, cols]))
        h_re[:, cols] = hr
        h_im[:, cols] = hi

    @pl.when(i >= n_ctx_tiles)
    def _():
        for c in range(S5_CHUNKS):
            sl = slice(c * half, (c + 1) * half)
            y = _dot(bu_re[:, sl].astype(BF16), cre_ref[0, c]) + _dot(bu_im[:, sl].astype(BF16), cim_ref[0, c])
            y_ref[0, :, c * cw:(c + 1) * cw] = y


def _s5_scan(u_all, mats, *, bsz, l_ctx, tt):
    a_re_b, a_im_b, b_re_bd, b_im_bd, c_re_bd, c_im_bd = mats
    rows = tt * bsz
    n_tiles = u_all.shape[0] // rows
    n_ctx_tiles = l_ctx // tt
    n_lat_tiles = n_tiles - n_ctx_tiles

    def u_map(d, i):
        bwd = jnp.where(i < n_ctx_tiles, n_ctx_tiles - 1 - i, n_tiles - 1 - (i - n_ctx_tiles))
        return (jnp.where(d == 0, i, bwd), 0)

    def y_map(d, i):
        j = jnp.maximum(i - n_ctx_tiles, 0)
        return (d, jnp.where(d == 0, j, n_lat_tiles - 1 - j), 0)

    dmap3 = lambda d, i: (d, 0, 0)
    dmap4 = lambda d, i: (d, 0, 0, 0)
    return pl.pallas_call(
        functools.partial(_s5_kernel, tt=tt, bsz=bsz, n_ctx_tiles=n_ctx_tiles),
        grid=(2, n_tiles),
        in_specs=[
            pl.BlockSpec((rows, S5_WIDTH), u_map),
            pl.BlockSpec((1,) + a_re_b.shape[1:], dmap3),
            pl.BlockSpec((1,) + a_im_b.shape[1:], dmap3),
            pl.BlockSpec((1,) + b_re_bd.shape[1:], dmap4),
            pl.BlockSpec((1,) + b_im_bd.shape[1:], dmap4),
            pl.BlockSpec((1,) + c_re_bd.shape[1:], dmap4),
            pl.BlockSpec((1,) + c_im_bd.shape[1:], dmap4),
        ],
        out_specs=pl.BlockSpec((1, rows, S5_WIDTH), y_map),
        out_shape=jax.ShapeDtypeStruct((2, n_lat_tiles * rows, S5_WIDTH), F32),
        scratch_shapes=[
            pltpu.VMEM((rows, S5_COLS), F32),
            pltpu.VMEM((rows, S5_COLS), F32),
            pltpu.VMEM((bsz, S5_COLS), F32),
            pltpu.VMEM((bsz, S5_COLS), F32),
        ],
        compiler_params=_params("arbitrary", "arbitrary"),
        name="s5_scan",
    )(u_all, a_re_b, a_im_b, b_re_bd, b_im_bd, c_re_bd, c_im_bd)


def _s5_matrices(a_re, a_im, log_dt, b_re, b_im, c_re, c_im, bsz):
    dt = jnp.exp(log_dt)[..., None]
    mag = jnp.exp(a_re * dt)
    ab_re = mag * jnp.cos(a_im * dt)
    ab_im = mag * jnp.sin(a_im * dt)
    den = a_re * a_re + a_im * a_im
    f_re = ((ab_re - 1.0) * a_re + ab_im * a_im) / den
    f_im = (ab_im * a_re - (ab_re - 1.0) * a_im) / den
    bb_re = f_re[..., None] * b_re - f_im[..., None] * b_im
    bb_im = f_re[..., None] * b_im + f_im[..., None] * b_re
    gl = S5_GROUPS // S5_CHUNKS
    eye = jnp.eye(gl, dtype=F32)

    def in_bd(bb):
        t = bb.reshape(2, S5_CHUNKS, gl, S5_STATE, S5_GROUP)
        t = jnp.einsum('dkgpc,gh->dkgchp', t, eye)
        return t.reshape(2, S5_CHUNKS, gl * S5_GROUP, gl * S5_STATE).astype(BF16)

    def out_bd(cc):
        t = cc.reshape(2, S5_CHUNKS, gl, S5_GROUP, S5_STATE)
        t = jnp.einsum('dkgcp,gh->dkgphc', t, eye)
        return t.reshape(2, S5_CHUNKS, gl * S5_STATE, gl * S5_GROUP).astype(BF16)

    bcast = lambda a: jnp.broadcast_to(a.reshape(2, 1, S5_COLS), (2, bsz, S5_COLS))
    return bcast(ab_re), bcast(ab_im), in_bd(bb_re), in_bd(bb_im), out_bd(c_re), out_bd(-c_im)


def _merge_kernel(x_ref, att_ref, gm_ref, gs_ref, su_ref, y_ref, dsk_ref, wglu_ref, wout_ref, ga_ref,
                  l1g_ref, l1b_ref, shf_ref, scf_ref, wq_ref, keys_ref, h1_ref, xm_ref, st_ref):
    y = su_ref[...] * dsk_ref[...] + y_ref[0] + y_ref[1]
    gl = _dot(_gelu(y).astype(BF16), wglu_ref[...])
    s5_out = gl[:, :D_MODEL] * jax.nn.sigmoid(gl[:, D_MODEL:])
    merged = gm_ref[0].astype(F32) * att_ref[0].astype(F32) + gs_ref[0].astype(F32) * s5_out
    out = _dot(merged.astype(BF16), wout_ref[...])
    h1 = _ln_plain(DEEPNORM_ALPHA * x_ref[0] + ga_ref[0] * out) * l1g_ref[...] + l1b_ref[...]
    h1_ref[0] = h1
    xm = _ln_plain(h1) * (1.0 + scf_ref[0]) + shf_ref[0]
    xm_ref[0] = xm
    qp = _dot(xm.astype(BF16), wq_ref[...]).astype(BF16)
    for j in range(2 * PEER_HEADS):
        st_ref[j] = _dot_nt(keys_ref[j], qp[:, j * PEER_HALF:(j + 1) * PEER_HALF])


def _merge(x, att, gm, gs, su_all2d, y2d, dsk, wglu, wout, g_a, l1g, l1b, sh_f, sc_f, wq, keys, *, tm, ctx_tiles):
    bsz, length, _ = x.shape
    nt = length // tm
    row = pl.BlockSpec((1, tm, D_MODEL), lambda b, t: (b, t, 0))
    modb = pl.BlockSpec((1, 1, D_MODEL), lambda b, t: (b, 0, 0))
    const = lambda a: pl.BlockSpec(a.shape, lambda b, t: (0,) * a.ndim)
    return pl.pallas_call(
        _merge_kernel,
        grid=(bsz, nt),
        in_specs=[
            row, row, row, row,
            pl.BlockSpec((tm, S5_WIDTH), lambda b, t: (ctx_tiles + t, b)),
            pl.BlockSpec((2, tm, S5_WIDTH), lambda b, t: (0, t, b)),
            const(dsk), const(wglu), const(wout), modb, const(l1g), const(l1b), modb, modb, const(wq), const(keys),
        ],
        out_specs=[row, row, pl.BlockSpec((2 * PEER_HEADS, PEER_N_KEYS, tm), lambda b, t: (0, 0, b * nt + t))],
        out_shape=[
            jax.ShapeDtypeStruct((bsz, length, D_MODEL), F32),
            jax.ShapeDtypeStruct((bsz, length, D_MODEL), F32),
            jax.ShapeDtypeStruct((2 * PEER_HEADS, PEER_N_KEYS, bsz * length), F32),
        ],
        compiler_params=_params("parallel", "parallel"),
        name="merge_peer_query",
    )(x, att, gm, gs, su_all2d, y2d, dsk, wglu, wout, g_a, l1g, l1b, sh_f, sc_f, wq, keys)


def _take_top(vals, codes, payload, k):
    rows = lax.broadcasted_iota(jnp.int32, (k, vals.shape[1]), 0)
    top_v = jnp.zeros((k, vals.shape[1]), F32)
    top_p = jnp.zeros((k, vals.shape[1]), jnp.int32)
    big = jnp.int32(1 << 30)
    for r in range(k):
        m = jnp.max(vals, axis=0, keepdims=True)
        cm = jnp.min(jnp.where(vals == m, codes, big), axis=0, keepdims=True)
        sel = codes == cm
        if payload is None:
            p = cm
        else:
            p = jnp.max(jnp.where(sel, payload, -1), axis=0, keepdims=True)
        vals = jnp.where(sel, -jnp.inf, vals)
        top_v = jnp.where(rows == r, m, top_v)
        top_p = jnp.where(rows == r, p, top_p)
    return top_v, top_p


def _topk_kernel(st_ref, e_ref, g_ref):
    lanes = st_ref.shape[2]
    key_idx = lax.broadcasted_iota(jnp.int32, (PEER_N_KEYS, lanes), 0)
    sub_idx = lax.broadcasted_iota(jnp.int32, (PEER_TOPK, lanes), 0)
    for h in range(PEER_HEADS):
        sv0, si0 = _take_top(st_ref[2 * h], key_idx, None, PEER_TOPK)
        sv1, si1 = _take_top(st_ref[2 * h + 1], key_idx, None, PEER_TOPK)
        cs, ce, cc = [], [], []
        for i in range(PEER_TOPK):
            nj = PEER_TOPK if i == 0 else PEER_TOPK // 2
            cs.append(sv0[i:i + 1, :] + sv1[0:nj, :])
            ce.append(si0[i:i + 1, :] * PEER_N_KEYS + si1[0:nj, :])
            cc.append(sub_idx[0:nj, :] + i * PEER_TOPK)
        top_s, top_e = _take_top(jnp.concatenate(cs, axis=0), jnp.concatenate(cc, axis=0),
                                 jnp.concatenate(ce, axis=0), PEER_TOPK)
        p = jnp.exp(top_s - jnp.max(top_s, axis=0, keepdims=True))
        g_ref[h] = p / jnp.sum(p, axis=0, keepdims=True)
        e_ref[h] = top_e


def _peer_topk(st, *, lanes):
    n = st.shape[2]
    out_spec = pl.BlockSpec((PEER_HEADS, PEER_TOPK, lanes), lambda i: (0, 0, i))
    return pl.pallas_call(
        _topk_kernel,
        grid=(n // lanes,),
        in_specs=[pl.BlockSpec((2 * PEER_HEADS, PEER_N_KEYS, lanes), lambda i: (0, 0, i))],
        out_specs=[out_spec, out_spec],
        out_shape=[jax.ShapeDtypeStruct((PEER_HEADS, PEER_TOPK, n), jnp.int32),
                   jax.ShapeDtypeStruct((PEER_HEADS, PEER_TOPK, n), F32)],
        compiler_params=_params("parallel"),
        name="peer_topk",
    )(st)


def _sc_gather(table, idx, *, win):
    n_idx = idx.shape[0]
    width = table.shape[1]
    per_worker = n_idx // SC_WORKERS
    n_win = per_worker // win
    mesh = plsc.VectorSubcoreMesh(core_axis_name="c", subcore_axis_name="s")

    @functools.partial(
        pl.kernel, mesh=mesh,
        out_type=jax.ShapeDtypeStruct((n_idx, width), table.dtype),
        scratch_types=[pltpu.VMEM((win,), jnp.int32), pltpu.VMEM((win, width), table.dtype), pltpu.SemaphoreType.DMA],
    )
    def gather(table_hbm, idx_hbm, out_hbm, idx_v, rows_v, sem):
        wid = lax.axis_index("s") * 2 + lax.axis_index("c")
        base = wid * per_worker

        @pl.loop(0, n_win)
        def _(j):
            off = pl.multiple_of(base + j * win, 8)
            pltpu.sync_copy(idx_hbm.at[pl.ds(off, win)], idx_v)
            pltpu.async_copy(table_hbm.at[idx_v], rows_v, sem).wait()
            pltpu.sync_copy(rows_v, out_hbm.at[pl.ds(off, win)])

    return gather(table, idx)


def _sc_dots(table, idx, x):
    n_tok = x.shape[0]
    per_worker = n_tok // SC_WORKERS
    n_win = PEER_PAIRS // DOT_WIN
    groups = DOT_WIN // SC_LANES
    mesh = plsc.VectorSubcoreMesh(core_axis_name="c", subcore_axis_name="s")

    @functools.partial(
        pl.kernel, mesh=mesh,
        out_type=jax.ShapeDtypeStruct((n_tok * PEER_PAIRS,), F32),
        scratch_types=[
            pltpu.VMEM((2, PEER_PAIRS), jnp.int32),
            pltpu.VMEM((2, 2 * HALF_W), F32),
            pltpu.VMEM((n_win, DOT_WIN, HALF_W), jnp.int32),
            pltpu.VMEM((2, PEER_PAIRS), F32),
            pltpu.SemaphoreType.DMA((n_win,)),
            pltpu.SemaphoreType.DMA((2,)),
            pltpu.SemaphoreType.DMA((2,)),
            pltpu.SemaphoreType.DMA((2,)),
        ],
        compiler_params=pltpu.CompilerParams(needs_layout_passes=False),
    )
    def dots(table_hbm, idx_hbm, x_hbm, out_hbm, idx_v, x_v, rows_v, act_v, row_sems, idx_sems, x_sems, out_sems):
        wid = lax.axis_index("s") * 2 + lax.axis_index("c")
        base = wid * per_worker
        last = base + per_worker - 1
        lane = lax.iota(jnp.int32, SC_LANES)

        def idx_copy(tok, slot):
            return pltpu.make_async_copy(idx_hbm.at[pl.ds(pl.multiple_of(tok * PEER_PAIRS, PEER_PAIRS), PEER_PAIRS)],
                                         idx_v.at[slot], idx_sems.at[slot])

        def x_copy(tok, slot):
            return pltpu.make_async_copy(x_hbm.at[tok], x_v.at[slot], x_sems.at[slot])

        def row_copy(slot, w):
            return pltpu.make_async_copy(table_hbm.at[idx_v.at[slot, pl.ds(w * DOT_WIN, DOT_WIN)]], rows_v.at[w],
                                         row_sems.at[w])

        def out_copy(tok, slot):
            return pltpu.make_async_copy(act_v.at[slot],
                                         out_hbm.at[pl.ds(pl.multiple_of(tok * PEER_PAIRS, PEER_PAIRS), PEER_PAIRS)],
                                         out_sems.at[slot])

        def window_dots(slot, w):
            for g in range(groups):
                def body(c, accs, g=g):
                    xlo = x_v[slot, pl.ds(c * SC_LANES, SC_LANES)]
                    xhi = x_v[slot, pl.ds(HALF_W + c * SC_LANES, SC_LANES)]
                    new = []
                    for r in range(SC_LANES):
                        wv = rows_v[w, g * SC_LANES + r, pl.ds(c * SC_LANES, SC_LANES)]
                        lo = lax.bitcast_convert_type(wv << 16, F32)
                        hi = lax.bitcast_convert_type(wv & jnp.int32(-65536), F32)
                        new.append(accs[r] + lo * xlo + hi * xhi)
                    return tuple(new)
                accs = lax.fori_loop(0, HALF_W // SC_LANES, body,
                                     tuple(jnp.zeros((SC_LANES,), F32) for _ in range(SC_LANES)))
                v = jnp.zeros((SC_LANES,), F32)
                for r in range(SC_LANES):
                    v = jnp.where(lane == r, jnp.sum(accs[r]), v)
                act_v[slot, pl.ds(w * DOT_WIN + g * SC_LANES, SC_LANES)] = v

        idx_copy(base, 0).start()
        x_copy(base, 0).start()
        idx_copy(base, 0).wait()
        for w in range(n_win):
            row_copy(0, w).start()

        @pl.loop(0, per_worker, step=2)
        def _(i):
            for slot in range(2):
                tok = base + i + slot
                nxt = jnp.minimum(tok + 1, last)
                other = 1 - slot
                idx_copy(nxt, other).start()
                x_copy(nxt, other).start()
                x_copy(tok, slot).wait()

                @pl.when(i + slot >= 2)
                def _():
                    out_copy(tok, slot).wait()

                idx_copy(nxt, other).wait()
                for w in range(n_win):
                    row_copy(slot, w).wait()
                    window_dots(slot, w)
                    row_copy(other, w).start()
                out_copy(tok, slot).start()

        for w in range(n_win):
            row_copy(0, w).wait()
        x_copy(last, 0).wait()
        for slot in range(2):
            out_copy(last, slot).wait()

    return dots(table, idx, x)


def _peer_kernel(vg_ref, act2_ref, w2_ref, h1_ref, gf_ref, l2g_ref, l2b_ref, o_ref, f_ref, *, tb):
    sub = 8
    span = sub * PEER_PAIRS
    pair_w = 2 * PEER_PAIRS
    half = D_MODEL // 2
    lane = lax.broadcasted_iota(jnp.int32, (sub, 2 * span), 1)
    diag = (lane // pair_w) == lax.broadcasted_iota(jnp.int32, (sub, 2 * span), 0)
    even = (lane % 2) == 0
    for s in range(tb // sub):
        rows = slice(s * span, (s + 1) * span)
        toks = slice(s * sub, (s + 1) * sub)
        coef = w2_ref[toks, :] * _gelu(act2_ref[toks, :])
        cdiag = jnp.where(diag, jnp.concatenate([coef] * sub, axis=1), 0.0)
        c2 = jnp.concatenate([jnp.where(even, cdiag, 0.0), jnp.where(even, 0.0, cdiag)], axis=0).astype(BF16)
        f2 = _dot(c2, pltpu.bitcast(vg_ref[rows, :], BF16))
        f_ref[toks, :half] = f2[:sub]
        f_ref[toks, half:] = f2[sub:]
    r = DEEPNORM_ALPHA * h1_ref[...] + gf_ref[0] * f_ref[...]
    o_ref[...] = _ln_plain(r) * l2g_ref[...] + l2b_ref[...]


def _peer_eval(vg, act2, w2, h1, g_f, l2g, l2b, *, tb, tok0, length):
    n = h1.shape[0]
    tok_blk0 = tok0 // tb
    per_b = length // tb
    tok_spec = lambda width: pl.BlockSpec((tb, width), lambda i: (i, 0))
    return pl.pallas_call(
        functools.partial(_peer_kernel, tb=tb),
        grid=(n // tb,),
        in_specs=[
            pl.BlockSpec((tb * PEER_PAIRS, D_MODEL // 2), lambda i: (i, 0)),
            tok_spec(2 * PEER_PAIRS), tok_spec(2 * PEER_PAIRS), tok_spec(D_MODEL),
            pl.BlockSpec((1, 1, D_MODEL), lambda i: ((tok_blk0 + i) // per_b, 0, 0)),
            pl.BlockSpec((1, D_MODEL), lambda i: (0, 0)),
            pl.BlockSpec((1, D_MODEL), lambda i: (0, 0)),
        ],
        out_specs=tok_spec(D_MODEL),
        out_shape=jax.ShapeDtypeStruct((n, D_MODEL), F32),
        scratch_shapes=[pltpu.VMEM((tb, D_MODEL), F32)],
        compiler_params=_params("parallel"),
        name="peer_eval",
    )(vg, act2, w2, h1, g_f, l2g, l2b)


def _rope_perm():
    half = ROPE_AXIS_DIM // 2
    base = jnp.arange(QK_ROPE_DIM)
    return jnp.where((base % ROPE_AXIS_DIM) < half, base + half, base - half)


def _rope_tables(length):
    pos = jnp.arange(length)
    row = (pos // GRID_W).astype(F32)
    col = (pos % GRID_W).astype(F32)
    inv_freq = jnp.power(ROPE_BASE, -jnp.arange(0, ROPE_AXIS_DIM, 2, dtype=F32) / ROPE_AXIS_DIM)
    ang_r = row[:, None] * inv_freq
    ang_c = col[:, None] * inv_freq
    zeros = jnp.zeros((length, QK_ROPE_DIM), F32)
    c_tab = jnp.concatenate([jnp.cos(ang_r), jnp.cos(ang_r), jnp.cos(ang_c), jnp.cos(ang_c), zeros], axis=1)
    s_tab = jnp.concatenate([-jnp.sin(ang_r), jnp.sin(ang_r), -jnp.sin(ang_c), jnp.sin(ang_c), zeros], axis=1)
    return c_tab, s_tab


def _identity_tables(length):
    ones = jnp.ones((length, QK_ROPE_DIM), F32)
    zeros = jnp.zeros((length, QK_ROPE_DIM), F32)
    return jnp.concatenate([ones, zeros], axis=1), jnp.zeros((length, 2 * QK_ROPE_DIM), F32)


def _pack_rows(table):
    bits = lax.bitcast_convert_type(table.astype(BF16), jnp.uint16).astype(jnp.uint32)
    return lax.bitcast_convert_type(bits[:, :HALF_W] | (bits[:, HALF_W:] << 16), jnp.int32)


def _layer_weights(w_in, q_norm_g, kv_norm_g, w_uq, w_ukv):
    perm = _rope_perm()
    p0 = Q_RANK
    p1 = p0 + KV_RANK
    p2 = p1 + QK_ROPE_DIM
    p3 = p2 + S5_WIDTH
    p4 = p3 + D_MODEL
    w_kpe = w_in[:, p1:p2]
    wckv = jnp.concatenate([w_in[:, p0:p1], w_kpe, w_kpe[:, perm]], axis=1)
    uq = w_uq.reshape(Q_RANK, N_HEADS, QK_NOPE_DIM + QK_ROPE_DIM)
    pe = uq[:, :, QK_NOPE_DIM:]
    wuq = jnp.concatenate([uq[:, :, :QK_NOPE_DIM], pe, pe[:, :, perm]], axis=2).reshape(Q_RANK, N_HEADS * HEAD_PAD)
    cast = lambda a: a.astype(BF16)
    return (cast(w_in[:, :p0]), cast(wckv), cast(w_in[:, p2:p3]), cast(w_in[:, p3:p4]), cast(w_in[:, p4:]),
            q_norm_g.reshape(1, Q_RANK), kv_norm_g.reshape(1, KV_RANK), cast(wuq), cast(w_ukv))


def kernel(x, c, ctx, c_ctx, w_mod, b_mod, w_in, q_norm_g, kv_norm_g, w_uq, w_ukv, s5_a_re, s5_a_im, s5_log_dt, s5_b_re, s5_b_im, s5_c_re, s5_c_im, s5_d, w_glu, w_out, ln1_g, ln1_b, peer_wq, peer_keys, peer_u, peer_v, ln2_g, ln2_b):
    bsz, length, _ = x.shape
    l_ctx = ctx.shape[1]
    n_tok = bsz * length
    layer = 0
    tm = 256
    s5_tt = 64
    tok_chunk = 4096
    peer_tb = 32

    cond = jnp.concatenate([c, c_ctx[None, :], jnp.zeros((7, D_MODEL), F32)], axis=0)
    mod = _modulation(cond, w_mod[layer], b_mod[layer]).reshape(cond.shape[0], N_MOD, 1, D_MODEL)
    sh_a, sc_a, g_a, sh_f, sc_f, g_f = (mod[:bsz, i] for i in range(N_MOD))
    csh_a, csc_a = mod[bsz:bsz + 1, 0], mod[bsz:bsz + 1, 1]

    wts = _layer_weights(w_in[layer], q_norm_g[layer], kv_norm_g[layer], w_uq[layer], w_ukv[layer])
    c_lat, s_lat = _rope_tables(length)
    c_id, s_id = _identity_tables(l_ctx)
    k_ctx, v_ctx, su_ctx = _inproj(ctx, csh_a, csc_a, c_id, s_id, wts, is_ctx=True, tm=min(tm, l_ctx))
    q, k_lat, v_lat, su_lat, gm, gs = _inproj(x, sh_a, sc_a, c_lat, s_lat, wts, is_ctx=False, tm=tm)

    att = _attention(q, k_ctx, v_ctx, k_lat, v_lat, tq=min(512, length))

    su_all = jnp.concatenate([su_ctx, su_lat], axis=0)
    mats = _s5_matrices(s5_a_re[layer], s5_a_im[layer], s5_log_dt[layer], s5_b_re[layer], s5_b_im[layer],
                        s5_c_re[layer], s5_c_im[layer], bsz)
    y = _s5_scan(su_all.reshape(-1, S5_WIDTH), mats, bsz=bsz, l_ctx=l_ctx, tt=s5_tt)
    y2d = y.reshape(2, length, bsz * S5_WIDTH)

    keys = peer_keys[layer].reshape(2 * PEER_HEADS, PEER_N_KEYS, PEER_HALF).astype(BF16)
    h1, xm, st = _merge(x, att, gm, gs, su_all, y2d, s5_d[layer].reshape(1, S5_WIDTH),
                        w_glu[layer].astype(BF16), w_out[layer].astype(BF16), g_a,
                        ln1_g[layer].reshape(1, D_MODEL), ln1_b[layer].reshape(1, D_MODEL), sh_f, sc_f,
                        peer_wq[layer].astype(BF16), keys, tm=tm, ctx_tiles=l_ctx // tm)

    experts, gates = _peer_topk(st, lanes=128)
    idx = experts.transpose(2, 0, 1).reshape(n_tok * PEER_PAIRS)
    w2 = jnp.repeat(gates.transpose(2, 0, 1).reshape(n_tok, PEER_PAIRS), 2, axis=1)
    u_pack = _pack_rows(peer_u[layer])
    v_pack = _pack_rows(peer_v[layer])

    h1f = h1.reshape(n_tok, D_MODEL)
    xmf = xm.reshape(n_tok, D_MODEL)
    l2g = ln2_g[layer].reshape(1, D_MODEL)
    l2b = ln2_b[layer].reshape(1, D_MODEL)
    tok_chunk = min(tok_chunk, n_tok)
    outs = []
    for t0 in range(0, n_tok, tok_chunk):
        ids = idx[t0 * PEER_PAIRS:(t0 + tok_chunk) * PEER_PAIRS]
        acts = _sc_dots(u_pack, ids, xmf[t0:t0 + tok_chunk]).reshape(tok_chunk, PEER_PAIRS)
        vg = _sc_gather(v_pack, ids, win=64)
        outs.append(_peer_eval(vg, jnp.repeat(acts, 2, axis=1), w2[t0:t0 + tok_chunk], h1f[t0:t0 + tok_chunk],
                               g_f, l2g, l2b, tb=peer_tb, tok0=t0, length=length))
    return jnp.concatenate(outs, axis=0).reshape(bsz, length, D_MODEL)
```

```python
import functools
import math

import jax
import jax.numpy as jnp
from jax import lax
from jax.experimental import pallas as pl
from jax.experimental.pallas import tpu as pltpu
from jax.experimental.pallas import tpu_sc as plsc

F32 = jnp.float32
BF16 = jnp.bfloat16

D_MODEL = 1024
DEPTH = 1
GRID_W = 64
N_HEADS = 8
QK_NOPE_DIM = 128
QK_ROPE_DIM = 64
V_HEAD_DIM = 128
Q_RANK = 384
KV_RANK = 256
ROPE_AXIS_DIM = QK_ROPE_DIM // 2
ROPE_BASE = 10000.0
S5_WIDTH = D_MODEL // 2
S5_GROUP = 16
S5_GROUPS = S5_WIDTH // S5_GROUP
S5_STATE = 64
PEER_HEADS = 8
PEER_N_KEYS = 128
PEER_TOPK = 16
PEER_HALF = 128
PEER_PAIRS = PEER_HEADS * PEER_TOPK
DEEPNORM_ALPHA = (2.0 * DEPTH) ** 0.25
LN_EPS = 1e-6
N_MOD = 6
ATT_SCALE = (QK_NOPE_DIM + QK_ROPE_DIM) ** -0.5

HEAD_PAD = 256
S5_COLS = S5_GROUPS * S5_STATE
S5_CHUNKS = 2
SC_WORKERS = 32
SC_LANES = 16
DOT_WIN = 32
V_BUFS = 4
HALF_W = D_MODEL // 2
VMEM_LIMIT = 48 * 1024 * 1024


def _dot(a, b):
    return jnp.dot(a, b, preferred_element_type=F32)


def _dot_nt(a, b):
    return lax.dot_general(a, b, (((1,), (1,)), ((), ())), preferred_element_type=F32)


def _gelu(x):
    return 0.5 * x * (1.0 + jnp.tanh(0.7978845608028654 * (x + 0.044715 * (x * x * x))))


def _ln_plain(x):
    mu = jnp.mean(x, axis=-1, keepdims=True)
    xc = x - mu
    var = jnp.mean(xc * xc, axis=-1, keepdims=True)
    return xc * lax.rsqrt(var + LN_EPS)


def _rms(x, g):
    return x * lax.rsqrt(jnp.mean(x * x, axis=-1, keepdims=True) + LN_EPS) * g


def _params(*sem):
    return pltpu.CompilerParams(dimension_semantics=sem, vmem_limit_bytes=VMEM_LIMIT)


def _mod_kernel(cond_ref, w_ref, b_ref, o_ref):
    a = cond_ref[...]
    a = a * jax.nn.sigmoid(a)
    a_hi = a.astype(BF16)
    a_lo = (a - a_hi.astype(F32)).astype(BF16)
    w = w_ref[...]
    w_hi = w.astype(BF16)
    w_lo = (w - w_hi.astype(F32)).astype(BF16)
    o_ref[...] = _dot(a_hi, w_hi) + _dot(a_lo, w_hi) + _dot(a_hi, w_lo) + b_ref[...]


def _modulation(cond, w_mod, b_mod):
    rows = cond.shape[0]
    n = w_mod.shape[1]
    blk = D_MODEL
    return pl.pallas_call(
        _mod_kernel,
        grid=(n // blk,),
        in_specs=[
            pl.BlockSpec((rows, D_MODEL), lambda j: (0, 0)),
            pl.BlockSpec((D_MODEL, blk), lambda j: (0, j)),
            pl.BlockSpec((1, blk), lambda j: (0, j)),
        ],
        out_specs=pl.BlockSpec((rows, blk), lambda j: (0, j)),
        out_shape=jax.ShapeDtypeStruct((rows, n), F32),
        compiler_params=_params("arbitrary"),
        name="modulation",
    )(cond, w_mod, b_mod.reshape(1, n))


def _rope128(t, c_tab, s_tab):
    return t * c_tab + pltpu.roll(t, 64, axis=1) * s_tab


def _inproj_kernel(x_ref, sh_ref, sc_ref, c_ref, s_ref, wcq_ref, wckv_ref, wsu_ref, wgm_ref, wgs_ref,
                   qg_ref, kvg_ref, wuq_ref, wukv_ref, *out_refs, is_ctx):
    if is_ctx:
        k_ref, v_ref, su_ref = out_refs
    else:
        q_ref, k_ref, v_ref, su_ref, gm_ref, gs_ref = out_refs
    xm = _ln_plain(x_ref[0]) * (1.0 + sc_ref[0]) + sh_ref[0]
    xb = xm.astype(BF16)
    c_tab = c_ref[...]
    s_tab = s_ref[...]

    ckvpe = _dot(xb, wckv_ref[...])
    ckv_n = _rms(ckvpe[:, :KV_RANK], kvg_ref[...]).astype(BF16)
    kpe = _rope128(ckvpe[:, KV_RANK:], c_tab, s_tab).astype(BF16)
    kv = _dot(ckv_n, wukv_ref[...])
    for h in range(N_HEADS):
        k_ref[0, h, :, 0:128] = kv[:, h * 256:h * 256 + 128].astype(BF16)
        k_ref[0, h, :, 128:256] = kpe
        v_ref[0, h] = kv[:, h * 256 + 128:(h + 1) * 256].astype(BF16)

    su_ref[...] = _dot(xb, wsu_ref[...])

    if not is_ctx:
        cq_n = _rms(_dot(xb, wcq_ref[...]), qg_ref[...]).astype(BF16)
        q = _dot(cq_n, wuq_ref[...])
        for h in range(N_HEADS):
            q_ref[0, h, :, 0:128] = (q[:, h * 256:h * 256 + 128] * ATT_SCALE).astype(BF16)
            q_ref[0, h, :, 128:256] = (_rope128(q[:, h * 256 + 128:(h + 1) * 256], c_tab, s_tab) * ATT_SCALE).astype(BF16)
        gm_ref[0] = jax.nn.sigmoid(_dot(xb, wgm_ref[...])).astype(BF16)
        gs_ref[0] = jax.nn.sigmoid(_dot(xb, wgs_ref[...])).astype(BF16)


def _inproj(x, shift, scale, c_tab, s_tab, wts, *, is_ctx, tm, b0, bsz):
    length = x.shape[1]
    mod_map = (lambda b, t: (0, 0, 0)) if shift.shape[0] == 1 else (lambda b, t: (b0 + b, 0, 0))
    const2 = lambda b, t: (0, 0)
    wcq, wckv, wsu, wgm, wgs, qg, kvg, wuq, wukv = wts
    in_specs = [
        pl.BlockSpec((1, tm, D_MODEL), lambda b, t: (b0 + b, t, 0)),
        pl.BlockSpec((1, 1, D_MODEL), mod_map),
        pl.BlockSpec((1, 1, D_MODEL), mod_map),
        pl.BlockSpec((tm, 128), lambda b, t: (t, 0)),
        pl.BlockSpec((tm, 128), lambda b, t: (t, 0)),
    ] + [pl.BlockSpec(w.shape, const2) for w in (wcq, wckv, wsu, wgm, wgs, qg, kvg, wuq, wukv)]
    head_spec = lambda width: pl.BlockSpec((1, N_HEADS, tm, width), lambda b, t: (b, 0, t, 0))
    row_spec = pl.BlockSpec((1, tm, D_MODEL), lambda b, t: (b, t, 0))
    su_spec = pl.BlockSpec((tm, S5_WIDTH), lambda b, t: (t, b))
    k_shape = jax.ShapeDtypeStruct((bsz, N_HEADS, length, HEAD_PAD), BF16)
    v_shape = jax.ShapeDtypeStruct((bsz, N_HEADS, length, V_HEAD_DIM), BF16)
    su_shape = jax.ShapeDtypeStruct((length, bsz * S5_WIDTH), F32)
    g_shape = jax.ShapeDtypeStruct((bsz, length, D_MODEL), BF16)
    if is_ctx:
        out_specs = [head_spec(HEAD_PAD), head_spec(V_HEAD_DIM), su_spec]
        out_shape = [k_shape, v_shape, su_shape]
    else:
        out_specs = [head_spec(HEAD_PAD), head_spec(HEAD_PAD), head_spec(V_HEAD_DIM), su_spec, row_spec, row_spec]
        out_shape = [k_shape, k_shape, v_shape, su_shape, g_shape, g_shape]
    return pl.pallas_call(
        functools.partial(_inproj_kernel, is_ctx=is_ctx),
        grid=(bsz, length // tm),
        in_specs=in_specs,
        out_specs=out_specs,
        out_shape=out_shape,
        compiler_params=_params("parallel", "parallel"),
        name="inproj_ctx" if is_ctx else "inproj_lat",
    )(x, shift, scale, c_tab, s_tab, wcq, wckv, wsu, wgm, wgs, qg, kvg, wuq, wukv)


def _attn_kernel(q_ref, kc_ref, vc_ref, kl_ref, vl_ref, o_ref):
    q = q_ref[0, 0]
    s_c = _dot_nt(q, kc_ref[0, 0])
    s_l = _dot_nt(q, kl_ref[0, 0])
    m = jnp.maximum(jnp.max(s_c, axis=-1, keepdims=True), jnp.max(s_l, axis=-1, keepdims=True))
    p_c = jnp.exp(s_c - m)
    p_l = jnp.exp(s_l - m)
    denom = jnp.sum(p_c, axis=-1, keepdims=True) + jnp.sum(p_l, axis=-1, keepdims=True)
    o = _dot(p_c.astype(BF16), vc_ref[0, 0]) + _dot(p_l.astype(BF16), vl_ref[0, 0])
    o_ref[0] = (o / denom).astype(BF16)


def _attention(q, k_ctx, v_ctx, k_lat, v_lat, *, tq):
    bsz, heads, length, _ = q.shape
    l_ctx = k_ctx.shape[2]
    full = lambda b, h, i: (b, h, 0, 0)
    return pl.pallas_call(
        _attn_kernel,
        grid=(bsz, heads, length // tq),
        in_specs=[
            pl.BlockSpec((1, 1, tq, HEAD_PAD), lambda b, h, i: (b, h, i, 0)),
            pl.BlockSpec((1, 1, l_ctx, HEAD_PAD), full),
            pl.BlockSpec((1, 1, l_ctx, V_HEAD_DIM), full),
            pl.BlockSpec((1, 1, length, HEAD_PAD), full),
            pl.BlockSpec((1, 1, length, V_HEAD_DIM), full),
        ],
        out_specs=pl.BlockSpec((1, tq, V_HEAD_DIM), lambda b, h, i: (b, i, h)),
        out_shape=jax.ShapeDtypeStruct((bsz, length, heads * V_HEAD_DIM), BF16),
        compiler_params=_params("parallel", "parallel", "arbitrary"),
        name="attention",
    )(q, k_ctx, v_ctx, k_lat, v_lat)


def _s5_kernel(u_ref, are_ref, aim_ref, bre_ref, bim_ref, cre_ref, cim_ref, y_ref,
               bu_re, bu_im, h_re, h_im, *, tt, bsz, n_ctx_tiles):
    d = pl.program_id(0)
    i = pl.program_id(1)

    @pl.when(i == 0)
    def _():
        h_re[...] = jnp.zeros_like(h_re)
        h_im[...] = jnp.zeros_like(h_im)

    u = u_ref[...].astype(BF16)
    half = S5_COLS // S5_CHUNKS
    cw = S5_WIDTH // S5_CHUNKS
    for c in range(S5_CHUNKS):
        uc = u[:, c * cw:(c + 1) * cw]
        bu_re[:, c * half:(c + 1) * half] = _dot(uc, bre_ref[0, c])
        bu_im[:, c * half:(c + 1) * half] = _dot(uc, bim_ref[0, c])

    col_w = 512
    for cc in range(S5_COLS // col_w):
        cols = slice(cc * col_w, (cc + 1) * col_w)
        a_r = are_ref[0, :, cols]
        a_i = aim_ref[0, :, cols]

        def step(t, carry, cols=cols, a_r=a_r, a_i=a_i):
            hr, hi = carry
            pos = jnp.where(d == 0, t, tt - 1 - t)
            r = pl.multiple_of(pos * bsz, bsz)
            nr = a_r * hr - a_i * hi + bu_re[pl.ds(r, bsz), cols]
            ni = a_r * hi + a_i * hr + bu_im[pl.ds(r, bsz), cols]
            bu_re[pl.ds(r, bsz), cols] = nr
            bu_im[pl.ds(r, bsz), cols] = ni
            return nr, ni

        hr, hi = lax.fori_loop(0, tt, step, (h_re[:, cols], h_im[:, cols]))
        h_re[:, cols] = hr
        h_im[:, cols] = hi

    @pl.when(i >= n_ctx_tiles)
    def _():
        for c in range(S5_CHUNKS):
            sl = slice(c * half, (c + 1) * half)
            y = _dot(bu_re[:, sl].astype(BF16), cre_ref[0, c]) + _dot(bu_im[:, sl].astype(BF16), cim_ref[0, c])
            y_ref[0, :, c * cw:(c + 1) * cw] = y


def _s5_scan(u_all, mats, *, bsz, l_ctx, tt):
    a_re_b, a_im_b, b_re_bd, b_im_bd, c_re_bd, c_im_bd = mats
    rows = tt * bsz
    n_tiles = u_all.shape[0] // rows
    n_ctx_tiles = l_ctx // tt
    n_lat_tiles = n_tiles - n_ctx_tiles

    def u_map(d, i):
        bwd = jnp.where(i < n_ctx_tiles, n_ctx_tiles - 1 - i, n_tiles - 1 - (i - n_ctx_tiles))
        return (jnp.where(d == 0, i, bwd), 0)

    def y_map(d, i):
        j = jnp.maximum(i - n_ctx_tiles, 0)
        return (d, jnp.where(d == 0, j, n_lat_tiles - 1 - j), 0)

    dmap3 = lambda d, i: (d, 0, 0)
    dmap4 = lambda d, i: (d, 0, 0, 0)
    return pl.pallas_call(
        functools.partial(_s5_kernel, tt=tt, bsz=bsz, n_ctx_tiles=n_ctx_tiles),
        grid=(2, n_tiles),
        in_specs=[
            pl.BlockSpec((rows, S5_WIDTH), u_map),
            pl.BlockSpec((1,) + a_re_b.shape[1:], dmap3),
            pl.BlockSpec((1,) + a_im_b.shape[1:], dmap3),
            pl.BlockSpec((1,) + b_re_bd.shape[1:], dmap4),
            pl.BlockSpec((1,) + b_im_bd.shape[1:], dmap4),
            pl.BlockSpec((1,) + c_re_bd.shape[1:], dmap4),
            pl.BlockSpec((1,) + c_im_bd.shape[1:], dmap4),
        ],
        out_specs=pl.BlockSpec((1, rows, S5_WIDTH), y_map),
        out_shape=jax.ShapeDtypeStruct((2, n_lat_tiles * rows, S5_WIDTH), F32),
        scratch_shapes=[
            pltpu.VMEM((rows, S5_COLS), F32),
            pltpu.VMEM((rows, S5_COLS), F32),
            pltpu.VMEM((bsz, S5_COLS), F32),
            pltpu.VMEM((bsz, S5_COLS), F32),
        ],
        compiler_params=_params("arbitrary", "arbitrary"),
        name="s5_scan",
    )(u_all, a_re_b, a_im_b, b_re_bd, b_im_bd, c_re_bd, c_im_bd)


def _s5_matrices(a_re, a_im, log_dt, b_re, b_im, c_re, c_im, bsz):
    dt = jnp.exp(log_dt)[..., None]
    mag = jnp.exp(a_re * dt)
    ab_re = mag * jnp.cos(a_im * dt)
    ab_im = mag * jnp.sin(a_im * dt)
    den = a_re * a_re + a_im * a_im
    f_re = ((ab_re - 1.0) * a_re + ab_im * a_im) / den
    f_im = (ab_im * a_re - (ab_re - 1.0) * a_im) / den
    bb_re = f_re[..., None] * b_re - f_im[..., None] * b_im
    bb_im = f_re[..., None] * b_im + f_im[..., None] * b_re
    gl = S5_GROUPS // S5_CHUNKS
    eye = jnp.eye(gl, dtype=F32)

    def in_bd(bb):
        t = bb.reshape(2, S5_CHUNKS, gl, S5_STATE, S5_GROUP)
        t = jnp.einsum('dkgpc,gh->dkgchp', t, eye)
        return t.reshape(2, S5_CHUNKS, gl * S5_GROUP, gl * S5_STATE).astype(BF16)

    def out_bd(cc):
        t = cc.reshape(2, S5_CHUNKS, gl, S5_GROUP, S5_STATE)
        t = jnp.einsum('dkgcp,gh->dkgphc', t, eye)
        return t.reshape(2, S5_CHUNKS, gl * S5_STATE, gl * S5_GROUP).astype(BF16)

    bcast = lambda a: jnp.broadcast_to(a.reshape(2, 1, S5_COLS), (2, bsz, S5_COLS))
    return bcast(ab_re), bcast(ab_im), in_bd(bb_re), in_bd(bb_im), out_bd(c_re), out_bd(-c_im)


def _merge_kernel(x_ref, att_ref, gm_ref, gs_ref, su_ref, y_ref, dsk_ref, wglu_ref, wout_ref, ga_ref,
                  l1g_ref, l1b_ref, shf_ref, scf_ref, wq_ref, keys_ref, h1_ref, xm_ref, st_ref):
    y = su_ref[...] * dsk_ref[...] + y_ref[0] + y_ref[1]
    gl = _dot(_gelu(y).astype(BF16), wglu_ref[...])
    s5_out = gl[:, :D_MODEL] * jax.nn.sigmoid(gl[:, D_MODEL:])
    merged = gm_ref[0].astype(F32) * att_ref[0].astype(F32) + gs_ref[0].astype(F32) * s5_out
    out = _dot(merged.astype(BF16), wout_ref[...])
    h1 = _ln_plain(DEEPNORM_ALPHA * x_ref[0] + ga_ref[0] * out) * l1g_ref[...] + l1b_ref[...]
    h1_ref[0] = h1
    xm = _ln_plain(h1) * (1.0 + scf_ref[0]) + shf_ref[0]
    xm_ref[0] = xm
    qp = _dot(xm.astype(BF16), wq_ref[...]).astype(BF16)
    for j in range(2 * PEER_HEADS):
        st_ref[j] = _dot_nt(keys_ref[j], qp[:, j * PEER_HALF:(j + 1) * PEER_HALF])


def _merge(x, att, gm, gs, su_all2d, y2d, dsk, wglu, wout, g_a, l1g, l1b, sh_f, sc_f, wq, keys, *, tm, ctx_tiles, b0):
    bsz, length, _ = att.shape
    nt = length // tm
    row = pl.BlockSpec((1, tm, D_MODEL), lambda b, t: (b, t, 0))
    xrow = pl.BlockSpec((1, tm, D_MODEL), lambda b, t: (b0 + b, t, 0))
    modb = pl.BlockSpec((1, 1, D_MODEL), lambda b, t: (b0 + b, 0, 0))
    const = lambda a: pl.BlockSpec(a.shape, lambda b, t: (0,) * a.ndim)
    return pl.pallas_call(
        _merge_kernel,
        grid=(bsz, nt),
        in_specs=[
            xrow, row, row, row,
            pl.BlockSpec((tm, S5_WIDTH), lambda b, t: (ctx_tiles + t, b)),
            pl.BlockSpec((2, tm, S5_WIDTH), lambda b, t: (0, t, b)),
            const(dsk), const(wglu), const(wout), modb, const(l1g), const(l1b), modb, modb, const(wq), const(keys),
        ],
        out_specs=[row, row, pl.BlockSpec((2 * PEER_HEADS, PEER_N_KEYS, tm), lambda b, t: (0, 0, b * nt + t))],
        out_shape=[
            jax.ShapeDtypeStruct((bsz, length, D_MODEL), F32),
            jax.ShapeDtypeStruct((bsz, length, D_MODEL), F32),
            jax.ShapeDtypeStruct((2 * PEER_HEADS, PEER_N_KEYS, bsz * length), F32),
        ],
        compiler_params=_params("parallel", "parallel"),
        name="merge_peer_query",
    )(x, att, gm, gs, su_all2d, y2d, dsk, wglu, wout, g_a, l1g, l1b, sh_f, sc_f, wq, keys)


def _take_top(vals, codes, payload, k):
    rows = lax.broadcasted_iota(jnp.int32, (k, vals.shape[1]), 0)
    top_v = jnp.zeros((k, vals.shape[1]), F32)
    top_p = jnp.zeros((k, vals.shape[1]), jnp.int32)
    big = jnp.int32(1 << 30)
    for r in range(k):
        m = jnp.max(vals, axis=0, keepdims=True)
        cm = jnp.min(jnp.where(vals == m, codes, big), axis=0, keepdims=True)
        sel = codes == cm
        if payload is None:
            p = cm
        else:
            p = jnp.max(jnp.where(sel, payload, -1), axis=0, keepdims=True)
        vals = jnp.where(sel, -jnp.inf, vals)
        top_v = jnp.where(rows == r, m, top_v)
        top_p = jnp.where(rows == r, p, top_p)
    return top_v, top_p


def _topk_kernel(st_ref, e_ref, g_ref):
    lanes = st_ref.shape[2]
    key_idx = lax.broadcasted_iota(jnp.int32, (PEER_N_KEYS, lanes), 0)
    sub_idx = lax.broadcasted_iota(jnp.int32, (PEER_TOPK, lanes), 0)
    for h in range(PEER_HEADS):
        sv0, si0 = _take_top(st_ref[2 * h], key_idx, None, PEER_TOPK)
        sv1, si1 = _take_top(st_ref[2 * h + 1], key_idx, None, PEER_TOPK)
        cs, ce, cc = [], [], []
        for i in range(PEER_TOPK):
            nj = PEER_TOPK if i == 0 else PEER_TOPK // 2
            cs.append(sv0[i:i + 1, :] + sv1[0:nj, :])
            ce.append(si0[i:i + 1, :] * PEER_N_KEYS + si1[0:nj, :])
            cc.append(sub_idx[0:nj, :] + i * PEER_TOPK)
        top_s, top_e = _take_top(jnp.concatenate(cs, axis=0), jnp.concatenate(cc, axis=0),
                                 jnp.concatenate(ce, axis=0), PEER_TOPK)
        p = jnp.exp(top_s - jnp.max(top_s, axis=0, keepdims=True))
        g_ref[h] = p / jnp.sum(p, axis=0, keepdims=True)
        e_ref[h] = top_e


def _peer_topk(st, *, lanes):
    n = st.shape[2]
    out_spec = pl.BlockSpec((PEER_HEADS, PEER_TOPK, lanes), lambda i: (0, 0, i))
    return pl.pallas_call(
        _topk_kernel,
        grid=(n // lanes,),
        in_specs=[pl.BlockSpec((2 * PEER_HEADS, PEER_N_KEYS, lanes), lambda i: (0, 0, i))],
        out_specs=[out_spec, out_spec],
        out_shape=[jax.ShapeDtypeStruct((PEER_HEADS, PEER_TOPK, n), jnp.int32),
                   jax.ShapeDtypeStruct((PEER_HEADS, PEER_TOPK, n), F32)],
        compiler_params=_params("parallel"),
        name="peer_topk",
    )(st)


def _sc_peer(u_tab, v_tab, idx, x):
    n_tok = x.shape[0]
    per_worker = n_tok // SC_WORKERS
    n_win = PEER_PAIRS // DOT_WIN
    steps = PEER_PAIRS // SC_LANES
    per_win = DOT_WIN // SC_LANES
    mesh = plsc.VectorSubcoreMesh(core_axis_name="c", subcore_axis_name="s")

    @functools.partial(
        pl.kernel, mesh=mesh,
        out_type=(jax.ShapeDtypeStruct((n_tok * PEER_PAIRS,), F32),
                  jax.ShapeDtypeStruct((n_tok * PEER_PAIRS, HALF_W), jnp.int32)),
        scratch_types=[
            pltpu.VMEM((2, PEER_PAIRS), jnp.int32),
            pltpu.VMEM((2, PEER_PAIRS), jnp.int32),
            pltpu.VMEM((2, 2 * HALF_W), F32),
            pltpu.VMEM((n_win, DOT_WIN, HALF_W), jnp.int32),
            pltpu.VMEM((V_BUFS, SC_LANES, HALF_W), jnp.int32),
            pltpu.VMEM((2, PEER_PAIRS), F32),
            pltpu.SemaphoreType.DMA((n_win,)),
            pltpu.SemaphoreType.DMA((V_BUFS,)),
            pltpu.SemaphoreType.DMA((V_BUFS,)),
            pltpu.SemaphoreType.DMA((2,)),
            pltpu.SemaphoreType.DMA((2,)),
            pltpu.SemaphoreType.DMA((2,)),
        ],
        compiler_params=pltpu.CompilerParams(needs_layout_passes=False),
    )
    def peer(u_hbm, v_hbm, idx_hbm, x_hbm, act_hbm, vrows_hbm, idx_v, vidx_v, x_v, urows_v, vrows_v, act_v,
             u_sems, vg_sems, vw_sems, idx_sems, x_sems, act_sems):
        wid = lax.axis_index("s") * 2 + lax.axis_index("c")
        base = wid * per_worker
        last = base + per_worker - 1
        lane = lax.iota(jnp.int32, SC_LANES)

        def pair_off(tok):
            return pl.multiple_of(tok * PEER_PAIRS, PEER_PAIRS)

        def idx_copy(tok, slot):
            return pltpu.make_async_copy(idx_hbm.at[pl.ds(pair_off(tok), PEER_PAIRS)], idx_v.at[slot], idx_sems.at[slot])

        def x_copy(tok, slot):
            return pltpu.make_async_copy(x_hbm.at[tok], x_v.at[slot], x_sems.at[slot])

        def u_gather(slot, w):
            return pltpu.make_async_copy(u_hbm.at[idx_v.at[slot, pl.ds(w * DOT_WIN, DOT_WIN)]], urows_v.at[w],
                                         u_sems.at[w])

        def v_gather(slot, g):
            return pltpu.make_async_copy(v_hbm.at[vidx_v.at[slot, pl.ds(g * SC_LANES, SC_LANES)]],
                                         vrows_v.at[g % V_BUFS], vg_sems.at[g % V_BUFS])

        def v_write(tok, g):
            return pltpu.make_async_copy(vrows_v.at[g % V_BUFS],
                                         vrows_hbm.at[pl.ds(pl.multiple_of(pair_off(tok) + g * SC_LANES, SC_LANES), SC_LANES)],
                                         vw_sems.at[g % V_BUFS])

        def act_copy(tok, slot):
            return pltpu.make_async_copy(act_v.at[slot], act_hbm.at[pl.ds(pair_off(tok), PEER_PAIRS)], act_sems.at[slot])

        def step_dots(slot, g):
            w, sub = g // per_win, g % per_win

            def body(c, accs):
                xlo = x_v[slot, pl.ds(c * SC_LANES, SC_LANES)]
                xhi = x_v[slot, pl.ds(HALF_W + c * SC_LANES, SC_LANES)]
                new = []
                for r in range(SC_LANES):
                    wv = urows_v[w, sub * SC_LANES + r, pl.ds(c * SC_LANES, SC_LANES)]
                    lo = lax.bitcast_convert_type(wv << 16, F32)
                    hi = lax.bitcast_convert_type(wv & jnp.int32(-65536), F32)
                    new.append(accs[r] + lo * xlo + hi * xhi)
                return tuple(new)

            accs = lax.fori_loop(0, HALF_W // SC_LANES, body, tuple(jnp.zeros((SC_LANES,), F32) for _ in range(SC_LANES)))
            v = jnp.zeros((SC_LANES,), F32)
            for r in range(SC_LANES):
                v = jnp.where(lane == r, jnp.sum(accs[r]), v)
            act_v[slot, pl.ds(g * SC_LANES, SC_LANES)] = v

        def unless_first(i, slot, skip, fn):
            if slot == 0 and skip:
                pl.when(i > 0)(fn)
            else:
                fn()

        idx_copy(base, 0).start()
        x_copy(base, 0).start()
        idx_copy(base, 0).wait()
        for w in range(n_win):
            u_gather(0, w).start()

        @pl.loop(0, per_worker, step=2)
        def _(i):
            for slot in range(2):
                tok = base + i + slot
                nxt = jnp.minimum(tok + 1, last)
                other = 1 - slot
                idx_copy(nxt, other).start()
                x_copy(nxt, other).start()
                for q in range(PEER_PAIRS // SC_LANES):
                    vidx_v[slot, pl.ds(q * SC_LANES, SC_LANES)] = idx_v[slot, pl.ds(q * SC_LANES, SC_LANES)]
                x_copy(tok, slot).wait()

                @pl.when(i > 0)
                def _():
                    act_copy(tok, slot).wait()

                idx_copy(nxt, other).wait()
                for g in range(steps):
                    unless_first(i, slot, g < V_BUFS, lambda tok=tok, g=g: v_write(tok, g).wait())
                    v_gather(slot, g).start()
                    if g % per_win == 0:
                        u_gather(slot, g // per_win).wait()
                    step_dots(slot, g)
                    if g % per_win == per_win - 1:
                        u_gather(other, g // per_win).start()
                    gp = (g - 2) % steps
                    tokp = tok if g >= 2 else tok - 1

                    def finish(slot=slot, gp=gp, tokp=tokp):
                        v_gather(slot, gp).wait()
                        v_write(tokp, gp).start()

                    unless_first(i, slot, g < 2, finish)
                act_copy(tok, slot).start()

        for gp in (steps - 2, steps - 1):
            v_gather(1, gp).wait()
            v_write(last, gp).start()
        for b in range(V_BUFS):
            v_write(last, b).wait()
        for w in range(n_win):
            u_gather(0, w).wait()
        x_copy(last, 0).wait()
        for slot in range(2):
            act_copy(last, slot).wait()

    return peer(u_tab, v_tab, idx, x)


def _peer_kernel(vg_ref, act2_ref, w2_ref, h1_ref, gf_ref, l2g_ref, l2b_ref, o_ref, f_ref, *, tb):
    sub = 8
    span = sub * PEER_PAIRS
    pair_w = 2 * PEER_PAIRS
    half = D_MODEL // 2
    lane = lax.broadcasted_iota(jnp.int32, (sub, 2 * span), 1)
    diag = (lane // pair_w) == lax.broadcasted_iota(jnp.int32, (sub, 2 * span), 0)
    even = (lane % 2) == 0
    for s in range(tb // sub):
        rows = slice(s * span, (s + 1) * span)
        toks = slice(s * sub, (s + 1) * sub)
        coef = w2_ref[toks, :] * _gelu(act2_ref[toks, :])
        cdiag = jnp.where(diag, jnp.concatenate([coef] * sub, axis=1), 0.0)
        c2 = jnp.concatenate([jnp.where(even, cdiag, 0.0), jnp.where(even, 0.0, cdiag)], axis=0).astype(BF16)
        f2 = _dot(c2, pltpu.bitcast(vg_ref[rows, :], BF16))
        f_ref[toks, :half] = f2[:sub]
        f_ref[toks, half:] = f2[sub:]
    r = DEEPNORM_ALPHA * h1_ref[...] + gf_ref[0] * f_ref[...]
    o_ref[...] = _ln_plain(r) * l2g_ref[...] + l2b_ref[...]


def _peer_eval(vg, act2, w2, h1, g_f, l2g, l2b, *, tb, tok0, length):
    n = h1.shape[0]
    tok_blk0 = tok0 // tb
    per_b = length // tb
    tok_spec = lambda width: pl.BlockSpec((tb, width), lambda i: (i, 0))
    return pl.pallas_call(
        functools.partial(_peer_kernel, tb=tb),
        grid=(n // tb,),
        in_specs=[
            pl.BlockSpec((tb * PEER_PAIRS, D_MODEL // 2), lambda i: (i, 0)),
            tok_spec(2 * PEER_PAIRS), tok_spec(2 * PEER_PAIRS), tok_spec(D_MODEL),
            pl.BlockSpec((1, 1, D_MODEL), lambda i: ((tok_blk0 + i) // per_b, 0, 0)),
            pl.BlockSpec((1, D_MODEL), lambda i: (0, 0)),
            pl.BlockSpec((1, D_MODEL), lambda i: (0, 0)),
        ],
        out_specs=tok_spec(D_MODEL),
        out_shape=jax.ShapeDtypeStruct((n, D_MODEL), F32),
        scratch_shapes=[pltpu.VMEM((tb, D_MODEL), F32)],
        compiler_params=_params("parallel"),
        name="peer_eval",
    )(vg, act2, w2, h1, g_f, l2g, l2b)


def _rope_perm():
    half = ROPE_AXIS_DIM // 2
    base = jnp.arange(QK_ROPE_DIM)
    return jnp.where((base % ROPE_AXIS_DIM) < half, base + half, base - half)


def _rope_tables(length):
    pos = jnp.arange(length)
    row = (pos // GRID_W).astype(F32)
    col = (pos % GRID_W).astype(F32)
    inv_freq = jnp.power(ROPE_BASE, -jnp.arange(0, ROPE_AXIS_DIM, 2, dtype=F32) / ROPE_AXIS_DIM)
    ang_r = row[:, None] * inv_freq
    ang_c = col[:, None] * inv_freq
    zeros = jnp.zeros((length, QK_ROPE_DIM), F32)
    c_tab = jnp.concatenate([jnp.cos(ang_r), jnp.cos(ang_r), jnp.cos(ang_c), jnp.cos(ang_c), zeros], axis=1)
    s_tab = jnp.concatenate([-jnp.sin(ang_r), jnp.sin(ang_r), -jnp.sin(ang_c), jnp.sin(ang_c), zeros], axis=1)
    return c_tab, s_tab


def _identity_tables(length):
    ones = jnp.ones((length, QK_ROPE_DIM), F32)
    zeros = jnp.zeros((length, QK_ROPE_DIM), F32)
    return jnp.concatenate([ones, zeros], axis=1), jnp.zeros((length, 2 * QK_ROPE_DIM), F32)


def _pack_rows(table):
    bits = lax.bitcast_convert_type(table.astype(BF16), jnp.uint16).astype(jnp.uint32)
    return lax.bitcast_convert_type(bits[:, :HALF_W] | (bits[:, HALF_W:] << 16), jnp.int32)


def _layer_weights(w_in, q_norm_g, kv_norm_g, w_uq, w_ukv):
    perm = _rope_perm()
    p0 = Q_RANK
    p1 = p0 + KV_RANK
    p2 = p1 + QK_ROPE_DIM
    p3 = p2 + S5_WIDTH
    p4 = p3 + D_MODEL
    w_kpe = w_in[:, p1:p2]
    wckv = jnp.concatenate([w_in[:, p0:p1], w_kpe, w_kpe[:, perm]], axis=1)
    uq = w_uq.reshape(Q_RANK, N_HEADS, QK_NOPE_DIM + QK_ROPE_DIM)
    pe = uq[:, :, QK_NOPE_DIM:]
    wuq = jnp.concatenate([uq[:, :, :QK_NOPE_DIM], pe, pe[:, :, perm]], axis=2).reshape(Q_RANK, N_HEADS * HEAD_PAD)
    cast = lambda a: a.astype(BF16)
    return (cast(w_in[:, :p0]), cast(wckv), cast(w_in[:, p2:p3]), cast(w_in[:, p3:p4]), cast(w_in[:, p4:]),
            q_norm_g.reshape(1, Q_RANK), kv_norm_g.reshape(1, KV_RANK), cast(wuq), cast(w_ukv))


def kernel(x, c, ctx, c_ctx, w_mod, b_mod, w_in, q_norm_g, kv_norm_g, w_uq, w_ukv, s5_a_re, s5_a_im, s5_log_dt, s5_b_re, s5_b_im, s5_c_re, s5_c_im, s5_d, w_glu, w_out, ln1_g, ln1_b, peer_wq, peer_keys, peer_u, peer_v, ln2_g, ln2_b):
    bsz, length, _ = x.shape
    l_ctx = ctx.shape[1]
    n_tok = bsz * length
    layer = 0
    tm = 256
    s5_tt = 64
    tok_chunk = 4096
    peer_tb = 32

    cond = jnp.concatenate([c, c_ctx[None, :], jnp.zeros((7, D_MODEL), F32)], axis=0)
    mod = _modulation(cond, w_mod[layer], b_mod[layer]).reshape(cond.shape[0], N_MOD, 1, D_MODEL)
    sh_a, sc_a, g_a, sh_f, sc_f, g_f = (mod[:bsz, i] for i in range(N_MOD))
    csh_a, csc_a = mod[bsz:bsz + 1, 0], mod[bsz:bsz + 1, 1]

    wts = _layer_weights(w_in[layer], q_norm_g[layer], kv_norm_g[layer], w_uq[layer], w_ukv[layer])
    c_lat, s_lat = _rope_tables(length)
    c_id, s_id = _identity_tables(l_ctx)
    keys = peer_keys[layer].reshape(2 * PEER_HEADS, PEER_N_KEYS, PEER_HALF).astype(BF16)
    wglu, wout, wq = w_glu[layer].astype(BF16), w_out[layer].astype(BF16), peer_wq[layer].astype(BF16)
    u_pack = _pack_rows(peer_u[layer])
    v_pack = _pack_rows(peer_v[layer])
    l2g = ln2_g[layer].reshape(1, D_MODEL)
    l2b = ln2_b[layer].reshape(1, D_MODEL)

    n_groups = 2 if bsz % 16 == 0 else 1
    gb = bsz // n_groups
    g_tok = gb * length
    mats = _s5_matrices(s5_a_re[layer], s5_a_im[layer], s5_log_dt[layer], s5_b_re[layer], s5_b_im[layer],
                        s5_c_re[layer], s5_c_im[layer], gb)

    def dense_stages(b0):
        k_ctx, v_ctx, su_ctx = _inproj(ctx, csh_a, csc_a, c_id, s_id, wts, is_ctx=True, tm=min(tm, l_ctx), b0=b0, bsz=gb)
        q, k_lat, v_lat, su_lat, gm, gs = _inproj(x, sh_a, sc_a, c_lat, s_lat, wts, is_ctx=False, tm=tm, b0=b0, bsz=gb)
        att = _attention(q, k_ctx, v_ctx, k_lat, v_lat, tq=min(512, length))
        su_all = jnp.concatenate([su_ctx, su_lat], axis=0)
        y = _s5_scan(su_all.reshape(-1, S5_WIDTH), mats, bsz=gb, l_ctx=l_ctx, tt=s5_tt)
        h1, xm, st = _merge(x, att, gm, gs, su_all, y.reshape(2, length, gb * S5_WIDTH), s5_d[layer].reshape(1, S5_WIDTH),
                            wglu, wout, g_a, ln1_g[layer].reshape(1, D_MODEL), ln1_b[layer].reshape(1, D_MODEL),
                            sh_f, sc_f, wq, keys, tm=tm, ctx_tiles=l_ctx // tm, b0=b0)
        experts, gates = _peer_topk(st, lanes=128)
        idx = experts.transpose(2, 0, 1).reshape(g_tok * PEER_PAIRS)
        w2 = jnp.repeat(gates.transpose(2, 0, 1).reshape(g_tok, PEER_PAIRS), 2, axis=1)
        return h1.reshape(g_tok, D_MODEL), xm.reshape(g_tok, D_MODEL), idx, w2

    tok_chunk = min(tok_chunk, g_tok)
    outs = []
    for g in range(n_groups):
        h1f, xmf, idx, w2 = dense_stages(g * gb)
        for t0 in range(0, g_tok, tok_chunk):
            acts, vg = _sc_peer(u_pack, v_pack, idx[t0 * PEER_PAIRS:(t0 + tok_chunk) * PEER_PAIRS], xmf[t0:t0 + tok_chunk])
            act2 = jnp.repeat(acts.reshape(tok_chunk, PEER_PAIRS), 2, axis=1)
            outs.append(_peer_eval(vg, act2, w2[t0:t0 + tok_chunk], h1f[t0:t0 + tok_chunk], g_f, l2g, l2b,
                                   tb=peer_tb, tok0=g * g_tok + t0, length=length))
    return jnp.concatenate(outs, axis=0).reshape(bsz, length, D_MODEL)
```

```python
import functools
import math

import jax
import jax.numpy as jnp
from jax import lax
from jax.experimental import pallas as pl
from jax.experimental.pallas import tpu as pltpu
from jax.experimental.pallas import tpu_sc as plsc

F32 = jnp.float32
BF16 = jnp.bfloat16

D_MODEL = 1024
DEPTH = 1
GRID_W = 64
N_HEADS = 8
QK_NOPE_DIM = 128
QK_ROPE_DIM = 64
V_HEAD_DIM = 128
Q_RANK = 384
KV_RANK = 256
ROPE_AXIS_DIM = QK_ROPE_DIM // 2
ROPE_BASE = 10000.0
S5_WIDTH = D_MODEL // 2
S5_GROUP = 16
S5_GROUPS = S5_WIDTH // S5_GROUP
S5_STATE = 64
PEER_HEADS = 8
PEER_N_KEYS = 128
PEER_TOPK = 16
PEER_HALF = 128
PEER_PAIRS = PEER_HEADS * PEER_TOPK
DEEPNORM_ALPHA = (2.0 * DEPTH) ** 0.25
LN_EPS = 1e-6
N_MOD = 6
ATT_SCALE = (QK_NOPE_DIM + QK_ROPE_DIM) ** -0.5

HEAD_PAD = 256
S5_COLS = S5_GROUPS * S5_STATE
S5_CHUNKS = 2
SC_WORKERS = 32
SC_LANES = 16
DOT_WIN = 32
V_BUFS = 4
CODE_NONE = 1.0e9
PEER_GROUPS = 4
HALF_W = D_MODEL // 2
VMEM_LIMIT = 48 * 1024 * 1024


def _dot(a, b):
    return jnp.dot(a, b, preferred_element_type=F32)


def _dot_nt(a, b):
    return lax.dot_general(a, b, (((1,), (1,)), ((), ())), preferred_element_type=F32)


def _gelu(x):
    return 0.5 * x * (1.0 + jnp.tanh(0.7978845608028654 * (x + 0.044715 * (x * x * x))))


def _ln_plain(x):
    mu = jnp.mean(x, axis=-1, keepdims=True)
    xc = x - mu
    var = jnp.mean(xc * xc, axis=-1, keepdims=True)
    return xc * lax.rsqrt(var + LN_EPS)


def _rms(x, g):
    return x * lax.rsqrt(jnp.mean(x * x, axis=-1, keepdims=True) + LN_EPS) * g


def _params(*sem):
    return pltpu.CompilerParams(dimension_semantics=sem, vmem_limit_bytes=VMEM_LIMIT)


def _mod_kernel(cond_ref, w_ref, b_ref, o_ref):
    a = cond_ref[...]
    a = a * jax.nn.sigmoid(a)
    a_hi = a.astype(BF16)
    a_lo = (a - a_hi.astype(F32)).astype(BF16)
    w = w_ref[...]
    w_hi = w.astype(BF16)
    w_lo = (w - w_hi.astype(F32)).astype(BF16)
    o_ref[...] = _dot(a_hi, w_hi) + _dot(a_lo, w_hi) + _dot(a_hi, w_lo) + b_ref[...]


def _modulation(cond, w_mod, b_mod):
    rows = cond.shape[0]
    n = w_mod.shape[1]
    blk = D_MODEL
    return pl.pallas_call(
        _mod_kernel,
        grid=(n // blk,),
        in_specs=[
            pl.BlockSpec((rows, D_MODEL), lambda j: (0, 0)),
            pl.BlockSpec((D_MODEL, blk), lambda j: (0, j)),
            pl.BlockSpec((1, blk), lambda j: (0, j)),
        ],
        out_specs=pl.BlockSpec((rows, blk), lambda j: (0, j)),
        out_shape=jax.ShapeDtypeStruct((rows, n), F32),
        compiler_params=_params("arbitrary"),
        name="modulation",
    )(cond, w_mod, b_mod.reshape(1, n))


def _rope128(t, c_tab, s_tab):
    return t * c_tab + pltpu.roll(t, 64, axis=1) * s_tab


def _inproj_kernel(x_ref, sh_ref, sc_ref, c_ref, s_ref, wcq_ref, wckv_ref, wsu_ref, wgm_ref, wgs_ref,
                   qg_ref, kvg_ref, wuq_ref, wukv_ref, *out_refs, is_ctx):
    if is_ctx:
        k_ref, v_ref, su_ref = out_refs
    else:
        q_ref, k_ref, v_ref, su_ref, gm_ref, gs_ref = out_refs
    xm = _ln_plain(x_ref[0]) * (1.0 + sc_ref[0]) + sh_ref[0]
    xb = xm.astype(BF16)
    c_tab = c_ref[...]
    s_tab = s_ref[...]

    ckvpe = _dot(xb, wckv_ref[...])
    ckv_n = _rms(ckvpe[:, :KV_RANK], kvg_ref[...]).astype(BF16)
    kpe = _rope128(ckvpe[:, KV_RANK:], c_tab, s_tab).astype(BF16)
    kv = _dot(ckv_n, wukv_ref[...])
    for h in range(N_HEADS):
        k_ref[0, h, :, 0:128] = kv[:, h * 256:h * 256 + 128].astype(BF16)
        k_ref[0, h, :, 128:256] = kpe
        v_ref[0, h] = kv[:, h * 256 + 128:(h + 1) * 256].astype(BF16)

    su_ref[...] = _dot(xb, wsu_ref[...])

    if not is_ctx:
        cq_n = _rms(_dot(xb, wcq_ref[...]), qg_ref[...]).astype(BF16)
        q = _dot(cq_n, wuq_ref[...])
        for h in range(N_HEADS):
            q_ref[0, h, :, 0:128] = (q[:, h * 256:h * 256 + 128] * ATT_SCALE).astype(BF16)
            q_ref[0, h, :, 128:256] = (_rope128(q[:, h * 256 + 128:(h + 1) * 256], c_tab, s_tab) * ATT_SCALE).astype(BF16)
        gm_ref[0] = jax.nn.sigmoid(_dot(xb, wgm_ref[...])).astype(BF16)
        gs_ref[0] = jax.nn.sigmoid(_dot(xb, wgs_ref[...])).astype(BF16)


def _inproj(x, shift, scale, c_tab, s_tab, wts, *, is_ctx, tm, b0, bsz):
    length = x.shape[1]
    mod_map = (lambda b, t: (0, 0, 0)) if shift.shape[0] == 1 else (lambda b, t: (b0 + b, 0, 0))
    const2 = lambda b, t: (0, 0)
    wcq, wckv, wsu, wgm, wgs, qg, kvg, wuq, wukv = wts
    in_specs = [
        pl.BlockSpec((1, tm, D_MODEL), lambda b, t: (b0 + b, t, 0)),
        pl.BlockSpec((1, 1, D_MODEL), mod_map),
        pl.BlockSpec((1, 1, D_MODEL), mod_map),
        pl.BlockSpec((tm, 128), lambda b, t: (t, 0)),
        pl.BlockSpec((tm, 128), lambda b, t: (t, 0)),
    ] + [pl.BlockSpec(w.shape, const2) for w in (wcq, wckv, wsu, wgm, wgs, qg, kvg, wuq, wukv)]
    head_spec = lambda width: pl.BlockSpec((1, N_HEADS, tm, width), lambda b, t: (b, 0, t, 0))
    row_spec = pl.BlockSpec((1, tm, D_MODEL), lambda b, t: (b, t, 0))
    su_spec = pl.BlockSpec((tm, S5_WIDTH), lambda b, t: (t, b))
    k_shape = jax.ShapeDtypeStruct((bsz, N_HEADS, length, HEAD_PAD), BF16)
    v_shape = jax.ShapeDtypeStruct((bsz, N_HEADS, length, V_HEAD_DIM), BF16)
    su_shape = jax.ShapeDtypeStruct((length, bsz * S5_WIDTH), F32)
    g_shape = jax.ShapeDtypeStruct((bsz, length, D_MODEL), BF16)
    if is_ctx:
        out_specs = [head_spec(HEAD_PAD), head_spec(V_HEAD_DIM), su_spec]
        out_shape = [k_shape, v_shape, su_shape]
    else:
        out_specs = [head_spec(HEAD_PAD), head_spec(HEAD_PAD), head_spec(V_HEAD_DIM), su_spec, row_spec, row_spec]
        out_shape = [k_shape, k_shape, v_shape, su_shape, g_shape, g_shape]
    return pl.pallas_call(
        functools.partial(_inproj_kernel, is_ctx=is_ctx),
        grid=(bsz, length // tm),
        in_specs=in_specs,
        out_specs=out_specs,
        out_shape=out_shape,
        compiler_params=_params("parallel", "parallel"),
        name="inproj_ctx" if is_ctx else "inproj_lat",
    )(x, shift, scale, c_tab, s_tab, wcq, wckv, wsu, wgm, wgs, qg, kvg, wuq, wukv)


def _attn_kernel(q_ref, kc_ref, vc_ref, kl_ref, vl_ref, o_ref):
    q = q_ref[0, 0]
    s_c = _dot_nt(q, kc_ref[0, 0])
    s_l = _dot_nt(q, kl_ref[0, 0])
    m = jnp.maximum(jnp.max(s_c, axis=-1, keepdims=True), jnp.max(s_l, axis=-1, keepdims=True))
    p_c = jnp.exp(s_c - m)
    p_l = jnp.exp(s_l - m)
    denom = jnp.sum(p_c, axis=-1, keepdims=True) + jnp.sum(p_l, axis=-1, keepdims=True)
    o = _dot(p_c.astype(BF16), vc_ref[0, 0]) + _dot(p_l.astype(BF16), vl_ref[0, 0])
    o_ref[0] = (o / denom).astype(BF16)


def _attention(q, k_ctx, v_ctx, k_lat, v_lat, *, tq, b0, bsz):
    _, heads, length, _ = q.shape
    l_ctx = k_ctx.shape[2]
    full = lambda b, h, i: (b0 + b, h, 0, 0)
    return pl.pallas_call(
        _attn_kernel,
        grid=(bsz, heads, length // tq),
        in_specs=[
            pl.BlockSpec((1, 1, tq, HEAD_PAD), lambda b, h, i: (b0 + b, h, i, 0)),
            pl.BlockSpec((1, 1, l_ctx, HEAD_PAD), full),
            pl.BlockSpec((1, 1, l_ctx, V_HEAD_DIM), full),
            pl.BlockSpec((1, 1, length, HEAD_PAD), full),
            pl.BlockSpec((1, 1, length, V_HEAD_DIM), full),
        ],
        out_specs=pl.BlockSpec((1, tq, V_HEAD_DIM), lambda b, h, i: (b, i, h)),
        out_shape=jax.ShapeDtypeStruct((bsz, length, heads * V_HEAD_DIM), BF16),
        compiler_params=_params("parallel", "parallel", "arbitrary"),
        name="attention",
    )(q, k_ctx, v_ctx, k_lat, v_lat)


def _s5_kernel(u_ref, are_ref, aim_ref, bre_ref, bim_ref, cre_ref, cim_ref, y_ref,
               bu_re, bu_im, h_re, h_im, *, tt, bsz, n_ctx_tiles):
    d = pl.program_id(0)
    i = pl.program_id(1)

    @pl.when(i == 0)
    def _():
        h_re[...] = jnp.zeros_like(h_re)
        h_im[...] = jnp.zeros_like(h_im)

    u = u_ref[...].astype(BF16)
    half = S5_COLS // S5_CHUNKS
    cw = S5_WIDTH // S5_CHUNKS
    for c in range(S5_CHUNKS):
        uc = u[:, c * cw:(c + 1) * cw]
        bu_re[:, c * half:(c + 1) * half] = _dot(uc, bre_ref[0, c])
        bu_im[:, c * half:(c + 1) * half] = _dot(uc, bim_ref[0, c])

    col_w = 512
    for cc in range(S5_COLS // col_w):
        cols = slice(cc * col_w, (cc + 1) * col_w)
        a_r = are_ref[0, :, cols]
        a_i = aim_ref[0, :, cols]

        def step(t, carry, cols=cols, a_r=a_r, a_i=a_i):
            hr, hi = carry
            pos = jnp.where(d == 0, t, tt - 1 - t)
            r = pl.multiple_of(pos * bsz, bsz)
            nr = a_r * hr - a_i * hi + bu_re[pl.ds(r, bsz), cols]
            ni = a_r * hi + a_i * hr + bu_im[pl.ds(r, bsz), cols]
            bu_re[pl.ds(r, bsz), cols] = nr
            bu_im[pl.ds(r, bsz), cols] = ni
            return nr, ni

        hr, hi = lax.fori_loop(0, tt, step, (h_re[:, cols], h_im[:, cols]))
        h_re[:, cols] = hr
        h_im[:, cols] = hi

    @pl.when(i >= n_ctx_tiles)
    def _():
        for c in range(S5_CHUNKS):
            sl = slice(c * half, (c + 1) * half)
            y = _dot(bu_re[:, sl].astype(BF16), cre_ref[0, c]) + _dot(bu_im[:, sl].astype(BF16), cim_ref[0, c])
            y_ref[0, :, c * cw:(c + 1) * cw] = y


def _s5_scan(u_all, mats, *, bsz, l_ctx, tt):
    a_re_b, a_im_b, b_re_bd, b_im_bd, c_re_bd, c_im_bd = mats
    rows = tt * bsz
    n_tiles = u_all.shape[0] // rows
    n_ctx_tiles = l_ctx // tt
    n_lat_tiles = n_tiles - n_ctx_tiles

    def u_map(d, i):
        bwd = jnp.where(i < n_ctx_tiles, n_ctx_tiles - 1 - i, n_tiles - 1 - (i - n_ctx_tiles))
        return (jnp.where(d == 0, i, bwd), 0)

    def y_map(d, i):
        j = jnp.maximum(i - n_ctx_tiles, 0)
        return (d, jnp.where(d == 0, j, n_lat_tiles - 1 - j), 0)

    dmap3 = lambda d, i: (d, 0, 0)
    dmap4 = lambda d, i: (d, 0, 0, 0)
    return pl.pallas_call(
        functools.partial(_s5_kernel, tt=tt, bsz=bsz, n_ctx_tiles=n_ctx_tiles),
        grid=(2, n_tiles),
        in_specs=[
            pl.BlockSpec((rows, S5_WIDTH), u_map),
            pl.BlockSpec((1,) + a_re_b.shape[1:], dmap3),
            pl.BlockSpec((1,) + a_im_b.shape[1:], dmap3),
            pl.BlockSpec((1,) + b_re_bd.shape[1:], dmap4),
            pl.BlockSpec((1,) + b_im_bd.shape[1:], dmap4),
            pl.BlockSpec((1,) + c_re_bd.shape[1:], dmap4),
            pl.BlockSpec((1,) + c_im_bd.shape[1:], dmap4),
        ],
        out_specs=pl.BlockSpec((1, rows, S5_WIDTH), y_map),
        out_shape=jax.ShapeDtypeStruct((2, n_lat_tiles * rows, S5_WIDTH), F32),
        scratch_shapes=[
            pltpu.VMEM((rows, S5_COLS), F32),
            pltpu.VMEM((rows, S5_COLS), F32),
            pltpu.VMEM((bsz, S5_COLS), F32),
            pltpu.VMEM((bsz, S5_COLS), F32),
        ],
        compiler_params=_params("arbitrary", "arbitrary"),
        name="s5_scan",
    )(u_all, a_re_b, a_im_b, b_re_bd, b_im_bd, c_re_bd, c_im_bd)


def _s5_matrices(a_re, a_im, log_dt, b_re, b_im, c_re, c_im, bsz):
    dt = jnp.exp(log_dt)[..., None]
    mag = jnp.exp(a_re * dt)
    ab_re = mag * jnp.cos(a_im * dt)
    ab_im = mag * jnp.sin(a_im * dt)
    den = a_re * a_re + a_im * a_im
    f_re = ((ab_re - 1.0) * a_re + ab_im * a_im) / den
    f_im = (ab_im * a_re - (ab_re - 1.0) * a_im) / den
    bb_re = f_re[..., None] * b_re - f_im[..., None] * b_im
    bb_im = f_re[..., None] * b_im + f_im[..., None] * b_re
    gl = S5_GROUPS // S5_CHUNKS
    eye = jnp.eye(gl, dtype=F32)

    def in_bd(bb):
        t = bb.reshape(2, S5_CHUNKS, gl, S5_STATE, S5_GROUP)
        t = jnp.einsum('dkgpc,gh->dkgchp', t, eye)
        return t.reshape(2, S5_CHUNKS, gl * S5_GROUP, gl * S5_STATE).astype(BF16)

    def out_bd(cc):
        t = cc.reshape(2, S5_CHUNKS, gl, S5_GROUP, S5_STATE)
        t = jnp.einsum('dkgcp,gh->dkgphc', t, eye)
        return t.reshape(2, S5_CHUNKS, gl * S5_STATE, gl * S5_GROUP).astype(BF16)

    bcast = lambda a: jnp.broadcast_to(a.reshape(2, 1, S5_COLS), (2, bsz, S5_COLS))
    return bcast(ab_re), bcast(ab_im), in_bd(bb_re), in_bd(bb_im), out_bd(c_re), out_bd(-c_im)


def _merge_kernel(x_ref, att_ref, gm_ref, gs_ref, su_ref, y_ref, dsk_ref, wglu_ref, wout_ref, ga_ref,
                  l1g_ref, l1b_ref, shf_ref, scf_ref, wq_ref, keys_ref, h1_ref, xm_ref, st_ref):
    y = su_ref[...] * dsk_ref[...] + y_ref[0] + y_ref[1]
    gl = _dot(_gelu(y).astype(BF16), wglu_ref[...])
    s5_out = gl[:, :D_MODEL] * jax.nn.sigmoid(gl[:, D_MODEL:])
    merged = gm_ref[0].astype(F32) * att_ref[0].astype(F32) + gs_ref[0].astype(F32) * s5_out
    out = _dot(merged.astype(BF16), wout_ref[...])
    h1 = _ln_plain(DEEPNORM_ALPHA * x_ref[0] + ga_ref[0] * out) * l1g_ref[...] + l1b_ref[...]
    h1_ref[0] = h1
    xm = _ln_plain(h1) * (1.0 + scf_ref[0]) + shf_ref[0]
    xm_ref[0] = xm
    qp = _dot(xm.astype(BF16), wq_ref[...]).astype(BF16)
    for j in range(2 * PEER_HEADS):
        st_ref[j] = _dot_nt(keys_ref[j], qp[:, j * PEER_HALF:(j + 1) * PEER_HALF])


def _merge(x, att, gm, gs, su_all2d, y2d, dsk, wglu, wout, g_a, l1g, l1b, sh_f, sc_f, wq, keys, *, tm, ctx_tiles, b0):
    bsz, length, _ = att.shape
    nt = length // tm
    row = pl.BlockSpec((1, tm, D_MODEL), lambda b, t: (b, t, 0))
    xrow = pl.BlockSpec((1, tm, D_MODEL), lambda b, t: (b0 + b, t, 0))
    modb = pl.BlockSpec((1, 1, D_MODEL), lambda b, t: (b0 + b, 0, 0))
    const = lambda a: pl.BlockSpec(a.shape, lambda b, t: (0,) * a.ndim)
    return pl.pallas_call(
        _merge_kernel,
        grid=(bsz, nt),
        in_specs=[
            xrow, row, xrow, xrow,
            pl.BlockSpec((tm, S5_WIDTH), lambda b, t: (ctx_tiles + t, b0 + b)),
            pl.BlockSpec((2, tm, S5_WIDTH), lambda b, t: (0, t, b0 + b)),
            const(dsk), const(wglu), const(wout), modb, const(l1g), const(l1b), modb, modb, const(wq), const(keys),
        ],
        out_specs=[row, row, pl.BlockSpec((2 * PEER_HEADS, PEER_N_KEYS, tm), lambda b, t: (0, 0, b * nt + t))],
        out_shape=[
            jax.ShapeDtypeStruct((bsz, length, D_MODEL), F32),
            jax.ShapeDtypeStruct((bsz, length, D_MODEL), F32),
            jax.ShapeDtypeStruct((2 * PEER_HEADS, PEER_N_KEYS, bsz * length), F32),
        ],
        compiler_params=_params("parallel", "parallel"),
        name="merge_peer_query",
    )(x, att, gm, gs, su_all2d, y2d, dsk, wglu, wout, g_a, l1g, l1b, sh_f, sc_f, wq, keys)


def _take_top(vals, codes, payload, k):
    rows = lax.broadcasted_iota(jnp.int32, (k, vals.shape[1]), 0)
    top_v = jnp.zeros((k, vals.shape[1]), F32)
    top_p = jnp.zeros((k, vals.shape[1]), F32)
    for r in range(k):
        m = jnp.max(vals, axis=0, keepdims=True)
        cm = jnp.min(jnp.where(vals == m, codes, CODE_NONE), axis=0, keepdims=True)
        sel = codes == cm
        if payload is None:
            p = cm
        else:
            p = jnp.max(jnp.where(sel, payload, -1.0), axis=0, keepdims=True)
        vals = jnp.where(sel, -jnp.inf, vals)
        top_v = jnp.where(rows == r, m, top_v)
        top_p = jnp.where(rows == r, p, top_p)
    return top_v, top_p


def _topk_kernel(st_ref, e_ref, g_ref):
    lanes = st_ref.shape[2]
    key_idx = lax.broadcasted_iota(jnp.int32, (PEER_N_KEYS, lanes), 0).astype(F32)
    sub_idx = lax.broadcasted_iota(jnp.int32, (PEER_TOPK, lanes), 0).astype(F32)
    half_k = PEER_TOPK // 2
    for h in range(PEER_HEADS):
        sv0, si0 = _take_top(st_ref[2 * h], key_idx, None, PEER_TOPK)
        sv1, si1 = _take_top(st_ref[2 * h + 1], key_idx, None, PEER_TOPK)
        cs, ce, cc = [], [], []
        for i in range(half_k):
            nj = PEER_TOPK if i == 0 else half_k
            cs.append(sv0[i:i + 1, :] + sv1[0:nj, :])
            ce.append(si0[i:i + 1, :] * PEER_N_KEYS + si1[0:nj, :])
            cc.append(sub_idx[0:nj, :] + i * PEER_TOPK)
        cs.append(sv0[half_k:, :] + sv1[0:1, :])
        ce.append(si0[half_k:, :] * PEER_N_KEYS + si1[0:1, :])
        cc.append((sub_idx[0:half_k, :] + half_k) * PEER_TOPK)
        top_s, top_e = _take_top(jnp.concatenate(cs, axis=0), jnp.concatenate(cc, axis=0),
                                 jnp.concatenate(ce, axis=0), PEER_TOPK)
        p = jnp.exp(top_s - jnp.max(top_s, axis=0, keepdims=True))
        g_ref[h] = p / jnp.sum(p, axis=0, keepdims=True)
        e_ref[h] = top_e.astype(jnp.int32)


def _peer_topk(st, *, lanes):
    n = st.shape[2]
    out_spec = pl.BlockSpec((PEER_HEADS, PEER_TOPK, lanes), lambda i: (0, 0, i))
    return pl.pallas_call(
        _topk_kernel,
        grid=(n // lanes,),
        in_specs=[pl.BlockSpec((2 * PEER_HEADS, PEER_N_KEYS, lanes), lambda i: (0, 0, i))],
        out_specs=[out_spec, out_spec],
        out_shape=[jax.ShapeDtypeStruct((PEER_HEADS, PEER_TOPK, n), jnp.int32),
                   jax.ShapeDtypeStruct((PEER_HEADS, PEER_TOPK, n), F32)],
        compiler_params=_params("parallel"),
        name="peer_topk",
    )(st)


def _sc_peer(u_tab, v_tab, idx, x):
    n_tok = x.shape[0]
    per_worker = n_tok // SC_WORKERS
    n_win = PEER_PAIRS // DOT_WIN
    steps = PEER_PAIRS // SC_LANES
    per_win = DOT_WIN // SC_LANES
    mesh = plsc.VectorSubcoreMesh(core_axis_name="c", subcore_axis_name="s")

    @functools.partial(
        pl.kernel, mesh=mesh,
        out_type=(jax.ShapeDtypeStruct((n_tok * PEER_PAIRS,), F32),
                  jax.ShapeDtypeStruct((n_tok * PEER_PAIRS, HALF_W), jnp.int32)),
        scratch_types=[
            pltpu.VMEM((2, PEER_PAIRS), jnp.int32),
            pltpu.VMEM((2, PEER_PAIRS), jnp.int32),
            pltpu.VMEM((2, 2 * HALF_W), F32),
            pltpu.VMEM((n_win, DOT_WIN, HALF_W), jnp.int32),
            pltpu.VMEM((V_BUFS, SC_LANES, HALF_W), jnp.int32),
            pltpu.VMEM((2, PEER_PAIRS), F32),
            pltpu.SemaphoreType.DMA((n_win,)),
            pltpu.SemaphoreType.DMA((V_BUFS,)),
            pltpu.SemaphoreType.DMA((V_BUFS,)),
            pltpu.SemaphoreType.DMA((2,)),
            pltpu.SemaphoreType.DMA((2,)),
            pltpu.SemaphoreType.DMA((2,)),
        ],
        compiler_params=pltpu.CompilerParams(needs_layout_passes=False),
    )
    def peer(u_hbm, v_hbm, idx_hbm, x_hbm, act_hbm, vrows_hbm, idx_v, vidx_v, x_v, urows_v, vrows_v, act_v,
             u_sems, vg_sems, vw_sems, idx_sems, x_sems, act_sems):
        wid = lax.axis_index("s") * 2 + lax.axis_index("c")
        base = wid * per_worker
        last = base + per_worker - 1
        lane = lax.iota(jnp.int32, SC_LANES)

        def pair_off(tok):
            return pl.multiple_of(tok * PEER_PAIRS, PEER_PAIRS)

        def idx_copy(tok, slot):
            return pltpu.make_async_copy(idx_hbm.at[pl.ds(pair_off(tok), PEER_PAIRS)], idx_v.at[slot], idx_sems.at[slot])

        def x_copy(tok, slot):
            return pltpu.make_async_copy(x_hbm.at[tok], x_v.at[slot], x_sems.at[slot])

        def u_gather(slot, w):
            return pltpu.make_async_copy(u_hbm.at[idx_v.at[slot, pl.ds(w * DOT_WIN, DOT_WIN)]], urows_v.at[w],
                                         u_sems.at[w])

        def v_gather(slot, g):
            return pltpu.make_async_copy(v_hbm.at[vidx_v.at[slot, pl.ds(g * SC_LANES, SC_LANES)]],
                                         vrows_v.at[g % V_BUFS], vg_sems.at[g % V_BUFS])

        def v_write(tok, g):
            return pltpu.make_async_copy(vrows_v.at[g % V_BUFS],
                                         vrows_hbm.at[pl.ds(pl.multiple_of(pair_off(tok) + g * SC_LANES, SC_LANES), SC_LANES)],
                                         vw_sems.at[g % V_BUFS])

        def act_copy(tok, slot):
            return pltpu.make_async_copy(act_v.at[slot], act_hbm.at[pl.ds(pair_off(tok), PEER_PAIRS)], act_sems.at[slot])

        def step_dots(slot, g):
            w, sub = g // per_win, g % per_win

            def body(c, accs):
                xlo = x_v[slot, pl.ds(c * SC_LANES, SC_LANES)]
                xhi = x_v[slot, pl.ds(HALF_W + c * SC_LANES, SC_LANES)]
                new = []
                for r in range(SC_LANES):
                    wv = urows_v[w, sub * SC_LANES + r, pl.ds(c * SC_LANES, SC_LANES)]
                    lo = lax.bitcast_convert_type(wv << 16, F32)
                    hi = lax.bitcast_convert_type(wv & jnp.int32(-65536), F32)
                    new.append(accs[r] + lo * xlo + hi * xhi)
                return tuple(new)

            accs = lax.fori_loop(0, HALF_W // SC_LANES, body, tuple(jnp.zeros((SC_LANES,), F32) for _ in range(SC_LANES)))
            v = jnp.zeros((SC_LANES,), F32)
            for r in range(SC_LANES):
                v = jnp.where(lane == r, jnp.sum(accs[r]), v)
            act_v[slot, pl.ds(g * SC_LANES, SC_LANES)] = v

        def unless_first(i, slot, skip, fn):
            if slot == 0 and skip:
                pl.when(i > 0)(fn)
            else:
                fn()

        idx_copy(base, 0).start()
        x_copy(base, 0).start()
        idx_copy(base, 0).wait()
        for w in range(n_win):
            u_gather(0, w).start()

        @pl.loop(0, per_worker, step=2)
        def _(i):
            for slot in range(2):
                tok = base + i + slot
                nxt = jnp.minimum(tok + 1, last)
                other = 1 - slot
                idx_copy(nxt, other).start()
                x_copy(nxt, other).start()
                for q in range(PEER_PAIRS // SC_LANES):
                    vidx_v[slot, pl.ds(q * SC_LANES, SC_LANES)] = idx_v[slot, pl.ds(q * SC_LANES, SC_LANES)]
                x_copy(tok, slot).wait()

                @pl.when(i > 0)
                def _():
                    act_copy(tok, slot).wait()

                idx_copy(nxt, other).wait()
                for g in range(steps):
                    unless_first(i, slot, g < V_BUFS, lambda tok=tok, g=g: v_write(tok, g).wait())
                    v_gather(slot, g).start()
                    if g % per_win == 0:
                        u_gather(slot, g // per_win).wait()
                    step_dots(slot, g)
                    if g % per_win == per_win - 1:
                        u_gather(other, g // per_win).start()
                    gp = (g - 2) % steps
                    tokp = tok if g >= 2 else tok - 1

                    def finish(slot=slot, gp=gp, tokp=tokp):
                        v_gather(slot, gp).wait()
                        v_write(tokp, gp).start()

                    unless_first(i, slot, g < 2, finish)
                act_copy(tok, slot).start()

        for gp in (steps - 2, steps - 1):
            v_gather(1, gp).wait()
            v_write(last, gp).start()
        for b in range(V_BUFS):
            v_write(last, b).wait()
        for w in range(n_win):
            u_gather(0, w).wait()
        x_copy(last, 0).wait()
        for slot in range(2):
            act_copy(last, slot).wait()

    return peer(u_tab, v_tab, idx, x)


def _peer_kernel(vg_ref, act2_ref, w2_ref, h1_ref, gf_ref, l2g_ref, l2b_ref, o_ref, f_ref, *, tb):
    sub = 8
    span = sub * PEER_PAIRS
    pair_w = 2 * PEER_PAIRS
    half = D_MODEL // 2
    lane = lax.broadcasted_iota(jnp.int32, (sub, 2 * span), 1)
    diag = (lane // pair_w) == lax.broadcasted_iota(jnp.int32, (sub, 2 * span), 0)
    even = (lane % 2) == 0
    for s in range(tb // sub):
        rows = slice(s * span, (s + 1) * span)
        toks = slice(s * sub, (s + 1) * sub)
        coef = w2_ref[toks, :] * _gelu(act2_ref[toks, :])
        cdiag = jnp.where(diag, jnp.concatenate([coef] * sub, axis=1), 0.0)
        c2 = jnp.concatenate([jnp.where(even, cdiag, 0.0), jnp.where(even, 0.0, cdiag)], axis=0).astype(BF16)
        f2 = _dot(c2, pltpu.bitcast(vg_ref[rows, :], BF16))
        f_ref[toks, :half] = f2[:sub]
        f_ref[toks, half:] = f2[sub:]
    r = DEEPNORM_ALPHA * h1_ref[...] + gf_ref[0] * f_ref[...]
    o_ref[...] = _ln_plain(r) * l2g_ref[...] + l2b_ref[...]


def _peer_eval(vg, act2, w2, h1, g_f, l2g, l2b, *, tb, tok0, length):
    n = h1.shape[0]
    tok_blk0 = tok0 // tb
    per_b = length // tb
    tok_spec = lambda width: pl.BlockSpec((tb, width), lambda i: (i, 0))
    return pl.pallas_call(
        functools.partial(_peer_kernel, tb=tb),
        grid=(n // tb,),
        in_specs=[
            pl.BlockSpec((tb * PEER_PAIRS, D_MODEL // 2), lambda i: (i, 0)),
            tok_spec(2 * PEER_PAIRS), tok_spec(2 * PEER_PAIRS), tok_spec(D_MODEL),
            pl.BlockSpec((1, 1, D_MODEL), lambda i: ((tok_blk0 + i) // per_b, 0, 0)),
            pl.BlockSpec((1, D_MODEL), lambda i: (0, 0)),
            pl.BlockSpec((1, D_MODEL), lambda i: (0, 0)),
        ],
        out_specs=tok_spec(D_MODEL),
        out_shape=jax.ShapeDtypeStruct((n, D_MODEL), F32),
        scratch_shapes=[pltpu.VMEM((tb, D_MODEL), F32)],
        compiler_params=_params("parallel"),
        name="peer_eval",
    )(vg, act2, w2, h1, g_f, l2g, l2b)


def _rope_perm():
    half = ROPE_AXIS_DIM // 2
    base = jnp.arange(QK_ROPE_DIM)
    return jnp.where((base % ROPE_AXIS_DIM) < half, base + half, base - half)


def _rope_tables(length):
    pos = jnp.arange(length)
    row = (pos // GRID_W).astype(F32)
    col = (pos % GRID_W).astype(F32)
    inv_freq = jnp.power(ROPE_BASE, -jnp.arange(0, ROPE_AXIS_DIM, 2, dtype=F32) / ROPE_AXIS_DIM)
    ang_r = row[:, None] * inv_freq
    ang_c = col[:, None] * inv_freq
    zeros = jnp.zeros((length, QK_ROPE_DIM), F32)
    c_tab = jnp.concatenate([jnp.cos(ang_r), jnp.cos(ang_r), jnp.cos(ang_c), jnp.cos(ang_c), zeros], axis=1)
    s_tab = jnp.concatenate([-jnp.sin(ang_r), jnp.sin(ang_r), -jnp.sin(ang_c), jnp.sin(ang_c), zeros], axis=1)
    return c_tab, s_tab


def _identity_tables(length):
    ones = jnp.ones((length, QK_ROPE_DIM), F32)
    zeros = jnp.zeros((length, QK_ROPE_DIM), F32)
    return jnp.concatenate([ones, zeros], axis=1), jnp.zeros((length, 2 * QK_ROPE_DIM), F32)


def _pack_rows(table):
    bits = lax.bitcast_convert_type(table.astype(BF16), jnp.uint16).astype(jnp.uint32)
    return lax.bitcast_convert_type(bits[:, :HALF_W] | (bits[:, HALF_W:] << 16), jnp.int32)


def _layer_weights(w_in, q_norm_g, kv_norm_g, w_uq, w_ukv):
    perm = _rope_perm()
    p0 = Q_RANK
    p1 = p0 + KV_RANK
    p2 = p1 + QK_ROPE_DIM
    p3 = p2 + S5_WIDTH
    p4 = p3 + D_MODEL
    w_kpe = w_in[:, p1:p2]
    wckv = jnp.concatenate([w_in[:, p0:p1], w_kpe, w_kpe[:, perm]], axis=1)
    uq = w_uq.reshape(Q_RANK, N_HEADS, QK_NOPE_DIM + QK_ROPE_DIM)
    pe = uq[:, :, QK_NOPE_DIM:]
    wuq = jnp.concatenate([uq[:, :, :QK_NOPE_DIM], pe, pe[:, :, perm]], axis=2).reshape(Q_RANK, N_HEADS * HEAD_PAD)
    cast = lambda a: a.astype(BF16)
    return (cast(w_in[:, :p0]), cast(wckv), cast(w_in[:, p2:p3]), cast(w_in[:, p3:p4]), cast(w_in[:, p4:]),
            q_norm_g.reshape(1, Q_RANK), kv_norm_g.reshape(1, KV_RANK), cast(wuq), cast(w_ukv))


def kernel(x, c, ctx, c_ctx, w_mod, b_mod, w_in, q_norm_g, kv_norm_g, w_uq, w_ukv, s5_a_re, s5_a_im, s5_log_dt, s5_b_re, s5_b_im, s5_c_re, s5_c_im, s5_d, w_glu, w_out, ln1_g, ln1_b, peer_wq, peer_keys, peer_u, peer_v, ln2_g, ln2_b):
    bsz, length, _ = x.shape
    l_ctx = ctx.shape[1]
    n_tok = bsz * length
    layer = 0
    tm = 256
    s5_tt = 64
    peer_tb = 32

    cond = jnp.concatenate([c, c_ctx[None, :], jnp.zeros((7, D_MODEL), F32)], axis=0)
    mod = _modulation(cond, w_mod[layer], b_mod[layer]).reshape(cond.shape[0], N_MOD, 1, D_MODEL)
    sh_a, sc_a, g_a, sh_f, sc_f, g_f = (mod[:bsz, i] for i in range(N_MOD))
    csh_a, csc_a = mod[bsz:bsz + 1, 0], mod[bsz:bsz + 1, 1]

    wts = _layer_weights(w_in[layer], q_norm_g[layer], kv_norm_g[layer], w_uq[layer], w_ukv[layer])
    c_lat, s_lat = _rope_tables(length)
    c_id, s_id = _identity_tables(l_ctx)
    keys = peer_keys[layer].reshape(2 * PEER_HEADS, PEER_N_KEYS, PEER_HALF).astype(BF16)
    wglu, wout, wq = w_glu[layer].astype(BF16), w_out[layer].astype(BF16), peer_wq[layer].astype(BF16)
    u_pack = _pack_rows(peer_u[layer])
    v_pack = _pack_rows(peer_v[layer])
    l2g = ln2_g[layer].reshape(1, D_MODEL)
    l2b = ln2_b[layer].reshape(1, D_MODEL)

    k_ctx, v_ctx, su_ctx = _inproj(ctx, csh_a, csc_a, c_id, s_id, wts, is_ctx=True, tm=min(tm, l_ctx), b0=0, bsz=bsz)
    q, k_lat, v_lat, su_lat, gm, gs = _inproj(x, sh_a, sc_a, c_lat, s_lat, wts, is_ctx=False, tm=tm, b0=0, bsz=bsz)
    su_all = jnp.concatenate([su_ctx, su_lat], axis=0)
    mats = _s5_matrices(s5_a_re[layer], s5_a_im[layer], s5_log_dt[layer], s5_b_re[layer], s5_b_im[layer],
                        s5_c_re[layer], s5_c_im[layer], bsz)
    y = _s5_scan(su_all.reshape(-1, S5_WIDTH), mats, bsz=bsz, l_ctx=l_ctx, tt=s5_tt)
    y2d = y.reshape(2, length, bsz * S5_WIDTH)

    n_groups = PEER_GROUPS if bsz % PEER_GROUPS == 0 else 1
    gb = bsz // n_groups
    g_tok = gb * length
    outs = []
    ready = (u_pack, v_pack)
    for g in range(n_groups):
        b0 = g * gb
        q_g, _ = lax.optimization_barrier((q, ready))
        att = _attention(q_g, k_ctx, v_ctx, k_lat, v_lat, tq=min(512, length), b0=b0, bsz=gb)
        h1, xm, st = _merge(x, att, gm, gs, su_all, y2d, s5_d[layer].reshape(1, S5_WIDTH), wglu, wout, g_a,
                            ln1_g[layer].reshape(1, D_MODEL), ln1_b[layer].reshape(1, D_MODEL), sh_f, sc_f, wq, keys,
                            tm=tm, ctx_tiles=l_ctx // tm, b0=b0)
        experts, gates = _peer_topk(st, lanes=128)
        idx = experts.transpose(2, 0, 1).reshape(g_tok * PEER_PAIRS)
        w2 = jnp.repeat(gates.transpose(2, 0, 1).reshape(g_tok, PEER_PAIRS), 2, axis=1)
        xmf = xm.reshape(g_tok, D_MODEL)
        acts, vg = _sc_peer(u_pack, v_pack, idx, xmf)
        act2 = jnp.repeat(acts.reshape(g_tok, PEER_PAIRS), 2, axis=1)
        outs.append(_peer_eval(vg, act2, w2, h1.reshape(g_tok, D_MODEL), g_f, l2g, l2b,
                               tb=peer_tb, tok0=g * g_tok, length=length))
        ready = (idx, w2, xmf) + ((outs[g - 1],) if g >= 1 else ())
    return jnp.concatenate(outs, axis=0).reshape(bsz, length, D_MODEL)
```

```python
import functools
import math

import jax
import jax.numpy as jnp
from jax import lax
from jax.experimental import pallas as pl
from jax.experimental.pallas import tpu as pltpu
from jax.experimental.pallas import tpu_sc as plsc

F32 = jnp.float32
BF16 = jnp.bfloat16

D_MODEL = 1024
DEPTH = 1
GRID_W = 64
N_HEADS = 8
QK_NOPE_DIM = 128
QK_ROPE_DIM = 64
V_HEAD_DIM = 128
Q_RANK = 384
KV_RANK = 256
ROPE_AXIS_DIM = QK_ROPE_DIM // 2
ROPE_BASE = 10000.0
S5_WIDTH = D_MODEL // 2
S5_GROUP = 16
S5_GROUPS = S5_WIDTH // S5_GROUP
S5_STATE = 64
PEER_HEADS = 8
PEER_N_KEYS = 128
PEER_TOPK = 16
PEER_HALF = 128
PEER_PAIRS = PEER_HEADS * PEER_TOPK
DEEPNORM_ALPHA = (2.0 * DEPTH) ** 0.25
LN_EPS = 1e-6
N_MOD = 6
ATT_SCALE = (QK_NOPE_DIM + QK_ROPE_DIM) ** -0.5

HEAD_PAD = 256
S5_COLS = S5_GROUPS * S5_STATE
S5_CHUNKS = 2
SC_WORKERS = 32
SC_LANES = 16
DOT_WIN = 32
V_BUFS = 4
CODE_NONE = 1.0e9
PEER_GROUPS = 4
HALF_W = D_MODEL // 2
VMEM_LIMIT = 48 * 1024 * 1024


def _dot(a, b):
    return jnp.dot(a, b, preferred_element_type=F32)


def _dot_nt(a, b):
    return lax.dot_general(a, b, (((1,), (1,)), ((), ())), preferred_element_type=F32)


def _gelu(x):
    return 0.5 * x * (1.0 + jnp.tanh(0.7978845608028654 * (x + 0.044715 * (x * x * x))))


def _ln_plain(x):
    mu = jnp.mean(x, axis=-1, keepdims=True)
    xc = x - mu
    var = jnp.mean(xc * xc, axis=-1, keepdims=True)
    return xc * lax.rsqrt(var + LN_EPS)


def _rms(x, g):
    return x * lax.rsqrt(jnp.mean(x * x, axis=-1, keepdims=True) + LN_EPS) * g


def _params(*sem):
    return pltpu.CompilerParams(dimension_semantics=sem, vmem_limit_bytes=VMEM_LIMIT)


def _cost(flops, nbytes, transcendentals=0):
    return pl.CostEstimate(flops=int(flops), transcendentals=int(transcendentals), bytes_accessed=int(nbytes))


def _mod_kernel(cond_ref, w_ref, b_ref, o_ref):
    a = cond_ref[...]
    a = a * jax.nn.sigmoid(a)
    a_hi = a.astype(BF16)
    a_lo = (a - a_hi.astype(F32)).astype(BF16)
    w = w_ref[...]
    w_hi = w.astype(BF16)
    w_lo = (w - w_hi.astype(F32)).astype(BF16)
    o_ref[...] = _dot(a_hi, w_hi) + _dot(a_lo, w_hi) + _dot(a_hi, w_lo) + b_ref[...]


def _modulation(cond, w_mod, b_mod):
    rows = cond.shape[0]
    n = w_mod.shape[1]
    blk = D_MODEL
    return pl.pallas_call(
        _mod_kernel,
        grid=(n // blk,),
        in_specs=[
            pl.BlockSpec((rows, D_MODEL), lambda j: (0, 0)),
            pl.BlockSpec((D_MODEL, blk), lambda j: (0, j)),
            pl.BlockSpec((1, blk), lambda j: (0, j)),
        ],
        out_specs=pl.BlockSpec((rows, blk), lambda j: (0, j)),
        out_shape=jax.ShapeDtypeStruct((rows, n), F32),
        compiler_params=_params("arbitrary"),
        cost_estimate=_cost(6 * rows * D_MODEL * n, 4 * D_MODEL * n),
        name="modulation",
    )(cond, w_mod, b_mod.reshape(1, n))


def _rope128(t, c_tab, s_tab):
    return t * c_tab + pltpu.roll(t, 64, axis=1) * s_tab


def _inproj_kernel(x_ref, sh_ref, sc_ref, c_ref, s_ref, wcq_ref, wckv_ref, wsu_ref, wgm_ref, wgs_ref,
                   qg_ref, kvg_ref, wuq_ref, wukv_ref, *out_refs, is_ctx):
    if is_ctx:
        k_ref, v_ref, su_ref = out_refs
    else:
        q_ref, k_ref, v_ref, su_ref, gm_ref, gs_ref = out_refs
    xm = _ln_plain(x_ref[0]) * (1.0 + sc_ref[0]) + sh_ref[0]
    xb = xm.astype(BF16)
    c_tab = c_ref[...]
    s_tab = s_ref[...]

    ckvpe = _dot(xb, wckv_ref[...])
    ckv_n = _rms(ckvpe[:, :KV_RANK], kvg_ref[...]).astype(BF16)
    kpe = _rope128(ckvpe[:, KV_RANK:], c_tab, s_tab).astype(BF16)
    kv = _dot(ckv_n, wukv_ref[...])
    for h in range(N_HEADS):
        k_ref[0, h, :, 0:128] = kv[:, h * 256:h * 256 + 128].astype(BF16)
        k_ref[0, h, :, 128:256] = kpe
        v_ref[0, h] = kv[:, h * 256 + 128:(h + 1) * 256].astype(BF16)

    su_ref[...] = _dot(xb, wsu_ref[...])

    if not is_ctx:
        cq_n = _rms(_dot(xb, wcq_ref[...]), qg_ref[...]).astype(BF16)
        q = _dot(cq_n, wuq_ref[...])
        for h in range(N_HEADS):
            q_ref[0, h, :, 0:128] = (q[:, h * 256:h * 256 + 128] * ATT_SCALE).astype(BF16)
            q_ref[0, h, :, 128:256] = (_rope128(q[:, h * 256 + 128:(h + 1) * 256], c_tab, s_tab) * ATT_SCALE).astype(BF16)
        gm_ref[0] = jax.nn.sigmoid(_dot(xb, wgm_ref[...])).astype(BF16)
        gs_ref[0] = jax.nn.sigmoid(_dot(xb, wgs_ref[...])).astype(BF16)


def _inproj(x, shift, scale, c_tab, s_tab, wts, *, is_ctx, tm, b0, bsz):
    length = x.shape[1]
    mod_map = (lambda b, t: (0, 0, 0)) if shift.shape[0] == 1 else (lambda b, t: (b0 + b, 0, 0))
    const2 = lambda b, t: (0, 0)
    wcq, wckv, wsu, wgm, wgs, qg, kvg, wuq, wukv = wts
    in_specs = [
        pl.BlockSpec((1, tm, D_MODEL), lambda b, t: (b0 + b, t, 0)),
        pl.BlockSpec((1, 1, D_MODEL), mod_map),
        pl.BlockSpec((1, 1, D_MODEL), mod_map),
        pl.BlockSpec((tm, 128), lambda b, t: (t, 0)),
        pl.BlockSpec((tm, 128), lambda b, t: (t, 0)),
    ] + [pl.BlockSpec(w.shape, const2) for w in (wcq, wckv, wsu, wgm, wgs, qg, kvg, wuq, wukv)]
    head_spec = lambda width: pl.BlockSpec((1, N_HEADS, tm, width), lambda b, t: (b, 0, t, 0))
    row_spec = pl.BlockSpec((1, tm, D_MODEL), lambda b, t: (b, t, 0))
    su_spec = pl.BlockSpec((tm, S5_WIDTH), lambda b, t: (t, b))
    k_shape = jax.ShapeDtypeStruct((bsz, N_HEADS, length, HEAD_PAD), BF16)
    v_shape = jax.ShapeDtypeStruct((bsz, N_HEADS, length, V_HEAD_DIM), BF16)
    su_shape = jax.ShapeDtypeStruct((length, bsz * S5_WIDTH), F32)
    g_shape = jax.ShapeDtypeStruct((bsz, length, D_MODEL), BF16)
    if is_ctx:
        out_specs = [head_spec(HEAD_PAD), head_spec(V_HEAD_DIM), su_spec]
        out_shape = [k_shape, v_shape, su_shape]
    else:
        out_specs = [head_spec(HEAD_PAD), head_spec(HEAD_PAD), head_spec(V_HEAD_DIM), su_spec, row_spec, row_spec]
        out_shape = [k_shape, k_shape, v_shape, su_shape, g_shape, g_shape]
    n = bsz * length
    in_cols = wckv.shape[1] + wsu.shape[1] + (0 if is_ctx else wcq.shape[1] + wgm.shape[1] + wgs.shape[1])
    up_cols = wukv.shape[1] + (0 if is_ctx else wuq.shape[1])
    flops = 2 * n * (D_MODEL * in_cols + KV_RANK * wukv.shape[1] + (0 if is_ctx else Q_RANK * wuq.shape[1]))
    nbytes = n * (4 * D_MODEL + 2 * up_cols + 4 * S5_WIDTH + (0 if is_ctx else 4 * D_MODEL))
    return pl.pallas_call(
        functools.partial(_inproj_kernel, is_ctx=is_ctx),
        grid=(bsz, length // tm),
        in_specs=in_specs,
        out_specs=out_specs,
        out_shape=out_shape,
        compiler_params=_params("parallel", "parallel"),
        cost_estimate=_cost(flops, nbytes, 0 if is_ctx else 2 * n * D_MODEL),
        name="inproj_ctx" if is_ctx else "inproj_lat",
    )(x, shift, scale, c_tab, s_tab, wcq, wckv, wsu, wgm, wgs, qg, kvg, wuq, wukv)


def _attn_kernel(q_ref, kc_ref, vc_ref, kl_ref, vl_ref, o_ref):
    q = q_ref[0, 0]
    s_c = _dot_nt(q, kc_ref[0, 0])
    s_l = _dot_nt(q, kl_ref[0, 0])
    m = jnp.maximum(jnp.max(s_c, axis=-1, keepdims=True), jnp.max(s_l, axis=-1, keepdims=True))
    p_c = jnp.exp(s_c - m)
    p_l = jnp.exp(s_l - m)
    denom = jnp.sum(p_c, axis=-1, keepdims=True) + jnp.sum(p_l, axis=-1, keepdims=True)
    o = _dot(p_c.astype(BF16), vc_ref[0, 0]) + _dot(p_l.astype(BF16), vl_ref[0, 0])
    o_ref[0] = (o / denom).astype(BF16)


def _attention(q, k_ctx, v_ctx, k_lat, v_lat, *, tq, b0, bsz):
    _, heads, length, _ = q.shape
    l_ctx = k_ctx.shape[2]
    full = lambda b, h, i: (b0 + b, h, 0, 0)
    return pl.pallas_call(
        _attn_kernel,
        grid=(bsz, heads, length // tq),
        in_specs=[
            pl.BlockSpec((1, 1, tq, HEAD_PAD), lambda b, h, i: (b0 + b, h, i, 0)),
            pl.BlockSpec((1, 1, l_ctx, HEAD_PAD), full),
            pl.BlockSpec((1, 1, l_ctx, V_HEAD_DIM), full),
            pl.BlockSpec((1, 1, length, HEAD_PAD), full),
            pl.BlockSpec((1, 1, length, V_HEAD_DIM), full),
        ],
        out_specs=pl.BlockSpec((1, tq, V_HEAD_DIM), lambda b, h, i: (b, i, h)),
        out_shape=jax.ShapeDtypeStruct((bsz, length, heads * V_HEAD_DIM), BF16),
        compiler_params=_params("parallel", "parallel", "arbitrary"),
        cost_estimate=_cost(2 * bsz * heads * length * (l_ctx + length) * (HEAD_PAD + V_HEAD_DIM),
                            2 * bsz * heads * ((2 * length + l_ctx) * HEAD_PAD + (2 * length + l_ctx) * V_HEAD_DIM),
                            bsz * heads * length * (l_ctx + length)),
        name="attention",
    )(q, k_ctx, v_ctx, k_lat, v_lat)


def _s5_kernel(u_ref, are_ref, aim_ref, bre_ref, bim_ref, cre_ref, cim_ref, y_ref,
               bu_re, bu_im, h_re, h_im, *, tt, bsz, n_ctx_tiles):
    d = pl.program_id(0)
    i = pl.program_id(1)

    @pl.when(i == 0)
    def _():
        h_re[...] = jnp.zeros_like(h_re)
        h_im[...] = jnp.zeros_like(h_im)

    u = u_ref[...].astype(BF16)
    half = S5_COLS // S5_CHUNKS
    cw = S5_WIDTH // S5_CHUNKS
    for c in range(S5_CHUNKS):
        uc = u[:, c * cw:(c + 1) * cw]
        bu_re[:, c * half:(c + 1) * half] = _dot(uc, bre_ref[0, c])
        bu_im[:, c * half:(c + 1) * half] = _dot(uc, bim_ref[0, c])

    col_w = 512
    for cc in range(S5_COLS // col_w):
        cols = slice(cc * col_w, (cc + 1) * col_w)
        a_r = are_ref[0, :, cols]
        a_i = aim_ref[0, :, cols]

        def step(t, carry, cols=cols, a_r=a_r, a_i=a_i):
            hr, hi = carry
            pos = jnp.where(d == 0, t, tt - 1 - t)
            r = pl.multiple_of(pos * bsz, bsz)
            nr = a_r * hr - a_i * hi + bu_re[pl.ds(r, bsz), cols]
            ni = a_r * hi + a_i * hr + bu_im[pl.ds(r, bsz), cols]
            bu_re[pl.ds(r, bsz), cols] = nr
            bu_im[pl.ds(r, bsz), cols] = ni
            return nr, ni

        hr, hi = lax.fori_loop(0, tt, step, (h_re[:, cols], h_im[:, cols]))
        h_re[:, cols] = hr
        h_im[:, cols] = hi

    @pl.when(i >= n_ctx_tiles)
    def _():
        for c in range(S5_CHUNKS):
            sl = slice(c * half, (c + 1) * half)
            y = _dot(bu_re[:, sl].astype(BF16), cre_ref[0, c]) + _dot(bu_im[:, sl].astype(BF16), cim_ref[0, c])
            y_ref[0, :, c * cw:(c + 1) * cw] = y


def _s5_scan(u_all, mats, *, bsz, l_ctx, tt):
    a_re_b, a_im_b, b_re_bd, b_im_bd, c_re_bd, c_im_bd = mats
    rows = tt * bsz
    n_tiles = u_all.shape[0] // rows
    n_ctx_tiles = l_ctx // tt
    n_lat_tiles = n_tiles - n_ctx_tiles

    def u_map(d, i):
        bwd = jnp.where(i < n_ctx_tiles, n_ctx_tiles - 1 - i, n_tiles - 1 - (i - n_ctx_tiles))
        return (jnp.where(d == 0, i, bwd), 0)

    def y_map(d, i):
        j = jnp.maximum(i - n_ctx_tiles, 0)
        return (d, jnp.where(d == 0, j, n_lat_tiles - 1 - j), 0)

    dmap3 = lambda d, i: (d, 0, 0)
    dmap4 = lambda d, i: (d, 0, 0, 0)
    return pl.pallas_call(
        functools.partial(_s5_kernel, tt=tt, bsz=bsz, n_ctx_tiles=n_ctx_tiles),
        grid=(2, n_tiles),
        in_specs=[
            pl.BlockSpec((rows, S5_WIDTH), u_map),
            pl.BlockSpec((1,) + a_re_b.shape[1:], dmap3),
            pl.BlockSpec((1,) + a_im_b.shape[1:], dmap3),
            pl.BlockSpec((1,) + b_re_bd.shape[1:], dmap4),
            pl.BlockSpec((1,) + b_im_bd.shape[1:], dmap4),
            pl.BlockSpec((1,) + c_re_bd.shape[1:], dmap4),
            pl.BlockSpec((1,) + c_im_bd.shape[1:], dmap4),
        ],
        out_specs=pl.BlockSpec((1, rows, S5_WIDTH), y_map),
        out_shape=jax.ShapeDtypeStruct((2, n_lat_tiles * rows, S5_WIDTH), F32),
        scratch_shapes=[
            pltpu.VMEM((rows, S5_COLS), F32),
            pltpu.VMEM((rows, S5_COLS), F32),
            pltpu.VMEM((bsz, S5_COLS), F32),
            pltpu.VMEM((bsz, S5_COLS), F32),
        ],
        compiler_params=_params("arbitrary", "arbitrary"),
        cost_estimate=_cost(2 * u_all.shape[0] * (8 * S5_WIDTH * S5_COLS // S5_CHUNKS + 8 * S5_COLS),
                            8 * (u_all.shape[0] + n_lat_tiles * rows) * S5_WIDTH),
        name="s5_scan",
    )(u_all, a_re_b, a_im_b, b_re_bd, b_im_bd, c_re_bd, c_im_bd)


def _s5_matrices(a_re, a_im, log_dt, b_re, b_im, c_re, c_im, bsz):
    dt = jnp.exp(log_dt)[..., None]
    mag = jnp.exp(a_re * dt)
    ab_re = mag * jnp.cos(a_im * dt)
    ab_im = mag * jnp.sin(a_im * dt)
    den = a_re * a_re + a_im * a_im
    f_re = ((ab_re - 1.0) * a_re + ab_im * a_im) / den
    f_im = (ab_im * a_re - (ab_re - 1.0) * a_im) / den
    bb_re = f_re[..., None] * b_re - f_im[..., None] * b_im
    bb_im = f_re[..., None] * b_im + f_im[..., None] * b_re
    gl = S5_GROUPS // S5_CHUNKS
    eye = jnp.eye(gl, dtype=F32)

    def in_bd(bb):
        t = bb.reshape(2, S5_CHUNKS, gl, S5_STATE, S5_GROUP)
        t = jnp.einsum('dkgpc,gh->dkgchp', t, eye)
        return t.reshape(2, S5_CHUNKS, gl * S5_GROUP, gl * S5_STATE).astype(BF16)

    def out_bd(cc):
        t = cc.reshape(2, S5_CHUNKS, gl, S5_GROUP, S5_STATE)
        t = jnp.einsum('dkgcp,gh->dkgphc', t, eye)
        return t.reshape(2, S5_CHUNKS, gl * S5_STATE, gl * S5_GROUP).astype(BF16)

    bcast = lambda a: jnp.broadcast_to(a.reshape(2, 1, S5_COLS), (2, bsz, S5_COLS))
    return bcast(ab_re), bcast(ab_im), in_bd(bb_re), in_bd(bb_im), out_bd(c_re), out_bd(-c_im)


def _merge_kernel(x_ref, att_ref, gm_ref, gs_ref, su_ref, y_ref, dsk_ref, wglu_ref, wout_ref, ga_ref,
                  l1g_ref, l1b_ref, shf_ref, scf_ref, wq_ref, keys_ref, h1_ref, xm_ref, st_ref):
    y = su_ref[...] * dsk_ref[...] + y_ref[0] + y_ref[1]
    gl = _dot(_gelu(y).astype(BF16), wglu_ref[...])
    s5_out = gl[:, :D_MODEL] * jax.nn.sigmoid(gl[:, D_MODEL:])
    merged = gm_ref[0].astype(F32) * att_ref[0].astype(F32) + gs_ref[0].astype(F32) * s5_out
    out = _dot(merged.astype(BF16), wout_ref[...])
    h1 = _ln_plain(DEEPNORM_ALPHA * x_ref[0] + ga_ref[0] * out) * l1g_ref[...] + l1b_ref[...]
    h1_ref[0] = h1
    xm = _ln_plain(h1) * (1.0 + scf_ref[0]) + shf_ref[0]
    xm_ref[0] = xm
    qp = _dot(xm.astype(BF16), wq_ref[...]).astype(BF16)
    for j in range(2 * PEER_HEADS):
        st_ref[j] = _dot_nt(keys_ref[j], qp[:, j * PEER_HALF:(j + 1) * PEER_HALF])


def _merge(x, att, gm, gs, su_all2d, y2d, dsk, wglu, wout, g_a, l1g, l1b, sh_f, sc_f, wq, keys, *, tm, ctx_tiles, b0, h0):
    bsz, length, _ = att.shape
    nt = length // tm
    row = pl.BlockSpec((1, tm, D_MODEL), lambda b, t: (b, t, 0))
    xrow = pl.BlockSpec((1, tm, D_MODEL), lambda b, t: (b0 + b, t, 0))
    hrow = pl.BlockSpec((1, tm, D_MODEL), lambda b, t: (h0 + b, t, 0))
    modb = pl.BlockSpec((1, 1, D_MODEL), lambda b, t: (b0 + b, 0, 0))
    const = lambda a: pl.BlockSpec(a.shape, lambda b, t: (0,) * a.ndim)
    return pl.pallas_call(
        _merge_kernel,
        grid=(bsz, nt),
        in_specs=[
            xrow, row, hrow, hrow,
            pl.BlockSpec((tm, S5_WIDTH), lambda b, t: (ctx_tiles + t, h0 + b)),
            pl.BlockSpec((2, tm, S5_WIDTH), lambda b, t: (0, t, h0 + b)),
            const(dsk), const(wglu), const(wout), modb, const(l1g), const(l1b), modb, modb, const(wq), const(keys),
        ],
        out_specs=[row, row, pl.BlockSpec((2 * PEER_HEADS, PEER_N_KEYS, tm), lambda b, t: (0, 0, b * nt + t))],
        out_shape=[
            jax.ShapeDtypeStruct((bsz, length, D_MODEL), F32),
            jax.ShapeDtypeStruct((bsz, length, D_MODEL), F32),
            jax.ShapeDtypeStruct((2 * PEER_HEADS, PEER_N_KEYS, bsz * length), F32),
        ],
        compiler_params=_params("parallel", "parallel"),
        cost_estimate=_cost(2 * bsz * length * (S5_WIDTH * 2 * D_MODEL + D_MODEL * D_MODEL + D_MODEL * wq.shape[1]
                                                + wq.shape[1] * PEER_N_KEYS),
                            bsz * length * (4 * D_MODEL * 3 + 2 * D_MODEL * 3 + 4 * S5_WIDTH * 3 + 4 * wq.shape[1]),
                            bsz * length * (S5_WIDTH + 3 * D_MODEL)),
        name="merge_peer_query",
    )(x, att, gm, gs, su_all2d, y2d, dsk, wglu, wout, g_a, l1g, l1b, sh_f, sc_f, wq, keys)


def _take_top(vals, codes, payload, k):
    rows = lax.broadcasted_iota(jnp.int32, (k, vals.shape[1]), 0)
    top_v = jnp.zeros((k, vals.shape[1]), F32)
    top_p = jnp.zeros((k, vals.shape[1]), F32)
    for r in range(k):
        m = jnp.max(vals, axis=0, keepdims=True)
        cm = jnp.min(jnp.where(vals == m, codes, CODE_NONE), axis=0, keepdims=True)
        sel = codes == cm
        if payload is None:
            p = cm
        else:
            p = jnp.max(jnp.where(sel, payload, -1.0), axis=0, keepdims=True)
        vals = jnp.where(sel, -jnp.inf, vals)
        top_v = jnp.where(rows == r, m, top_v)
        top_p = jnp.where(rows == r, p, top_p)
    return top_v, top_p


def _topk_kernel(st_ref, e_ref, g_ref):
    lanes = st_ref.shape[2]
    key_idx = lax.broadcasted_iota(jnp.int32, (PEER_N_KEYS, lanes), 0).astype(F32)
    sub_idx = lax.broadcasted_iota(jnp.int32, (PEER_TOPK, lanes), 0).astype(F32)
    half_k = PEER_TOPK // 2
    for h in range(PEER_HEADS):
        sv0, si0 = _take_top(st_ref[2 * h], key_idx, None, PEER_TOPK)
        sv1, si1 = _take_top(st_ref[2 * h + 1], key_idx, None, PEER_TOPK)
        cs, ce, cc = [], [], []
        for i in range(half_k):
            nj = PEER_TOPK if i == 0 else half_k
            cs.append(sv0[i:i + 1, :] + sv1[0:nj, :])
            ce.append(si0[i:i + 1, :] * PEER_N_KEYS + si1[0:nj, :])
            cc.append(sub_idx[0:nj, :] + i * PEER_TOPK)
        cs.append(sv0[half_k:, :] + sv1[0:1, :])
        ce.append(si0[half_k:, :] * PEER_N_KEYS + si1[0:1, :])
        cc.append((sub_idx[0:half_k, :] + half_k) * PEER_TOPK)
        top_s, top_e = _take_top(jnp.concatenate(cs, axis=0), jnp.concatenate(cc, axis=0),
                                 jnp.concatenate(ce, axis=0), PEER_TOPK)
        p = jnp.exp(top_s - jnp.max(top_s, axis=0, keepdims=True))
        g_ref[h] = p / jnp.sum(p, axis=0, keepdims=True)
        e_ref[h] = top_e.astype(jnp.int32)


def _peer_topk(st, *, lanes):
    n = st.shape[2]
    out_spec = pl.BlockSpec((PEER_HEADS, PEER_TOPK, lanes), lambda i: (0, 0, i))
    return pl.pallas_call(
        _topk_kernel,
        grid=(n // lanes,),
        in_specs=[pl.BlockSpec((2 * PEER_HEADS, PEER_N_KEYS, lanes), lambda i: (0, 0, i))],
        out_specs=[out_spec, out_spec],
        out_shape=[jax.ShapeDtypeStruct((PEER_HEADS, PEER_TOPK, n), jnp.int32),
                   jax.ShapeDtypeStruct((PEER_HEADS, PEER_TOPK, n), F32)],
        compiler_params=_params("parallel"),
        cost_estimate=_cost(n * PEER_TOPK * 8 * (2 * PEER_HEADS * PEER_N_KEYS + PEER_HEADS * 10 * PEER_TOPK),
                            n * 4 * (2 * PEER_HEADS * PEER_N_KEYS + 2 * PEER_PAIRS)),
        name="peer_topk",
    )(st)


def _sc_peer(u_tab, v_tab, idx, x):
    n_tok = x.shape[0]
    per_worker = n_tok // SC_WORKERS
    n_win = PEER_PAIRS // DOT_WIN
    steps = PEER_PAIRS // SC_LANES
    per_win = DOT_WIN // SC_LANES
    mesh = plsc.VectorSubcoreMesh(core_axis_name="c", subcore_axis_name="s")

    @functools.partial(
        pl.kernel, mesh=mesh,
        out_type=(jax.ShapeDtypeStruct((n_tok * PEER_PAIRS,), F32),
                  jax.ShapeDtypeStruct((n_tok * PEER_PAIRS, HALF_W), jnp.int32)),
        scratch_types=[
            pltpu.VMEM((2, PEER_PAIRS), jnp.int32),
            pltpu.VMEM((2, PEER_PAIRS), jnp.int32),
            pltpu.VMEM((2, 2 * HALF_W), F32),
            pltpu.VMEM((n_win, DOT_WIN, HALF_W), jnp.int32),
            pltpu.VMEM((V_BUFS, SC_LANES, HALF_W), jnp.int32),
            pltpu.VMEM((2, PEER_PAIRS), F32),
            pltpu.SemaphoreType.DMA((n_win,)),
            pltpu.SemaphoreType.DMA((V_BUFS,)),
            pltpu.SemaphoreType.DMA((V_BUFS,)),
            pltpu.SemaphoreType.DMA((2,)),
            pltpu.SemaphoreType.DMA((2,)),
            pltpu.SemaphoreType.DMA((2,)),
        ],
        compiler_params=pltpu.CompilerParams(needs_layout_passes=False),
        cost_estimate=_cost(2 * n_tok * PEER_PAIRS * D_MODEL, n_tok * (PEER_PAIRS * (3 * 4 * HALF_W + 8) + 4 * D_MODEL)),
    )
    def peer(u_hbm, v_hbm, idx_hbm, x_hbm, act_hbm, vrows_hbm, idx_v, vidx_v, x_v, urows_v, vrows_v, act_v,
             u_sems, vg_sems, vw_sems, idx_sems, x_sems, act_sems):
        wid = lax.axis_index("s") * 2 + lax.axis_index("c")
        base = wid * per_worker
        last = base + per_worker - 1
        lane = lax.iota(jnp.int32, SC_LANES)

        def pair_off(tok):
            return pl.multiple_of(tok * PEER_PAIRS, PEER_PAIRS)

        def idx_copy(tok, slot):
            return pltpu.make_async_copy(idx_hbm.at[pl.ds(pair_off(tok), PEER_PAIRS)], idx_v.at[slot], idx_sems.at[slot])

        def x_copy(tok, slot):
            return pltpu.make_async_copy(x_hbm.at[tok], x_v.at[slot], x_sems.at[slot])

        def u_gather(slot, w):
            return pltpu.make_async_copy(u_hbm.at[idx_v.at[slot, pl.ds(w * DOT_WIN, DOT_WIN)]], urows_v.at[w],
                                         u_sems.at[w])

        def v_gather(slot, g):
            return pltpu.make_async_copy(v_hbm.at[vidx_v.at[slot, pl.ds(g * SC_LANES, SC_LANES)]],
                                         vrows_v.at[g % V_BUFS], vg_sems.at[g % V_BUFS])

        def v_write(tok, g):
            return pltpu.make_async_copy(vrows_v.at[g % V_BUFS],
                                         vrows_hbm.at[pl.ds(pl.multiple_of(pair_off(tok) + g * SC_LANES, SC_LANES), SC_LANES)],
                                         vw_sems.at[g % V_BUFS])

        def act_copy(tok, slot):
            return pltpu.make_async_copy(act_v.at[slot], act_hbm.at[pl.ds(pair_off(tok), PEER_PAIRS)], act_sems.at[slot])

        def step_dots(slot, g):
            w, sub = g // per_win, g % per_win

            def body(c, accs):
                xlo = x_v[slot, pl.ds(c * SC_LANES, SC_LANES)]
                xhi = x_v[slot, pl.ds(HALF_W + c * SC_LANES, SC_LANES)]
                new = []
                for r in range(SC_LANES):
                    wv = urows_v[w, sub * SC_LANES + r, pl.ds(c * SC_LANES, SC_LANES)]
                    lo = lax.bitcast_convert_type(wv << 16, F32)
                    hi = lax.bitcast_convert_type(wv & jnp.int32(-65536), F32)
                    new.append(accs[r] + lo * xlo + hi * xhi)
                return tuple(new)

            accs = lax.fori_loop(0, HALF_W // SC_LANES, body, tuple(jnp.zeros((SC_LANES,), F32) for _ in range(SC_LANES)))
            v = jnp.zeros((SC_LANES,), F32)
            for r in range(SC_LANES):
                v = jnp.where(lane == r, jnp.sum(accs[r]), v)
            act_v[slot, pl.ds(g * SC_LANES, SC_LANES)] = v

        def unless_first(i, slot, skip, fn):
            if slot == 0 and skip:
                pl.when(i > 0)(fn)
            else:
                fn()

        idx_copy(base, 0).start()
        x_copy(base, 0).start()
        idx_copy(base, 0).wait()
        for w in range(n_win):
            u_gather(0, w).start()

        @pl.loop(0, per_worker, step=2)
        def _(i):
            for slot in range(2):
                tok = base + i + slot
                nxt = jnp.minimum(tok + 1, last)
                other = 1 - slot
                idx_copy(nxt, other).start()
                x_copy(nxt, other).start()
                for q in range(PEER_PAIRS // SC_LANES):
                    vidx_v[slot, pl.ds(q * SC_LANES, SC_LANES)] = idx_v[slot, pl.ds(q * SC_LANES, SC_LANES)]
                x_copy(tok, slot).wait()

                @pl.when(i > 0)
                def _():
                    act_copy(tok, slot).wait()

                idx_copy(nxt, other).wait()
                for g in range(steps):
                    unless_first(i, slot, g < V_BUFS, lambda tok=tok, g=g: v_write(tok, g).wait())
                    v_gather(slot, g).start()
                    if g % per_win == 0:
                        u_gather(slot, g // per_win).wait()
                    step_dots(slot, g)
                    if g % per_win == per_win - 1:
                        u_gather(other, g // per_win).start()
                    gp = (g - 2) % steps
                    tokp = tok if g >= 2 else tok - 1

                    def finish(slot=slot, gp=gp, tokp=tokp):
                        v_gather(slot, gp).wait()
                        v_write(tokp, gp).start()

                    unless_first(i, slot, g < 2, finish)
                act_copy(tok, slot).start()

        for gp in (steps - 2, steps - 1):
            v_gather(1, gp).wait()
            v_write(last, gp).start()
        for b in range(V_BUFS):
            v_write(last, b).wait()
        for w in range(n_win):
            u_gather(0, w).wait()
        x_copy(last, 0).wait()
        for slot in range(2):
            act_copy(last, slot).wait()

    return peer(u_tab, v_tab, idx, x)


def _peer_kernel(vg_ref, act2_ref, w2_ref, h1_ref, gf_ref, l2g_ref, l2b_ref, o_ref, f_ref, *, tb):
    sub = 8
    span = sub * PEER_PAIRS
    pair_w = 2 * PEER_PAIRS
    half = D_MODEL // 2
    lane = lax.broadcasted_iota(jnp.int32, (sub, 2 * span), 1)
    diag = (lane // pair_w) == lax.broadcasted_iota(jnp.int32, (sub, 2 * span), 0)
    even = (lane % 2) == 0
    for s in range(tb // sub):
        rows = slice(s * span, (s + 1) * span)
        toks = slice(s * sub, (s + 1) * sub)
        coef = w2_ref[toks, :] * _gelu(act2_ref[toks, :])
        cdiag = jnp.where(diag, jnp.concatenate([coef] * sub, axis=1), 0.0)
        c2 = jnp.concatenate([jnp.where(even, cdiag, 0.0), jnp.where(even, 0.0, cdiag)], axis=0).astype(BF16)
        f2 = _dot(c2, pltpu.bitcast(vg_ref[rows, :], BF16))
        f_ref[toks, :half] = f2[:sub]
        f_ref[toks, half:] = f2[sub:]
    r = DEEPNORM_ALPHA * h1_ref[...] + gf_ref[0] * f_ref[...]
    o_ref[...] = _ln_plain(r) * l2g_ref[...] + l2b_ref[...]


def _peer_eval(vg, act2, w2, h1, g_f, l2g, l2b, *, tb, tok0, length):
    n = h1.shape[0]
    tok_blk0 = tok0 // tb
    per_b = length // tb
    tok_spec = lambda width: pl.BlockSpec((tb, width), lambda i: (i, 0))
    return pl.pallas_call(
        functools.partial(_peer_kernel, tb=tb),
        grid=(n // tb,),
        in_specs=[
            pl.BlockSpec((tb * PEER_PAIRS, D_MODEL // 2), lambda i: (i, 0)),
            tok_spec(2 * PEER_PAIRS), tok_spec(2 * PEER_PAIRS), tok_spec(D_MODEL),
            pl.BlockSpec((1, 1, D_MODEL), lambda i: ((tok_blk0 + i) // per_b, 0, 0)),
            pl.BlockSpec((1, D_MODEL), lambda i: (0, 0)),
            pl.BlockSpec((1, D_MODEL), lambda i: (0, 0)),
        ],
        out_specs=tok_spec(D_MODEL),
        out_shape=jax.ShapeDtypeStruct((n, D_MODEL), F32),
        scratch_shapes=[pltpu.VMEM((tb, D_MODEL), F32)],
        compiler_params=_params("parallel"),
        cost_estimate=_cost(2 * n * 2 * PEER_PAIRS * 8 * D_MODEL, n * (4 * PEER_PAIRS * HALF_W + 8 * D_MODEL + 16 * PEER_PAIRS),
                            n * 2 * PEER_PAIRS),
        name="peer_eval",
    )(vg, act2, w2, h1, g_f, l2g, l2b)


def _rope_perm():
    half = ROPE_AXIS_DIM // 2
    base = jnp.arange(QK_ROPE_DIM)
    return jnp.where((base % ROPE_AXIS_DIM) < half, base + half, base - half)


def _rope_tables(length):
    pos = jnp.arange(length)
    row = (pos // GRID_W).astype(F32)
    col = (pos % GRID_W).astype(F32)
    inv_freq = jnp.power(ROPE_BASE, -jnp.arange(0, ROPE_AXIS_DIM, 2, dtype=F32) / ROPE_AXIS_DIM)
    ang_r = row[:, None] * inv_freq
    ang_c = col[:, None] * inv_freq
    zeros = jnp.zeros((length, QK_ROPE_DIM), F32)
    c_tab = jnp.concatenate([jnp.cos(ang_r), jnp.cos(ang_r), jnp.cos(ang_c), jnp.cos(ang_c), zeros], axis=1)
    s_tab = jnp.concatenate([-jnp.sin(ang_r), jnp.sin(ang_r), -jnp.sin(ang_c), jnp.sin(ang_c), zeros], axis=1)
    return c_tab, s_tab


def _identity_tables(length):
    ones = jnp.ones((length, QK_ROPE_DIM), F32)
    zeros = jnp.zeros((length, QK_ROPE_DIM), F32)
    return jnp.concatenate([ones, zeros], axis=1), jnp.zeros((length, 2 * QK_ROPE_DIM), F32)


def _pack_rows(table):
    bits = lax.bitcast_convert_type(table.astype(BF16), jnp.uint16).astype(jnp.uint32)
    return lax.bitcast_convert_type(bits[:, :HALF_W] | (bits[:, HALF_W:] << 16), jnp.int32)


def _layer_weights(w_in, q_norm_g, kv_norm_g, w_uq, w_ukv):
    perm = _rope_perm()
    p0 = Q_RANK
    p1 = p0 + KV_RANK
    p2 = p1 + QK_ROPE_DIM
    p3 = p2 + S5_WIDTH
    p4 = p3 + D_MODEL
    w_kpe = w_in[:, p1:p2]
    wckv = jnp.concatenate([w_in[:, p0:p1], w_kpe, w_kpe[:, perm]], axis=1)
    uq = w_uq.reshape(Q_RANK, N_HEADS, QK_NOPE_DIM + QK_ROPE_DIM)
    pe = uq[:, :, QK_NOPE_DIM:]
    wuq = jnp.concatenate([uq[:, :, :QK_NOPE_DIM], pe, pe[:, :, perm]], axis=2).reshape(Q_RANK, N_HEADS * HEAD_PAD)
    cast = lambda a: a.astype(BF16)
    return (cast(w_in[:, :p0]), cast(wckv), cast(w_in[:, p2:p3]), cast(w_in[:, p3:p4]), cast(w_in[:, p4:]),
            q_norm_g.reshape(1, Q_RANK), kv_norm_g.reshape(1, KV_RANK), cast(wuq), cast(w_ukv))


def kernel(x, c, ctx, c_ctx, w_mod, b_mod, w_in, q_norm_g, kv_norm_g, w_uq, w_ukv, s5_a_re, s5_a_im, s5_log_dt, s5_b_re, s5_b_im, s5_c_re, s5_c_im, s5_d, w_glu, w_out, ln1_g, ln1_b, peer_wq, peer_keys, peer_u, peer_v, ln2_g, ln2_b):
    bsz, length, _ = x.shape
    l_ctx = ctx.shape[1]
    n_tok = bsz * length
    layer = 0
    tm = 256
    s5_tt = 64
    peer_tb = 32

    cond = jnp.concatenate([c, c_ctx[None, :], jnp.zeros((7, D_MODEL), F32)], axis=0)
    mod = _modulation(cond, w_mod[layer], b_mod[layer]).reshape(cond.shape[0], N_MOD, 1, D_MODEL)
    sh_a, sc_a, g_a, sh_f, sc_f, g_f = (mod[:bsz, i] for i in range(N_MOD))
    csh_a, csc_a = mod[bsz:bsz + 1, 0], mod[bsz:bsz + 1, 1]

    wts = _layer_weights(w_in[layer], q_norm_g[layer], kv_norm_g[layer], w_uq[layer], w_ukv[layer])
    c_lat, s_lat = _rope_tables(length)
    c_id, s_id = _identity_tables(l_ctx)
    keys = peer_keys[layer].reshape(2 * PEER_HEADS, PEER_N_KEYS, PEER_HALF).astype(BF16)
    wglu, wout, wq = w_glu[layer].astype(BF16), w_out[layer].astype(BF16), peer_wq[layer].astype(BF16)
    u_pack = _pack_rows(peer_u[layer])
    v_pack = _pack_rows(peer_v[layer])
    l2g = ln2_g[layer].reshape(1, D_MODEL)
    l2b = ln2_b[layer].reshape(1, D_MODEL)

    n_groups = PEER_GROUPS if bsz % PEER_GROUPS == 0 else 1
    halves = 2 if (n_groups % 2 == 0 and bsz % 16 == 0) else 1
    per_half = n_groups // halves
    hb = bsz // halves
    gb = bsz // n_groups
    g_tok = gb * length
    mats = _s5_matrices(s5_a_re[layer], s5_a_im[layer], s5_log_dt[layer], s5_b_re[layer], s5_b_im[layer],
                        s5_c_re[layer], s5_c_im[layer], hb)
    outs = []
    ready = (u_pack, v_pack)
    for g in range(n_groups):
        b0 = g * gb
        h0 = (g % per_half) * gb
        if g % per_half == 0:
            x_h, ctx_h, _ = lax.optimization_barrier((x, ctx, ready))
            k_ctx, v_ctx, su_ctx = _inproj(ctx_h, csh_a, csc_a, c_id, s_id, wts, is_ctx=True, tm=min(tm, l_ctx), b0=b0, bsz=hb)
            q, k_lat, v_lat, su_lat, gm, gs = _inproj(x_h, sh_a, sc_a, c_lat, s_lat, wts, is_ctx=False, tm=tm, b0=b0, bsz=hb)
            su_all = jnp.concatenate([su_ctx, su_lat], axis=0)
            y = _s5_scan(su_all.reshape(-1, S5_WIDTH), mats, bsz=hb, l_ctx=l_ctx, tt=s5_tt)
            y2d = y.reshape(2, length, hb * S5_WIDTH)
        q_g, _ = lax.optimization_barrier((q, ready))
        att = _attention(q_g, k_ctx, v_ctx, k_lat, v_lat, tq=min(512, length), b0=h0, bsz=gb)
        h1, xm, st = _merge(x, att, gm, gs, su_all, y2d, s5_d[layer].reshape(1, S5_WIDTH), wglu, wout, g_a,
                            ln1_g[layer].reshape(1, D_MODEL), ln1_b[layer].reshape(1, D_MODEL), sh_f, sc_f, wq, keys,
                            tm=tm, ctx_tiles=l_ctx // tm, b0=b0, h0=h0)
        experts, gates = _peer_topk(st, lanes=128)
        idx = experts.transpose(2, 0, 1).reshape(g_tok * PEER_PAIRS)
        w2 = jnp.repeat(gates.transpose(2, 0, 1).reshape(g_tok, PEER_PAIRS), 2, axis=1)
        xmf = xm.reshape(g_tok, D_MODEL)
        acts, vg = _sc_peer(u_pack, v_pack, idx, xmf)
        act2 = jnp.repeat(acts.reshape(g_tok, PEER_PAIRS), 2, axis=1)
        outs.append(_peer_eval(vg, act2, w2, h1.reshape(g_tok, D_MODEL), g_f, l2g, l2b,
                               tb=peer_tb, tok0=g * g_tok, length=length))
        ready = (idx, w2, xmf) + ((outs[g - 1],) if g >= 1 else ())
    return jnp.concatenate(outs, axis=0).reshape(bsz, length, D_MODEL)
```

```python
import functools
import math

import jax
import jax.numpy as jnp
from jax import lax
from jax.experimental import pallas as pl
from jax.experimental.pallas import tpu as pltpu
from jax.experimental.pallas import tpu_sc as plsc

F32 = jnp.float32
BF16 = jnp.bfloat16

D_MODEL = 1024
DEPTH = 1
GRID_W = 64
N_HEADS = 8
QK_NOPE_DIM = 128
QK_ROPE_DIM = 64
V_HEAD_DIM = 128
Q_RANK = 384
KV_RANK = 256
ROPE_AXIS_DIM = QK_ROPE_DIM // 2
ROPE_BASE = 10000.0
S5_WIDTH = D_MODEL // 2
S5_GROUP = 16
S5_GROUPS = S5_WIDTH // S5_GROUP
S5_STATE = 64
PEER_HEADS = 8
PEER_N_KEYS = 128
PEER_TOPK = 16
PEER_HALF = 128
PEER_PAIRS = PEER_HEADS * PEER_TOPK
DEEPNORM_ALPHA = (2.0 * DEPTH) ** 0.25
LN_EPS = 1e-6
N_MOD = 6
ATT_SCALE = (QK_NOPE_DIM + QK_ROPE_DIM) ** -0.5

HEAD_PAD = 256
S5_COLS = S5_GROUPS * S5_STATE
S5_CHUNKS = 2
SC_WORKERS = 32
SC_LANES = 16
DOT_WIN = 32
V_BUFS = 4
CODE_NONE = 1.0e9
PEER_GROUPS = 8
HALF_W = D_MODEL // 2
VMEM_LIMIT = 48 * 1024 * 1024


def _dot(a, b):
    return jnp.dot(a, b, preferred_element_type=F32)


def _dot_nt(a, b):
    return lax.dot_general(a, b, (((1,), (1,)), ((), ())), preferred_element_type=F32)


def _gelu(x):
    return 0.5 * x * (1.0 + jnp.tanh(0.7978845608028654 * (x + 0.044715 * (x * x * x))))


def _ln_plain(x):
    mu = jnp.mean(x, axis=-1, keepdims=True)
    xc = x - mu
    var = jnp.mean(xc * xc, axis=-1, keepdims=True)
    return xc * lax.rsqrt(var + LN_EPS)


def _rms(x, g):
    return x * lax.rsqrt(jnp.mean(x * x, axis=-1, keepdims=True) + LN_EPS) * g


def _params(*sem):
    return pltpu.CompilerParams(dimension_semantics=sem, vmem_limit_bytes=VMEM_LIMIT)


def _mod_kernel(cond_ref, w_ref, b_ref, o_ref):
    a = cond_ref[...]
    a = a * jax.nn.sigmoid(a)
    a_hi = a.astype(BF16)
    a_lo = (a - a_hi.astype(F32)).astype(BF16)
    w = w_ref[...]
    w_hi = w.astype(BF16)
    w_lo = (w - w_hi.astype(F32)).astype(BF16)
    o_ref[...] = _dot(a_hi, w_hi) + _dot(a_lo, w_hi) + _dot(a_hi, w_lo) + b_ref[...]


def _modulation(cond, w_mod, b_mod):
    rows = cond.shape[0]
    n = w_mod.shape[1]
    blk = D_MODEL
    return pl.pallas_call(
        _mod_kernel,
        grid=(n // blk,),
        in_specs=[
            pl.BlockSpec((rows, D_MODEL), lambda j: (0, 0)),
            pl.BlockSpec((D_MODEL, blk), lambda j: (0, j)),
            pl.BlockSpec((1, blk), lambda j: (0, j)),
        ],
        out_specs=pl.BlockSpec((rows, blk), lambda j: (0, j)),
        out_shape=jax.ShapeDtypeStruct((rows, n), F32),
        compiler_params=_params("arbitrary"),
        name="modulation",
    )(cond, w_mod, b_mod.reshape(1, n))


def _rope128(t, c_tab, s_tab):
    return t * c_tab + pltpu.roll(t, 64, axis=1) * s_tab


def _inproj_kernel(x_ref, sh_ref, sc_ref, c_ref, s_ref, wcq_ref, wckv_ref, wsu_ref, wgm_ref, wgs_ref,
                   qg_ref, kvg_ref, wuq_ref, wukv_ref, *out_refs, is_ctx):
    if is_ctx:
        k_ref, v_ref, su_ref = out_refs
    else:
        q_ref, k_ref, v_ref, su_ref, gm_ref, gs_ref = out_refs
    xm = _ln_plain(x_ref[0]) * (1.0 + sc_ref[0]) + sh_ref[0]
    xb = xm.astype(BF16)
    c_tab = c_ref[...]
    s_tab = s_ref[...]

    ckvpe = _dot(xb, wckv_ref[...])
    ckv_n = _rms(ckvpe[:, :KV_RANK], kvg_ref[...]).astype(BF16)
    kpe = _rope128(ckvpe[:, KV_RANK:], c_tab, s_tab).astype(BF16)
    kv = _dot(ckv_n, wukv_ref[...])
    for h in range(N_HEADS):
        k_ref[0, h, :, 0:128] = kv[:, h * 256:h * 256 + 128].astype(BF16)
        k_ref[0, h, :, 128:256] = kpe
        v_ref[0, h] = kv[:, h * 256 + 128:(h + 1) * 256].astype(BF16)

    su_ref[...] = _dot(xb, wsu_ref[...])

    if not is_ctx:
        cq_n = _rms(_dot(xb, wcq_ref[...]), qg_ref[...]).astype(BF16)
        q = _dot(cq_n, wuq_ref[...])
        for h in range(N_HEADS):
            q_ref[0, h, :, 0:128] = (q[:, h * 256:h * 256 + 128] * ATT_SCALE).astype(BF16)
            q_ref[0, h, :, 128:256] = (_rope128(q[:, h * 256 + 128:(h + 1) * 256], c_tab, s_tab) * ATT_SCALE).astype(BF16)
        gm_ref[0] = jax.nn.sigmoid(_dot(xb, wgm_ref[...])).astype(BF16)
        gs_ref[0] = jax.nn.sigmoid(_dot(xb, wgs_ref[...])).astype(BF16)


def _inproj(x, shift, scale, c_tab, s_tab, wts, *, is_ctx, tm, b0, bsz):
    length = x.shape[1]
    mod_map = (lambda b, t: (0, 0, 0)) if shift.shape[0] == 1 else (lambda b, t: (b0 + b, 0, 0))
    const2 = lambda b, t: (0, 0)
    wcq, wckv, wsu, wgm, wgs, qg, kvg, wuq, wukv = wts
    in_specs = [
        pl.BlockSpec((1, tm, D_MODEL), lambda b, t: (b0 + b, t, 0)),
        pl.BlockSpec((1, 1, D_MODEL), mod_map),
        pl.BlockSpec((1, 1, D_MODEL), mod_map),
        pl.BlockSpec((tm, 128), lambda b, t: (t, 0)),
        pl.BlockSpec((tm, 128), lambda b, t: (t, 0)),
    ] + [pl.BlockSpec(w.shape, const2) for w in (wcq, wckv, wsu, wgm, wgs, qg, kvg, wuq, wukv)]
    head_spec = lambda width: pl.BlockSpec((1, N_HEADS, tm, width), lambda b, t: (b, 0, t, 0))
    row_spec = pl.BlockSpec((1, tm, D_MODEL), lambda b, t: (b, t, 0))
    su_spec = pl.BlockSpec((tm, S5_WIDTH), lambda b, t: (t, b))
    k_shape = jax.ShapeDtypeStruct((bsz, N_HEADS, length, HEAD_PAD), BF16)
    v_shape = jax.ShapeDtypeStruct((bsz, N_HEADS, length, V_HEAD_DIM), BF16)
    su_shape = jax.ShapeDtypeStruct((length, bsz * S5_WIDTH), F32)
    g_shape = jax.ShapeDtypeStruct((bsz, length, D_MODEL), BF16)
    if is_ctx:
        out_specs = [head_spec(HEAD_PAD), head_spec(V_HEAD_DIM), su_spec]
        out_shape = [k_shape, v_shape, su_shape]
    else:
        out_specs = [head_spec(HEAD_PAD), head_spec(HEAD_PAD), head_spec(V_HEAD_DIM), su_spec, row_spec, row_spec]
        out_shape = [k_shape, k_shape, v_shape, su_shape, g_shape, g_shape]
    return pl.pallas_call(
        functools.partial(_inproj_kernel, is_ctx=is_ctx),
        grid=(bsz, length // tm),
        in_specs=in_specs,
        out_specs=out_specs,
        out_shape=out_shape,
        compiler_params=_params("parallel", "parallel"),
        name="inproj_ctx" if is_ctx else "inproj_lat",
    )(x, shift, scale, c_tab, s_tab, wcq, wckv, wsu, wgm, wgs, qg, kvg, wuq, wukv)


def _attn_kernel(q_ref, kc_ref, vc_ref, kl_ref, vl_ref, o_ref):
    q = q_ref[0, 0]
    s_c = _dot_nt(q, kc_ref[0, 0])
    s_l = _dot_nt(q, kl_ref[0, 0])
    m = jnp.maximum(jnp.max(s_c, axis=-1, keepdims=True), jnp.max(s_l, axis=-1, keepdims=True))
    p_c = jnp.exp(s_c - m)
    p_l = jnp.exp(s_l - m)
    denom = jnp.sum(p_c, axis=-1, keepdims=True) + jnp.sum(p_l, axis=-1, keepdims=True)
    o = _dot(p_c.astype(BF16), vc_ref[0, 0]) + _dot(p_l.astype(BF16), vl_ref[0, 0])
    o_ref[0] = (o / denom).astype(BF16)


def _attention(q, k_ctx, v_ctx, k_lat, v_lat, *, tq, b0, bsz):
    _, heads, length, _ = q.shape
    l_ctx = k_ctx.shape[2]
    full = lambda b, h, i: (b0 + b, h, 0, 0)
    return pl.pallas_call(
        _attn_kernel,
        grid=(bsz, heads, length // tq),
        in_specs=[
            pl.BlockSpec((1, 1, tq, HEAD_PAD), lambda b, h, i: (b0 + b, h, i, 0)),
            pl.BlockSpec((1, 1, l_ctx, HEAD_PAD), full),
            pl.BlockSpec((1, 1, l_ctx, V_HEAD_DIM), full),
            pl.BlockSpec((1, 1, length, HEAD_PAD), full),
            pl.BlockSpec((1, 1, length, V_HEAD_DIM), full),
        ],
        out_specs=pl.BlockSpec((1, tq, V_HEAD_DIM), lambda b, h, i: (b, i, h)),
        out_shape=jax.ShapeDtypeStruct((bsz, length, heads * V_HEAD_DIM), BF16),
        compiler_params=_params("parallel", "parallel", "arbitrary"),
        name="attention",
    )(q, k_ctx, v_ctx, k_lat, v_lat)


def _s5_kernel(u_ref, are_ref, aim_ref, bre_ref, bim_ref, cre_ref, cim_ref, y_ref,
               bu_re, bu_im, h_re, h_im, *, tt, bsz, n_ctx_tiles):
    d = pl.program_id(0)
    i = pl.program_id(1)

    @pl.when(i == 0)
    def _():
        h_re[...] = jnp.zeros_like(h_re)
        h_im[...] = jnp.zeros_like(h_im)

    u = u_ref[...].astype(BF16)
    half = S5_COLS // S5_CHUNKS
    cw = S5_WIDTH // S5_CHUNKS
    for c in range(S5_CHUNKS):
        uc = u[:, c * cw:(c + 1) * cw]
        bu_re[:, c * half:(c + 1) * half] = _dot(uc, bre_ref[0, c])
        bu_im[:, c * half:(c + 1) * half] = _dot(uc, bim_ref[0, c])

    col_w = 512
    for cc in range(S5_COLS // col_w):
        cols = slice(cc * col_w, (cc + 1) * col_w)
        a_r = are_ref[0, :, cols]
        a_i = aim_ref[0, :, cols]

        def step(t, carry, cols=cols, a_r=a_r, a_i=a_i):
            hr, hi = carry
            pos = jnp.where(d == 0, t, tt - 1 - t)
            r = pl.multiple_of(pos * bsz, bsz)
            nr = a_r * hr - a_i * hi + bu_re[pl.ds(r, bsz), cols]
            ni = a_r * hi + a_i * hr + bu_im[pl.ds(r, bsz), cols]
            bu_re[pl.ds(r, bsz), cols] = nr
            bu_im[pl.ds(r, bsz), cols] = ni
            return nr, ni

        hr, hi = lax.fori_loop(0, tt, step, (h_re[:, cols], h_im[:, cols]))
        h_re[:, cols] = hr
        h_im[:, cols] = hi

    @pl.when(i >= n_ctx_tiles)
    def _():
        for c in range(S5_CHUNKS):
            sl = slice(c * half, (c + 1) * half)
            y = _dot(bu_re[:, sl].astype(BF16), cre_ref[0, c]) + _dot(bu_im[:, sl].astype(BF16), cim_ref[0, c])
            y_ref[0, :, c * cw:(c + 1) * cw] = y


def _s5_scan(u_all, mats, *, bsz, l_ctx, tt):
    a_re_b, a_im_b, b_re_bd, b_im_bd, c_re_bd, c_im_bd = mats
    rows = tt * bsz
    n_tiles = u_all.shape[0] // rows
    n_ctx_tiles = l_ctx // tt
    n_lat_tiles = n_tiles - n_ctx_tiles

    def u_map(d, i):
        bwd = jnp.where(i < n_ctx_tiles, n_ctx_tiles - 1 - i, n_tiles - 1 - (i - n_ctx_tiles))
        return (jnp.where(d == 0, i, bwd), 0)

    def y_map(d, i):
        j = jnp.maximum(i - n_ctx_tiles, 0)
        return (d, jnp.where(d == 0, j, n_lat_tiles - 1 - j), 0)

    dmap3 = lambda d, i: (d, 0, 0)
    dmap4 = lambda d, i: (d, 0, 0, 0)
    return pl.pallas_call(
        functools.partial(_s5_kernel, tt=tt, bsz=bsz, n_ctx_tiles=n_ctx_tiles),
        grid=(2, n_tiles),
        in_specs=[
            pl.BlockSpec((rows, S5_WIDTH), u_map),
            pl.BlockSpec((1,) + a_re_b.shape[1:], dmap3),
            pl.BlockSpec((1,) + a_im_b.shape[1:], dmap3),
            pl.BlockSpec((1,) + b_re_bd.shape[1:], dmap4),
            pl.BlockSpec((1,) + b_im_bd.shape[1:], dmap4),
            pl.BlockSpec((1,) + c_re_bd.shape[1:], dmap4),
            pl.BlockSpec((1,) + c_im_bd.shape[1:], dmap4),
        ],
        out_specs=pl.BlockSpec((1, rows, S5_WIDTH), y_map),
        out_shape=jax.ShapeDtypeStruct((2, n_lat_tiles * rows, S5_WIDTH), F32),
        scratch_shapes=[
            pltpu.VMEM((rows, S5_COLS), F32),
            pltpu.VMEM((rows, S5_COLS), F32),
            pltpu.VMEM((bsz, S5_COLS), F32),
            pltpu.VMEM((bsz, S5_COLS), F32),
        ],
        compiler_params=_params("arbitrary", "arbitrary"),
        name="s5_scan",
    )(u_all, a_re_b, a_im_b, b_re_bd, b_im_bd, c_re_bd, c_im_bd)


def _s5_matrices(a_re, a_im, log_dt, b_re, b_im, c_re, c_im, bsz):
    dt = jnp.exp(log_dt)[..., None]
    mag = jnp.exp(a_re * dt)
    ab_re = mag * jnp.cos(a_im * dt)
    ab_im = mag * jnp.sin(a_im * dt)
    den = a_re * a_re + a_im * a_im
    f_re = ((ab_re - 1.0) * a_re + ab_im * a_im) / den
    f_im = (ab_im * a_re - (ab_re - 1.0) * a_im) / den
    bb_re = f_re[..., None] * b_re - f_im[..., None] * b_im
    bb_im = f_re[..., None] * b_im + f_im[..., None] * b_re
    gl = S5_GROUPS // S5_CHUNKS
    eye = jnp.eye(gl, dtype=F32)

    def in_bd(bb):
        t = bb.reshape(2, S5_CHUNKS, gl, S5_STATE, S5_GROUP)
        t = jnp.einsum('dkgpc,gh->dkgchp', t, eye)
        return t.reshape(2, S5_CHUNKS, gl * S5_GROUP, gl * S5_STATE).astype(BF16)

    def out_bd(cc):
        t = cc.reshape(2, S5_CHUNKS, gl, S5_GROUP, S5_STATE)
        t = jnp.einsum('dkgcp,gh->dkgphc', t, eye)
        return t.reshape(2, S5_CHUNKS, gl * S5_STATE, gl * S5_GROUP).astype(BF16)

    bcast = lambda a: jnp.broadcast_to(a.reshape(2, 1, S5_COLS), (2, bsz, S5_COLS))
    return bcast(ab_re), bcast(ab_im), in_bd(bb_re), in_bd(bb_im), out_bd(c_re), out_bd(-c_im)


def _merge_kernel(x_ref, att_ref, gm_ref, gs_ref, su_ref, y_ref, dsk_ref, wglu_ref, wout_ref, ga_ref,
                  l1g_ref, l1b_ref, shf_ref, scf_ref, wq_ref, keys_ref, h1_ref, xm_ref, st_ref):
    y = su_ref[...] * dsk_ref[...] + y_ref[0] + y_ref[1]
    gl = _dot(_gelu(y).astype(BF16), wglu_ref[...])
    s5_out = gl[:, :D_MODEL] * jax.nn.sigmoid(gl[:, D_MODEL:])
    merged = gm_ref[0].astype(F32) * att_ref[0].astype(F32) + gs_ref[0].astype(F32) * s5_out
    out = _dot(merged.astype(BF16), wout_ref[...])
    h1 = _ln_plain(DEEPNORM_ALPHA * x_ref[0] + ga_ref[0] * out) * l1g_ref[...] + l1b_ref[...]
    h1_ref[0] = h1
    xm = _ln_plain(h1) * (1.0 + scf_ref[0]) + shf_ref[0]
    xm_ref[0] = xm
    qp = _dot(xm.astype(BF16), wq_ref[...]).astype(BF16)
    for j in range(2 * PEER_HEADS):
        st_ref[j] = _dot_nt(keys_ref[j], qp[:, j * PEER_HALF:(j + 1) * PEER_HALF])


def _merge(x, att, gm, gs, su_all2d, y2d, dsk, wglu, wout, g_a, l1g, l1b, sh_f, sc_f, wq, keys, *, tm, ctx_tiles, b0, h0):
    bsz, length, _ = att.shape
    nt = length // tm
    row = pl.BlockSpec((1, tm, D_MODEL), lambda b, t: (b, t, 0))
    xrow = pl.BlockSpec((1, tm, D_MODEL), lambda b, t: (b0 + b, t, 0))
    hrow = pl.BlockSpec((1, tm, D_MODEL), lambda b, t: (h0 + b, t, 0))
    modb = pl.BlockSpec((1, 1, D_MODEL), lambda b, t: (b0 + b, 0, 0))
    const = lambda a: pl.BlockSpec(a.shape, lambda b, t: (0,) * a.ndim)
    return pl.pallas_call(
        _merge_kernel,
        grid=(bsz, nt),
        in_specs=[
            xrow, row, hrow, hrow,
            pl.BlockSpec((tm, S5_WIDTH), lambda b, t: (ctx_tiles + t, h0 + b)),
            pl.BlockSpec((2, tm, S5_WIDTH), lambda b, t: (0, t, h0 + b)),
            const(dsk), const(wglu), const(wout), modb, const(l1g), const(l1b), modb, modb, const(wq), const(keys),
        ],
        out_specs=[row, row, pl.BlockSpec((2 * PEER_HEADS, PEER_N_KEYS, tm), lambda b, t: (0, 0, b * nt + t))],
        out_shape=[
            jax.ShapeDtypeStruct((bsz, length, D_MODEL), F32),
            jax.ShapeDtypeStruct((bsz, length, D_MODEL), F32),
            jax.ShapeDtypeStruct((2 * PEER_HEADS, PEER_N_KEYS, bsz * length), F32),
        ],
        compiler_params=_params("parallel", "parallel"),
        name="merge_peer_query",
    )(x, att, gm, gs, su_all2d, y2d, dsk, wglu, wout, g_a, l1g, l1b, sh_f, sc_f, wq, keys)


def _take_top(vals, codes, payload, k):
    rows = lax.broadcasted_iota(jnp.int32, (k, vals.shape[1]), 0)
    top_v = jnp.zeros((k, vals.shape[1]), F32)
    top_p = jnp.zeros((k, vals.shape[1]), F32)
    for r in range(k):
        m = jnp.max(vals, axis=0, keepdims=True)
        cm = jnp.min(jnp.where(vals == m, codes, CODE_NONE), axis=0, keepdims=True)
        sel = codes == cm
        if payload is None:
            p = cm
        else:
            p = jnp.max(jnp.where(sel, payload, -1.0), axis=0, keepdims=True)
        vals = jnp.where(sel, -jnp.inf, vals)
        top_v = jnp.where(rows == r, m, top_v)
        top_p = jnp.where(rows == r, p, top_p)
    return top_v, top_p


def _topk_kernel(st_ref, e_ref, g_ref):
    lanes = st_ref.shape[2]
    key_idx = lax.broadcasted_iota(jnp.int32, (PEER_N_KEYS, lanes), 0).astype(F32)
    sub_idx = lax.broadcasted_iota(jnp.int32, (PEER_TOPK, lanes), 0).astype(F32)
    half_k = PEER_TOPK // 2
    for h in range(PEER_HEADS):
        sv0, si0 = _take_top(st_ref[2 * h], key_idx, None, PEER_TOPK)
        sv1, si1 = _take_top(st_ref[2 * h + 1], key_idx, None, PEER_TOPK)
        cs, ce, cc = [], [], []
        for i in range(half_k):
            nj = PEER_TOPK if i == 0 else half_k
            cs.append(sv0[i:i + 1, :] + sv1[0:nj, :])
            ce.append(si0[i:i + 1, :] * PEER_N_KEYS + si1[0:nj, :])
            cc.append(sub_idx[0:nj, :] + i * PEER_TOPK)
        cs.append(sv0[half_k:, :] + sv1[0:1, :])
        ce.append(si0[half_k:, :] * PEER_N_KEYS + si1[0:1, :])
        cc.append((sub_idx[0:half_k, :] + half_k) * PEER_TOPK)
        top_s, top_e = _take_top(jnp.concatenate(cs, axis=0), jnp.concatenate(cc, axis=0),
                                 jnp.concatenate(ce, axis=0), PEER_TOPK)
        p = jnp.exp(top_s - jnp.max(top_s, axis=0, keepdims=True))
        g_ref[h] = p / jnp.sum(p, axis=0, keepdims=True)
        e_ref[h] = top_e.astype(jnp.int32)


def _peer_topk(st, *, lanes):
    n = st.shape[2]
    out_spec = pl.BlockSpec((PEER_HEADS, PEER_TOPK, lanes), lambda i: (0, 0, i))
    return pl.pallas_call(
        _topk_kernel,
        grid=(n // lanes,),
        in_specs=[pl.BlockSpec((2 * PEER_HEADS, PEER_N_KEYS, lanes), lambda i: (0, 0, i))],
        out_specs=[out_spec, out_spec],
        out_shape=[jax.ShapeDtypeStruct((PEER_HEADS, PEER_TOPK, n), jnp.int32),
                   jax.ShapeDtypeStruct((PEER_HEADS, PEER_TOPK, n), F32)],
        compiler_params=_params("parallel"),
        name="peer_topk",
    )(st)


def _sc_peer(u_tab, v_tab, idx, x):
    n_tok = x.shape[0]
    per_worker = n_tok // SC_WORKERS
    n_win = PEER_PAIRS // DOT_WIN
    steps = PEER_PAIRS // SC_LANES
    per_win = DOT_WIN // SC_LANES
    mesh = plsc.VectorSubcoreMesh(core_axis_name="c", subcore_axis_name="s")

    @functools.partial(
        pl.kernel, mesh=mesh,
        out_type=(jax.ShapeDtypeStruct((n_tok * PEER_PAIRS,), F32),
                  jax.ShapeDtypeStruct((n_tok * PEER_PAIRS, HALF_W), jnp.int32)),
        scratch_types=[
            pltpu.VMEM((2, PEER_PAIRS), jnp.int32),
            pltpu.VMEM((2, PEER_PAIRS), jnp.int32),
            pltpu.VMEM((2, 2 * HALF_W), F32),
            pltpu.VMEM((n_win, DOT_WIN, HALF_W), jnp.int32),
            pltpu.VMEM((V_BUFS, SC_LANES, HALF_W), jnp.int32),
            pltpu.VMEM((2, PEER_PAIRS), F32),
            pltpu.SemaphoreType.DMA((n_win,)),
            pltpu.SemaphoreType.DMA((V_BUFS,)),
            pltpu.SemaphoreType.DMA((V_BUFS,)),
            pltpu.SemaphoreType.DMA((2,)),
            pltpu.SemaphoreType.DMA((2,)),
            pltpu.SemaphoreType.DMA((2,)),
        ],
        compiler_params=pltpu.CompilerParams(needs_layout_passes=False),
    )
    def peer(u_hbm, v_hbm, idx_hbm, x_hbm, act_hbm, vrows_hbm, idx_v, vidx_v, x_v, urows_v, vrows_v, act_v,
             u_sems, vg_sems, vw_sems, idx_sems, x_sems, act_sems):
        wid = lax.axis_index("s") * 2 + lax.axis_index("c")
        base = wid * per_worker
        last = base + per_worker - 1
        lane = lax.iota(jnp.int32, SC_LANES)

        def pair_off(tok):
            return pl.multiple_of(tok * PEER_PAIRS, PEER_PAIRS)

        def idx_copy(tok, slot):
            return pltpu.make_async_copy(idx_hbm.at[pl.ds(pair_off(tok), PEER_PAIRS)], idx_v.at[slot], idx_sems.at[slot])

        def x_copy(tok, slot):
            return pltpu.make_async_copy(x_hbm.at[tok], x_v.at[slot], x_sems.at[slot])

        def u_gather(slot, w):
            return pltpu.make_async_copy(u_hbm.at[idx_v.at[slot, pl.ds(w * DOT_WIN, DOT_WIN)]], urows_v.at[w],
                                         u_sems.at[w])

        def v_gather(slot, g):
            return pltpu.make_async_copy(v_hbm.at[vidx_v.at[slot, pl.ds(g * SC_LANES, SC_LANES)]],
                                         vrows_v.at[g % V_BUFS], vg_sems.at[g % V_BUFS])

        def v_write(tok, g):
            return pltpu.make_async_copy(vrows_v.at[g % V_BUFS],
                                         vrows_hbm.at[pl.ds(pl.multiple_of(pair_off(tok) + g * SC_LANES, SC_LANES), SC_LANES)],
                                         vw_sems.at[g % V_BUFS])

        def act_copy(tok, slot):
            return pltpu.make_async_copy(act_v.at[slot], act_hbm.at[pl.ds(pair_off(tok), PEER_PAIRS)], act_sems.at[slot])

        def step_dots(slot, g):
            w, sub = g // per_win, g % per_win

            def body(c, accs):
                xlo = x_v[slot, pl.ds(c * SC_LANES, SC_LANES)]
                xhi = x_v[slot, pl.ds(HALF_W + c * SC_LANES, SC_LANES)]
                new = []
                for r in range(SC_LANES):
                    wv = urows_v[w, sub * SC_LANES + r, pl.ds(c * SC_LANES, SC_LANES)]
                    lo = lax.bitcast_convert_type(wv << 16, F32)
                    hi = lax.bitcast_convert_type(wv & jnp.int32(-65536), F32)
                    new.append(accs[r] + lo * xlo + hi * xhi)
                return tuple(new)

            accs = lax.fori_loop(0, HALF_W // SC_LANES, body, tuple(jnp.zeros((SC_LANES,), F32) for _ in range(SC_LANES)))
            v = jnp.zeros((SC_LANES,), F32)
            for r in range(SC_LANES):
                v = jnp.where(lane == r, jnp.sum(accs[r]), v)
            act_v[slot, pl.ds(g * SC_LANES, SC_LANES)] = v

        def unless_first(i, slot, skip, fn):
            if slot == 0 and skip:
                pl.when(i > 0)(fn)
            else:
                fn()

        idx_copy(base, 0).start()
        x_copy(base, 0).start()
        idx_copy(base, 0).wait()
        for w in range(n_win):
            u_gather(0, w).start()

        @pl.loop(0, per_worker, step=2)
        def _(i):
            for slot in range(2):
                tok = base + i + slot
                nxt = jnp.minimum(tok + 1, last)
                other = 1 - slot
                idx_copy(nxt, other).start()
                x_copy(nxt, other).start()
                for q in range(PEER_PAIRS // SC_LANES):
                    vidx_v[slot, pl.ds(q * SC_LANES, SC_LANES)] = idx_v[slot, pl.ds(q * SC_LANES, SC_LANES)]
                x_copy(tok, slot).wait()

                @pl.when(i > 0)
                def _():
                    act_copy(tok, slot).wait()

                idx_copy(nxt, other).wait()
                for g in range(steps):
                    unless_first(i, slot, g < V_BUFS, lambda tok=tok, g=g: v_write(tok, g).wait())
                    v_gather(slot, g).start()
                    if g % per_win == 0:
                        u_gather(slot, g // per_win).wait()
                    step_dots(slot, g)
                    if g % per_win == per_win - 1:
                        u_gather(other, g // per_win).start()
                    gp = (g - 2) % steps
                    tokp = tok if g >= 2 else tok - 1

                    def finish(slot=slot, gp=gp, tokp=tokp):
                        v_gather(slot, gp).wait()
                        v_write(tokp, gp).start()

                    unless_first(i, slot, g < 2, finish)
                act_copy(tok, slot).start()

        for gp in (steps - 2, steps - 1):
            v_gather(1, gp).wait()
            v_write(last, gp).start()
        for b in range(V_BUFS):
            v_write(last, b).wait()
        for w in range(n_win):
            u_gather(0, w).wait()
        x_copy(last, 0).wait()
        for slot in range(2):
            act_copy(last, slot).wait()

    return peer(u_tab, v_tab, idx, x)


def _peer_kernel(vg_ref, act2_ref, w2_ref, h1_ref, gf_ref, l2g_ref, l2b_ref, o_ref, f_ref, *, tb):
    sub = 8
    span = sub * PEER_PAIRS
    pair_w = 2 * PEER_PAIRS
    half = D_MODEL // 2
    lane = lax.broadcasted_iota(jnp.int32, (sub, 2 * span), 1)
    diag = (lane // pair_w) == lax.broadcasted_iota(jnp.int32, (sub, 2 * span), 0)
    even = (lane % 2) == 0
    for s in range(tb // sub):
        rows = slice(s * span, (s + 1) * span)
        toks = slice(s * sub, (s + 1) * sub)
        coef = w2_ref[toks, :] * _gelu(act2_ref[toks, :])
        cdiag = jnp.where(diag, jnp.concatenate([coef] * sub, axis=1), 0.0)
        c2 = jnp.concatenate([jnp.where(even, cdiag, 0.0), jnp.where(even, 0.0, cdiag)], axis=0).astype(BF16)
        f2 = _dot(c2, pltpu.bitcast(vg_ref[rows, :], BF16))
        f_ref[toks, :half] = f2[:sub]
        f_ref[toks, half:] = f2[sub:]
    r = DEEPNORM_ALPHA * h1_ref[...] + gf_ref[0] * f_ref[...]
    o_ref[...] = _ln_plain(r) * l2g_ref[...] + l2b_ref[...]


def _peer_eval(vg, act2, w2, h1, g_f, l2g, l2b, *, tb, tok0, length):
    n = h1.shape[0]
    tok_blk0 = tok0 // tb
    per_b = length // tb
    tok_spec = lambda width: pl.BlockSpec((tb, width), lambda i: (i, 0))
    return pl.pallas_call(
        functools.partial(_peer_kernel, tb=tb),
        grid=(n // tb,),
        in_specs=[
            pl.BlockSpec((tb * PEER_PAIRS, D_MODEL // 2), lambda i: (i, 0)),
            tok_spec(2 * PEER_PAIRS), tok_spec(2 * PEER_PAIRS), tok_spec(D_MODEL),
            pl.BlockSpec((1, 1, D_MODEL), lambda i: ((tok_blk0 + i) // per_b, 0, 0)),
            pl.BlockSpec((1, D_MODEL), lambda i: (0, 0)),
            pl.BlockSpec((1, D_MODEL), lambda i: (0, 0)),
        ],
        out_specs=tok_spec(D_MODEL),
        out_shape=jax.ShapeDtypeStruct((n, D_MODEL), F32),
        scratch_shapes=[pltpu.VMEM((tb, D_MODEL), F32)],
        compiler_params=_params("parallel"),
        name="peer_eval",
    )(vg, act2, w2, h1, g_f, l2g, l2b)


def _rope_perm():
    half = ROPE_AXIS_DIM // 2
    base = jnp.arange(QK_ROPE_DIM)
    return jnp.where((base % ROPE_AXIS_DIM) < half, base + half, base - half)


def _rope_tables(length):
    pos = jnp.arange(length)
    row = (pos // GRID_W).astype(F32)
    col = (pos % GRID_W).astype(F32)
    inv_freq = jnp.power(ROPE_BASE, -jnp.arange(0, ROPE_AXIS_DIM, 2, dtype=F32) / ROPE_AXIS_DIM)
    ang_r = row[:, None] * inv_freq
    ang_c = col[:, None] * inv_freq
    zeros = jnp.zeros((length, QK_ROPE_DIM), F32)
    c_tab = jnp.concatenate([jnp.cos(ang_r), jnp.cos(ang_r), jnp.cos(ang_c), jnp.cos(ang_c), zeros], axis=1)
    s_tab = jnp.concatenate([-jnp.sin(ang_r), jnp.sin(ang_r), -jnp.sin(ang_c), jnp.sin(ang_c), zeros], axis=1)
    return c_tab, s_tab


def _identity_tables(length):
    ones = jnp.ones((length, QK_ROPE_DIM), F32)
    zeros = jnp.zeros((length, QK_ROPE_DIM), F32)
    return jnp.concatenate([ones, zeros], axis=1), jnp.zeros((length, 2 * QK_ROPE_DIM), F32)


def _pack_rows(table):
    bits = lax.bitcast_convert_type(table.astype(BF16), jnp.uint16).astype(jnp.uint32)
    return lax.bitcast_convert_type(bits[:, :HALF_W] | (bits[:, HALF_W:] << 16), jnp.int32)


def _layer_weights(w_in, q_norm_g, kv_norm_g, w_uq, w_ukv):
    perm = _rope_perm()
    p0 = Q_RANK
    p1 = p0 + KV_RANK
    p2 = p1 + QK_ROPE_DIM
    p3 = p2 + S5_WIDTH
    p4 = p3 + D_MODEL
    w_kpe = w_in[:, p1:p2]
    wckv = jnp.concatenate([w_in[:, p0:p1], w_kpe, w_kpe[:, perm]], axis=1)
    uq = w_uq.reshape(Q_RANK, N_HEADS, QK_NOPE_DIM + QK_ROPE_DIM)
    pe = uq[:, :, QK_NOPE_DIM:]
    wuq = jnp.concatenate([uq[:, :, :QK_NOPE_DIM], pe, pe[:, :, perm]], axis=2).reshape(Q_RANK, N_HEADS * HEAD_PAD)
    cast = lambda a: a.astype(BF16)
    return (cast(w_in[:, :p0]), cast(wckv), cast(w_in[:, p2:p3]), cast(w_in[:, p3:p4]), cast(w_in[:, p4:]),
            q_norm_g.reshape(1, Q_RANK), kv_norm_g.reshape(1, KV_RANK), cast(wuq), cast(w_ukv))


def kernel(x, c, ctx, c_ctx, w_mod, b_mod, w_in, q_norm_g, kv_norm_g, w_uq, w_ukv, s5_a_re, s5_a_im, s5_log_dt, s5_b_re, s5_b_im, s5_c_re, s5_c_im, s5_d, w_glu, w_out, ln1_g, ln1_b, peer_wq, peer_keys, peer_u, peer_v, ln2_g, ln2_b):
    bsz, length, _ = x.shape
    l_ctx = ctx.shape[1]
    n_tok = bsz * length
    layer = 0
    tm = 256
    s5_tt = 64
    peer_tb = 32

    cond = jnp.concatenate([c, c_ctx[None, :], jnp.zeros((7, D_MODEL), F32)], axis=0)
    mod = _modulation(cond, w_mod[layer], b_mod[layer]).reshape(cond.shape[0], N_MOD, 1, D_MODEL)
    sh_a, sc_a, g_a, sh_f, sc_f, g_f = (mod[:bsz, i] for i in range(N_MOD))
    csh_a, csc_a = mod[bsz:bsz + 1, 0], mod[bsz:bsz + 1, 1]

    wts = _layer_weights(w_in[layer], q_norm_g[layer], kv_norm_g[layer], w_uq[layer], w_ukv[layer])
    c_lat, s_lat = _rope_tables(length)
    c_id, s_id = _identity_tables(l_ctx)
    keys = peer_keys[layer].reshape(2 * PEER_HEADS, PEER_N_KEYS, PEER_HALF).astype(BF16)
    wglu, wout, wq = w_glu[layer].astype(BF16), w_out[layer].astype(BF16), peer_wq[layer].astype(BF16)
    u_pack = _pack_rows(peer_u[layer])
    v_pack = _pack_rows(peer_v[layer])
    l2g = ln2_g[layer].reshape(1, D_MODEL)
    l2b = ln2_b[layer].reshape(1, D_MODEL)

    n_groups = PEER_GROUPS if bsz % PEER_GROUPS == 0 else 1
    halves = 2 if (n_groups % 2 == 0 and bsz % 16 == 0) else 1
    per_half = n_groups // halves
    hb = bsz // halves
    gb = bsz // n_groups
    g_tok = gb * length
    mats = _s5_matrices(s5_a_re[layer], s5_a_im[layer], s5_log_dt[layer], s5_b_re[layer], s5_b_im[layer],
                        s5_c_re[layer], s5_c_im[layer], hb)
    outs = []
    ready = (u_pack, v_pack)
    for g in range(n_groups):
        b0 = g * gb
        h0 = (g % per_half) * gb
        if g % per_half == 0:
            x_h, ctx_h, _ = lax.optimization_barrier((x, ctx, ready))
            k_ctx, v_ctx, su_ctx = _inproj(ctx_h, csh_a, csc_a, c_id, s_id, wts, is_ctx=True, tm=min(tm, l_ctx), b0=b0, bsz=hb)
            q, k_lat, v_lat, su_lat, gm, gs = _inproj(x_h, sh_a, sc_a, c_lat, s_lat, wts, is_ctx=False, tm=tm, b0=b0, bsz=hb)
            su_all = jnp.concatenate([su_ctx, su_lat], axis=0)
            y = _s5_scan(su_all.reshape(-1, S5_WIDTH), mats, bsz=hb, l_ctx=l_ctx, tt=s5_tt)
            y2d = y.reshape(2, length, hb * S5_WIDTH)
        q_g, _ = lax.optimization_barrier((q, ready))
        att = _attention(q_g, k_ctx, v_ctx, k_lat, v_lat, tq=min(512, length), b0=h0, bsz=gb)
        h1, xm, st = _merge(x, att, gm, gs, su_all, y2d, s5_d[layer].reshape(1, S5_WIDTH), wglu, wout, g_a,
                            ln1_g[layer].reshape(1, D_MODEL), ln1_b[layer].reshape(1, D_MODEL), sh_f, sc_f, wq, keys,
                            tm=tm, ctx_tiles=l_ctx // tm, b0=b0, h0=h0)
        experts, gates = _peer_topk(st, lanes=128)
        idx = experts.transpose(2, 0, 1).reshape(g_tok * PEER_PAIRS)
        w2 = jnp.repeat(gates.transpose(2, 0, 1).reshape(g_tok, PEER_PAIRS), 2, axis=1)
        xmf = xm.reshape(g_tok, D_MODEL)
        acts, vg = _sc_peer(u_pack, v_pack, idx, xmf)
        act2 = jnp.repeat(acts.reshape(g_tok, PEER_PAIRS), 2, axis=1)
        if g >= 1:
            w2, _ = lax.optimization_barrier((w2, outs[g - 1]))
        outs.append(_peer_eval(vg, act2, w2, h1.reshape(g_tok, D_MODEL), g_f, l2g, l2b,
                               tb=peer_tb, tok0=g * g_tok, length=length))
        ready = (idx, w2, xmf) + ((outs[g - 1],) if g >= 1 else ())
    return jnp.concatenate(outs, axis=0).reshape(bsz, length, D_MODEL)
```

```python
import functools
import math

import jax
import jax.numpy as jnp
from jax import lax
from jax.experimental import pallas as pl
from jax.experimental.pallas import tpu as pltpu
from jax.experimental.pallas import tpu_sc as plsc

F32 = jnp.float32
BF16 = jnp.bfloat16

D_MODEL = 1024
DEPTH = 1
GRID_W = 64
N_HEADS = 8
QK_NOPE_DIM = 128
QK_ROPE_DIM = 64
V_HEAD_DIM = 128
Q_RANK = 384
KV_RANK = 256
ROPE_AXIS_DIM = QK_ROPE_DIM // 2
ROPE_BASE = 10000.0
S5_WIDTH = D_MODEL // 2
S5_GROUP = 16
S5_GROUPS = S5_WIDTH // S5_GROUP
S5_STATE = 64
PEER_HEADS = 8
PEER_N_KEYS = 128
PEER_TOPK = 16
PEER_HALF = 128
PEER_PAIRS = PEER_HEADS * PEER_TOPK
DEEPNORM_ALPHA = (2.0 * DEPTH) ** 0.25
LN_EPS = 1e-6
N_MOD = 6
ATT_SCALE = (QK_NOPE_DIM + QK_ROPE_DIM) ** -0.5

HEAD_PAD = 256
S5_COLS = S5_GROUPS * S5_STATE
S5_CHUNKS = 2
SC_WORKERS = 32
SC_LANES = 16
DOT_WIN = 32
SC_PAIRS = 64
TC_PAIRS = PEER_PAIRS - SC_PAIRS
CODE_NONE = 1.0e9
PEER_GROUPS = 8
HALF_W = D_MODEL // 2
VMEM_LIMIT = 48 * 1024 * 1024


def _dot(a, b):
    return jnp.dot(a, b, preferred_element_type=F32)


def _dot_nt(a, b):
    return lax.dot_general(a, b, (((1,), (1,)), ((), ())), preferred_element_type=F32)


def _gelu(x):
    return 0.5 * x * (1.0 + jnp.tanh(0.7978845608028654 * (x + 0.044715 * (x * x * x))))


def _ln_plain(x):
    mu = jnp.mean(x, axis=-1, keepdims=True)
    xc = x - mu
    var = jnp.mean(xc * xc, axis=-1, keepdims=True)
    return xc * lax.rsqrt(var + LN_EPS)


def _rms(x, g):
    return x * lax.rsqrt(jnp.mean(x * x, axis=-1, keepdims=True) + LN_EPS) * g


def _params(*sem):
    return pltpu.CompilerParams(dimension_semantics=sem, vmem_limit_bytes=VMEM_LIMIT)


def _mod_kernel(cond_ref, w_ref, b_ref, o_ref):
    a = cond_ref[...]
    a = a * jax.nn.sigmoid(a)
    a_hi = a.astype(BF16)
    a_lo = (a - a_hi.astype(F32)).astype(BF16)
    w = w_ref[...]
    w_hi = w.astype(BF16)
    w_lo = (w - w_hi.astype(F32)).astype(BF16)
    o_ref[...] = _dot(a_hi, w_hi) + _dot(a_lo, w_hi) + _dot(a_hi, w_lo) + b_ref[...]


def _modulation(cond, w_mod, b_mod):
    rows = cond.shape[0]
    n = w_mod.shape[1]
    blk = D_MODEL
    return pl.pallas_call(
        _mod_kernel,
        grid=(n // blk,),
        in_specs=[
            pl.BlockSpec((rows, D_MODEL), lambda j: (0, 0)),
            pl.BlockSpec((D_MODEL, blk), lambda j: (0, j)),
            pl.BlockSpec((1, blk), lambda j: (0, j)),
        ],
        out_specs=pl.BlockSpec((rows, blk), lambda j: (0, j)),
        out_shape=jax.ShapeDtypeStruct((rows, n), F32),
        compiler_params=_params("arbitrary"),
        name="modulation",
    )(cond, w_mod, b_mod.reshape(1, n))


def _rope128(t, c_tab, s_tab):
    return t * c_tab + pltpu.roll(t, 64, axis=1) * s_tab


def _inproj_kernel(x_ref, sh_ref, sc_ref, c_ref, s_ref, wcq_ref, wckv_ref, wsu_ref, wgm_ref, wgs_ref,
                   qg_ref, kvg_ref, wuq_ref, wukv_ref, *out_refs, is_ctx):
    if is_ctx:
        k_ref, v_ref, su_ref = out_refs
    else:
        q_ref, k_ref, v_ref, su_ref, gm_ref, gs_ref = out_refs
    xm = _ln_plain(x_ref[0]) * (1.0 + sc_ref[0]) + sh_ref[0]
    xb = xm.astype(BF16)
    c_tab = c_ref[...]
    s_tab = s_ref[...]

    ckvpe = _dot(xb, wckv_ref[...])
    ckv_n = _rms(ckvpe[:, :KV_RANK], kvg_ref[...]).astype(BF16)
    kpe = _rope128(ckvpe[:, KV_RANK:], c_tab, s_tab).astype(BF16)
    kv = _dot(ckv_n, wukv_ref[...])
    for h in range(N_HEADS):
        k_ref[0, h, :, 0:128] = kv[:, h * 256:h * 256 + 128].astype(BF16)
        k_ref[0, h, :, 128:256] = kpe
        v_ref[0, h] = kv[:, h * 256 + 128:(h + 1) * 256].astype(BF16)

    su_ref[...] = _dot(xb, wsu_ref[...])

    if not is_ctx:
        cq_n = _rms(_dot(xb, wcq_ref[...]), qg_ref[...]).astype(BF16)
        q = _dot(cq_n, wuq_ref[...])
        for h in range(N_HEADS):
            q_ref[0, h, :, 0:128] = (q[:, h * 256:h * 256 + 128] * ATT_SCALE).astype(BF16)
            q_ref[0, h, :, 128:256] = (_rope128(q[:, h * 256 + 128:(h + 1) * 256], c_tab, s_tab) * ATT_SCALE).astype(BF16)
        gm_ref[0] = jax.nn.sigmoid(_dot(xb, wgm_ref[...])).astype(BF16)
        gs_ref[0] = jax.nn.sigmoid(_dot(xb, wgs_ref[...])).astype(BF16)


def _inproj(x, shift, scale, c_tab, s_tab, wts, *, is_ctx, tm, b0, bsz):
    length = x.shape[1]
    mod_map = (lambda b, t: (0, 0, 0)) if shift.shape[0] == 1 else (lambda b, t: (b0 + b, 0, 0))
    const2 = lambda b, t: (0, 0)
    wcq, wckv, wsu, wgm, wgs, qg, kvg, wuq, wukv = wts
    in_specs = [
        pl.BlockSpec((1, tm, D_MODEL), lambda b, t: (b0 + b, t, 0)),
        pl.BlockSpec((1, 1, D_MODEL), mod_map),
        pl.BlockSpec((1, 1, D_MODEL), mod_map),
        pl.BlockSpec((tm, 128), lambda b, t: (t, 0)),
        pl.BlockSpec((tm, 128), lambda b, t: (t, 0)),
    ] + [pl.BlockSpec(w.shape, const2) for w in (wcq, wckv, wsu, wgm, wgs, qg, kvg, wuq, wukv)]
    head_spec = lambda width: pl.BlockSpec((1, N_HEADS, tm, width), lambda b, t: (b, 0, t, 0))
    row_spec = pl.BlockSpec((1, tm, D_MODEL), lambda b, t: (b, t, 0))
    su_spec = pl.BlockSpec((tm, S5_WIDTH), lambda b, t: (t, b))
    k_shape = jax.ShapeDtypeStruct((bsz, N_HEADS, length, HEAD_PAD), BF16)
    v_shape = jax.ShapeDtypeStruct((bsz, N_HEADS, length, V_HEAD_DIM), BF16)
    su_shape = jax.ShapeDtypeStruct((length, bsz * S5_WIDTH), F32)
    g_shape = jax.ShapeDtypeStruct((bsz, length, D_MODEL), BF16)
    if is_ctx:
        out_specs = [head_spec(HEAD_PAD), head_spec(V_HEAD_DIM), su_spec]
        out_shape = [k_shape, v_shape, su_shape]
    else:
        out_specs = [head_spec(HEAD_PAD), head_spec(HEAD_PAD), head_spec(V_HEAD_DIM), su_spec, row_spec, row_spec]
        out_shape = [k_shape, k_shape, v_shape, su_shape, g_shape, g_shape]
    return pl.pallas_call(
        functools.partial(_inproj_kernel, is_ctx=is_ctx),
        grid=(bsz, length // tm),
        in_specs=in_specs,
        out_specs=out_specs,
        out_shape=out_shape,
        compiler_params=_params("parallel", "parallel"),
        name="inproj_ctx" if is_ctx else "inproj_lat",
    )(x, shift, scale, c_tab, s_tab, wcq, wckv, wsu, wgm, wgs, qg, kvg, wuq, wukv)


def _attn_kernel(q_ref, kc_ref, vc_ref, kl_ref, vl_ref, o_ref):
    q = q_ref[0, 0]
    s_c = _dot_nt(q, kc_ref[0, 0])
    s_l = _dot_nt(q, kl_ref[0, 0])
    m = jnp.maximum(jnp.max(s_c, axis=-1, keepdims=True), jnp.max(s_l, axis=-1, keepdims=True))
    p_c = jnp.exp(s_c - m)
    p_l = jnp.exp(s_l - m)
    denom = jnp.sum(p_c, axis=-1, keepdims=True) + jnp.sum(p_l, axis=-1, keepdims=True)
    o = _dot(p_c.astype(BF16), vc_ref[0, 0]) + _dot(p_l.astype(BF16), vl_ref[0, 0])
    o_ref[0] = (o / denom).astype(BF16)


def _attention(q, k_ctx, v_ctx, k_lat, v_lat, *, tq, b0, bsz):
    _, heads, length, _ = q.shape
    l_ctx = k_ctx.shape[2]
    full = lambda b, h, i: (b0 + b, h, 0, 0)
    return pl.pallas_call(
        _attn_kernel,
        grid=(bsz, heads, length // tq),
        in_specs=[
            pl.BlockSpec((1, 1, tq, HEAD_PAD), lambda b, h, i: (b0 + b, h, i, 0)),
            pl.BlockSpec((1, 1, l_ctx, HEAD_PAD), full),
            pl.BlockSpec((1, 1, l_ctx, V_HEAD_DIM), full),
            pl.BlockSpec((1, 1, length, HEAD_PAD), full),
            pl.BlockSpec((1, 1, length, V_HEAD_DIM), full),
        ],
        out_specs=pl.BlockSpec((1, tq, V_HEAD_DIM), lambda b, h, i: (b, i, h)),
        out_shape=jax.ShapeDtypeStruct((bsz, length, heads * V_HEAD_DIM), BF16),
        compiler_params=_params("parallel", "parallel", "arbitrary"),
        name="attention",
    )(q, k_ctx, v_ctx, k_lat, v_lat)


def _s5_kernel(u_ref, are_ref, aim_ref, bre_ref, bim_ref, cre_ref, cim_ref, y_ref,
               bu_re, bu_im, h_re, h_im, *, tt, bsz, n_ctx_tiles):
    d = pl.program_id(0)
    i = pl.program_id(1)

    @pl.when(i == 0)
    def _():
        h_re[...] = jnp.zeros_like(h_re)
        h_im[...] = jnp.zeros_like(h_im)

    u = u_ref[...].astype(BF16)
    half = S5_COLS // S5_CHUNKS
    cw = S5_WIDTH // S5_CHUNKS
    for c in range(S5_CHUNKS):
        uc = u[:, c * cw:(c + 1) * cw]
        bu_re[:, c * half:(c + 1) * half] = _dot(uc, bre_ref[0, c])
        bu_im[:, c * half:(c + 1) * half] = _dot(uc, bim_ref[0, c])

    col_w = 512
    for cc in range(S5_COLS // col_w):
        cols = slice(cc * col_w, (cc + 1) * col_w)
        a_r = are_ref[0, :, cols]
        a_i = aim_ref[0, :, cols]

        def step(t, carry, cols=cols, a_r=a_r, a_i=a_i):
            hr, hi = carry
            pos = jnp.where(d == 0, t, tt - 1 - t)
            r = pl.multiple_of(pos * bsz, bsz)
            nr = a_r * hr - a_i * hi + bu_re[pl.ds(r, bsz), cols]
            ni = a_r * hi + a_i * hr + bu_im[pl.ds(r, bsz), cols]
            bu_re[pl.ds(r, bsz), cols] = nr
            bu_im[pl.ds(r, bsz), cols] = ni
            return nr, ni

        hr, hi = lax.fori_loop(0, tt, step, (h_re[:, cols], h_im[:, cols]))
        h_re[:, cols] = hr
        h_im[:, cols] = hi

    @pl.when(i >= n_ctx_tiles)
    def _():
        for c in range(S5_CHUNKS):
            sl = slice(c * half, (c + 1) * half)
            y = _dot(bu_re[:, sl].astype(BF16), cre_ref[0, c]) + _dot(bu_im[:, sl].astype(BF16), cim_ref[0, c])
            y_ref[0, :, c * cw:(c + 1) * cw] = y


def _s5_scan(u_all, mats, *, bsz, l_ctx, tt):
    a_re_b, a_im_b, b_re_bd, b_im_bd, c_re_bd, c_im_bd = mats
    rows = tt * bsz
    n_tiles = u_all.shape[0] // rows
    n_ctx_tiles = l_ctx // tt
    n_lat_tiles = n_tiles - n_ctx_tiles

    def u_map(d, i):
        bwd = jnp.where(i < n_ctx_tiles, n_ctx_tiles - 1 - i, n_tiles - 1 - (i - n_ctx_tiles))
        return (jnp.where(d == 0, i, bwd), 0)

    def y_map(d, i):
        j = jnp.maximum(i - n_ctx_tiles, 0)
        return (d, jnp.where(d == 0, j, n_lat_tiles - 1 - j), 0)

    dmap3 = lambda d, i: (d, 0, 0)
    dmap4 = lambda d, i: (d, 0, 0, 0)
    return pl.pallas_call(
        functools.partial(_s5_kernel, tt=tt, bsz=bsz, n_ctx_tiles=n_ctx_tiles),
        grid=(2, n_tiles),
        in_specs=[
            pl.BlockSpec((rows, S5_WIDTH), u_map),
            pl.BlockSpec((1,) + a_re_b.shape[1:], dmap3),
            pl.BlockSpec((1,) + a_im_b.shape[1:], dmap3),
            pl.BlockSpec((1,) + b_re_bd.shape[1:], dmap4),
            pl.BlockSpec((1,) + b_im_bd.shape[1:], dmap4),
            pl.BlockSpec((1,) + c_re_bd.shape[1:], dmap4),
            pl.BlockSpec((1,) + c_im_bd.shape[1:], dmap4),
        ],
        out_specs=pl.BlockSpec((1, rows, S5_WIDTH), y_map),
        out_shape=jax.ShapeDtypeStruct((2, n_lat_tiles * rows, S5_WIDTH), F32),
        scratch_shapes=[
            pltpu.VMEM((rows, S5_COLS), F32),
            pltpu.VMEM((rows, S5_COLS), F32),
            pltpu.VMEM((bsz, S5_COLS), F32),
            pltpu.VMEM((bsz, S5_COLS), F32),
        ],
        compiler_params=_params("arbitrary", "arbitrary"),
        name="s5_scan",
    )(u_all, a_re_b, a_im_b, b_re_bd, b_im_bd, c_re_bd, c_im_bd)


def _s5_matrices(a_re, a_im, log_dt, b_re, b_im, c_re, c_im, bsz):
    dt = jnp.exp(log_dt)[..., None]
    mag = jnp.exp(a_re * dt)
    ab_re = mag * jnp.cos(a_im * dt)
    ab_im = mag * jnp.sin(a_im * dt)
    den = a_re * a_re + a_im * a_im
    f_re = ((ab_re - 1.0) * a_re + ab_im * a_im) / den
    f_im = (ab_im * a_re - (ab_re - 1.0) * a_im) / den
    bb_re = f_re[..., None] * b_re - f_im[..., None] * b_im
    bb_im = f_re[..., None] * b_im + f_im[..., None] * b_re
    gl = S5_GROUPS // S5_CHUNKS
    eye = jnp.eye(gl, dtype=F32)

    def in_bd(bb):
        t = bb.reshape(2, S5_CHUNKS, gl, S5_STATE, S5_GROUP)
        t = jnp.einsum('dkgpc,gh->dkgchp', t, eye)
        return t.reshape(2, S5_CHUNKS, gl * S5_GROUP, gl * S5_STATE).astype(BF16)

    def out_bd(cc):
        t = cc.reshape(2, S5_CHUNKS, gl, S5_GROUP, S5_STATE)
        t = jnp.einsum('dkgcp,gh->dkgphc', t, eye)
        return t.reshape(2, S5_CHUNKS, gl * S5_STATE, gl * S5_GROUP).astype(BF16)

    bcast = lambda a: jnp.broadcast_to(a.reshape(2, 1, S5_COLS), (2, bsz, S5_COLS))
    return bcast(ab_re), bcast(ab_im), in_bd(bb_re), in_bd(bb_im), out_bd(c_re), out_bd(-c_im)


def _merge_kernel(x_ref, att_ref, gm_ref, gs_ref, su_ref, y_ref, dsk_ref, wglu_ref, wout_ref, ga_ref,
                  l1g_ref, l1b_ref, shf_ref, scf_ref, wq_ref, keys_ref, h1_ref, xm_ref, st_ref):
    y = su_ref[...] * dsk_ref[...] + y_ref[0] + y_ref[1]
    gl = _dot(_gelu(y).astype(BF16), wglu_ref[...])
    s5_out = gl[:, :D_MODEL] * jax.nn.sigmoid(gl[:, D_MODEL:])
    merged = gm_ref[0].astype(F32) * att_ref[0].astype(F32) + gs_ref[0].astype(F32) * s5_out
    out = _dot(merged.astype(BF16), wout_ref[...])
    h1 = _ln_plain(DEEPNORM_ALPHA * x_ref[0] + ga_ref[0] * out) * l1g_ref[...] + l1b_ref[...]
    h1_ref[0] = h1
    xm = _ln_plain(h1) * (1.0 + scf_ref[0]) + shf_ref[0]
    xm_ref[0] = xm
    qp = _dot(xm.astype(BF16), wq_ref[...]).astype(BF16)
    for j in range(2 * PEER_HEADS):
        st_ref[j] = _dot_nt(keys_ref[j], qp[:, j * PEER_HALF:(j + 1) * PEER_HALF])


def _merge(x, att, gm, gs, su_all2d, y2d, dsk, wglu, wout, g_a, l1g, l1b, sh_f, sc_f, wq, keys, *, tm, ctx_tiles, b0, h0):
    bsz, length, _ = att.shape
    nt = length // tm
    row = pl.BlockSpec((1, tm, D_MODEL), lambda b, t: (b, t, 0))
    xrow = pl.BlockSpec((1, tm, D_MODEL), lambda b, t: (b0 + b, t, 0))
    hrow = pl.BlockSpec((1, tm, D_MODEL), lambda b, t: (h0 + b, t, 0))
    modb = pl.BlockSpec((1, 1, D_MODEL), lambda b, t: (b0 + b, 0, 0))
    const = lambda a: pl.BlockSpec(a.shape, lambda b, t: (0,) * a.ndim)
    return pl.pallas_call(
        _merge_kernel,
        grid=(bsz, nt),
        in_specs=[
            xrow, row, hrow, hrow,
            pl.BlockSpec((tm, S5_WIDTH), lambda b, t: (ctx_tiles + t, h0 + b)),
            pl.BlockSpec((2, tm, S5_WIDTH), lambda b, t: (0, t, h0 + b)),
            const(dsk), const(wglu), const(wout), modb, const(l1g), const(l1b), modb, modb, const(wq), const(keys),
        ],
        out_specs=[row, row, pl.BlockSpec((2 * PEER_HEADS, PEER_N_KEYS, tm), lambda b, t: (0, 0, b * nt + t))],
        out_shape=[
            jax.ShapeDtypeStruct((bsz, length, D_MODEL), F32),
            jax.ShapeDtypeStruct((bsz, length, D_MODEL), F32),
            jax.ShapeDtypeStruct((2 * PEER_HEADS, PEER_N_KEYS, bsz * length), F32),
        ],
        compiler_params=_params("parallel", "parallel"),
        name="merge_peer_query",
    )(x, att, gm, gs, su_all2d, y2d, dsk, wglu, wout, g_a, l1g, l1b, sh_f, sc_f, wq, keys)


def _take_top(vals, codes, payload, k):
    rows = lax.broadcasted_iota(jnp.int32, (k, vals.shape[1]), 0)
    top_v = jnp.zeros((k, vals.shape[1]), F32)
    top_p = jnp.zeros((k, vals.shape[1]), F32)
    for r in range(k):
        m = jnp.max(vals, axis=0, keepdims=True)
        cm = jnp.min(jnp.where(vals == m, codes, CODE_NONE), axis=0, keepdims=True)
        sel = codes == cm
        if payload is None:
            p = cm
        else:
            p = jnp.max(jnp.where(sel, payload, -1.0), axis=0, keepdims=True)
        vals = jnp.where(sel, -jnp.inf, vals)
        top_v = jnp.where(rows == r, m, top_v)
        top_p = jnp.where(rows == r, p, top_p)
    return top_v, top_p


def _topk_kernel(st_ref, e_ref, g_ref):
    lanes = st_ref.shape[2]
    key_idx = lax.broadcasted_iota(jnp.int32, (PEER_N_KEYS, lanes), 0).astype(F32)
    sub_idx = lax.broadcasted_iota(jnp.int32, (PEER_TOPK, lanes), 0).astype(F32)
    half_k = PEER_TOPK // 2
    for h in range(PEER_HEADS):
        sv0, si0 = _take_top(st_ref[2 * h], key_idx, None, PEER_TOPK)
        sv1, si1 = _take_top(st_ref[2 * h + 1], key_idx, None, PEER_TOPK)
        cs, ce, cc = [], [], []
        for i in range(half_k):
            nj = PEER_TOPK if i == 0 else half_k
            cs.append(sv0[i:i + 1, :] + sv1[0:nj, :])
            ce.append(si0[i:i + 1, :] * PEER_N_KEYS + si1[0:nj, :])
            cc.append(sub_idx[0:nj, :] + i * PEER_TOPK)
        cs.append(sv0[half_k:, :] + sv1[0:1, :])
        ce.append(si0[half_k:, :] * PEER_N_KEYS + si1[0:1, :])
        cc.append((sub_idx[0:half_k, :] + half_k) * PEER_TOPK)
        top_s, top_e = _take_top(jnp.concatenate(cs, axis=0), jnp.concatenate(cc, axis=0),
                                 jnp.concatenate(ce, axis=0), PEER_TOPK)
        p = jnp.exp(top_s - jnp.max(top_s, axis=0, keepdims=True))
        g_ref[h] = p / jnp.sum(p, axis=0, keepdims=True)
        e_ref[h] = top_e.astype(jnp.int32)


def _peer_topk(st, *, lanes):
    n = st.shape[2]
    out_spec = pl.BlockSpec((PEER_HEADS, PEER_TOPK, lanes), lambda i: (0, 0, i))
    return pl.pallas_call(
        _topk_kernel,
        grid=(n // lanes,),
        in_specs=[pl.BlockSpec((2 * PEER_HEADS, PEER_N_KEYS, lanes), lambda i: (0, 0, i))],
        out_specs=[out_spec, out_spec],
        out_shape=[jax.ShapeDtypeStruct((PEER_HEADS, PEER_TOPK, n), jnp.int32),
                   jax.ShapeDtypeStruct((PEER_HEADS, PEER_TOPK, n), F32)],
        compiler_params=_params("parallel"),
        name="peer_topk",
    )(st)


def _sc_peer(u_tab, v_tab, idx, x, gates):
    n_tok = x.shape[0]
    per_worker = n_tok // SC_WORKERS
    assert per_worker % 2 == 0 and per_worker >= 4
    n_win = PEER_PAIRS // DOT_WIN
    steps = PEER_PAIRS // SC_LANES
    per_win = DOT_WIN // SC_LANES
    n_vc = SC_PAIRS // SC_LANES
    n_vs = (PEER_PAIRS - SC_PAIRS) // SC_LANES
    tc_pairs = PEER_PAIRS - SC_PAIRS
    chunks = HALF_W // SC_LANES
    mesh = plsc.VectorSubcoreMesh(core_axis_name="c", subcore_axis_name="s")

    @functools.partial(
        pl.kernel, mesh=mesh,
        out_type=(jax.ShapeDtypeStruct((n_tok * PEER_PAIRS,), F32),
                  jax.ShapeDtypeStruct((n_tok * tc_pairs, HALF_W), jnp.int32),
                  jax.ShapeDtypeStruct((n_tok, 2 * HALF_W), F32)),
        scratch_types=[
            pltpu.VMEM((2, PEER_PAIRS), jnp.int32),
            pltpu.VMEM((2, PEER_PAIRS), jnp.int32),
            pltpu.VMEM((2, 2 * HALF_W), F32),
            pltpu.VMEM((2, SC_PAIRS), F32),
            pltpu.VMEM((2, SC_PAIRS), F32),
            pltpu.VMEM((2, PEER_PAIRS), F32),
            pltpu.VMEM((2, 2 * HALF_W), F32),
            pltpu.VMEM((n_win, DOT_WIN, HALF_W), jnp.int32),
            pltpu.VMEM((n_vc, SC_LANES, HALF_W), jnp.int32),
            pltpu.VMEM((2, SC_LANES, HALF_W), jnp.int32),
            pltpu.SemaphoreType.DMA((n_win,)),
            pltpu.SemaphoreType.DMA((n_vc,)),
            pltpu.SemaphoreType.DMA((2,)),
            pltpu.SemaphoreType.DMA((2,)),
            pltpu.SemaphoreType.DMA((2,)),
            pltpu.SemaphoreType.DMA((2,)),
            pltpu.SemaphoreType.DMA((2,)),
            pltpu.SemaphoreType.DMA((2,)),
            pltpu.SemaphoreType.DMA((2,)),
        ],
        compiler_params=pltpu.CompilerParams(needs_layout_passes=False),
    )
    def peer(u_hbm, v_hbm, idx_hbm, x_hbm, gate_hbm, act_hbm, vrows_hbm, f_hbm,
             idx_v, vidx_v, x_v, gate_v, coef_v, act_v, f_v, urows_v, vsum_v, vstage_v,
             u_sems, vc_sems, vg_sems, vw_sems, idx_sems, x_sems, gate_sems, act_sems, f_sems):
        wid = lax.axis_index("s") * 2 + lax.axis_index("c")
        base = wid * per_worker
        last = base + per_worker - 1
        lane = lax.iota(jnp.int32, SC_LANES)

        def pair_off(tok):
            return pl.multiple_of(tok * PEER_PAIRS, PEER_PAIRS)

        def idx_copy(tok, slot):
            return pltpu.make_async_copy(idx_hbm.at[pl.ds(pair_off(tok), PEER_PAIRS)], idx_v.at[slot], idx_sems.at[slot])

        def x_copy(tok, slot):
            return pltpu.make_async_copy(x_hbm.at[tok], x_v.at[slot], x_sems.at[slot])

        def gate_copy(tok, slot):
            return pltpu.make_async_copy(gate_hbm.at[pl.ds(pair_off(tok), SC_PAIRS)], gate_v.at[slot], gate_sems.at[slot])

        def u_gather(slot, w):
            return pltpu.make_async_copy(u_hbm.at[idx_v.at[slot, pl.ds(w * DOT_WIN, DOT_WIN)]], urows_v.at[w],
                                         u_sems.at[w])

        def vsum_gather(slot, j):
            return pltpu.make_async_copy(v_hbm.at[vidx_v.at[slot, pl.ds(j * SC_LANES, SC_LANES)]], vsum_v.at[j],
                                         vc_sems.at[j])

        def vstage_gather(slot, s):
            return pltpu.make_async_copy(v_hbm.at[vidx_v.at[slot, pl.ds(SC_PAIRS + s * SC_LANES, SC_LANES)]],
                                         vstage_v.at[s % 2], vg_sems.at[s % 2])

        def vstage_write(tok, s):
            row0 = pl.multiple_of(tok * tc_pairs + s * SC_LANES, SC_LANES)
            return pltpu.make_async_copy(vstage_v.at[s % 2], vrows_hbm.at[pl.ds(row0, SC_LANES)], vw_sems.at[s % 2])

        def act_copy(tok, slot):
            return pltpu.make_async_copy(act_v.at[slot], act_hbm.at[pl.ds(pair_off(tok), PEER_PAIRS)], act_sems.at[slot])

        def f_copy(tok, slot):
            return pltpu.make_async_copy(f_v.at[slot], f_hbm.at[tok], f_sems.at[slot])

        def unpack(wv):
            return (lax.bitcast_convert_type(wv << 16, F32), lax.bitcast_convert_type(wv & jnp.int32(-65536), F32))

        def step_dots(slot, g):
            w, sub = g // per_win, g % per_win

            def body(c, accs):
                xlo = x_v[slot, pl.ds(c * SC_LANES, SC_LANES)]
                xhi = x_v[slot, pl.ds(HALF_W + c * SC_LANES, SC_LANES)]
                new = []
                for r in range(SC_LANES):
                    lo, hi = unpack(urows_v[w, sub * SC_LANES + r, pl.ds(c * SC_LANES, SC_LANES)])
                    new.append(accs[r] + lo * xlo + hi * xhi)
                return tuple(new)

            accs = lax.fori_loop(0, chunks, body, tuple(jnp.zeros((SC_LANES,), F32) for _ in range(SC_LANES)))
            v = jnp.zeros((SC_LANES,), F32)
            for r in range(SC_LANES):
                v = jnp.where(lane == r, jnp.sum(accs[r]), v)
            act_v[slot, pl.ds(g * SC_LANES, SC_LANES)] = v

        def window_sum(fslot, j):
            cvec = coef_v[fslot, pl.ds(j * SC_LANES, SC_LANES)]
            cb = [jnp.zeros((SC_LANES,), F32) + jnp.sum(jnp.where(lane == r, cvec, 0.0)) for r in range(SC_LANES)]

            def body(c, carry):
                alo = f_v[fslot, pl.ds(c * SC_LANES, SC_LANES)]
                ahi = f_v[fslot, pl.ds(HALF_W + c * SC_LANES, SC_LANES)]
                for r in range(SC_LANES):
                    lo, hi = unpack(vsum_v[j, r, pl.ds(c * SC_LANES, SC_LANES)])
                    alo = alo + cb[r] * lo
                    ahi = ahi + cb[r] * hi
                f_v[fslot, pl.ds(c * SC_LANES, SC_LANES)] = alo
                f_v[fslot, pl.ds(HALF_W + c * SC_LANES, SC_LANES)] = ahi
                return carry

            lax.fori_loop(0, chunks, body, 0)

        def token_coefs(slot):
            for q in range(n_vc):
                a = act_v[slot, pl.ds(q * SC_LANES, SC_LANES)]
                z2 = 1.5957691216057308 * (a + 0.044715 * (a * a * a))
                coef_v[slot, pl.ds(q * SC_LANES, SC_LANES)] = gate_v[slot, pl.ds(q * SC_LANES, SC_LANES)] * a / (1.0 + jnp.exp(-z2))

        def zero_f(fslot):
            for q in range(2 * HALF_W // SC_LANES):
                f_v[fslot, pl.ds(q * SC_LANES, SC_LANES)] = jnp.zeros((SC_LANES,), F32)

        idx_copy(base, 0).start()
        x_copy(base, 0).start()
        gate_copy(base, 0).start()
        idx_copy(base, 0).wait()
        for w in range(n_win):
            u_gather(0, w).start()

        @pl.loop(0, per_worker, step=2)
        def _(i):
            for slot in range(2):
                tok = base + i + slot
                nxt = jnp.minimum(tok + 1, last)
                other = 1 - slot
                first = slot == 0
                idx_copy(nxt, other).start()
                x_copy(nxt, other).start()
                gate_copy(nxt, other).start()
                for q in range(PEER_PAIRS // SC_LANES):
                    vidx_v[slot, pl.ds(q * SC_LANES, SC_LANES)] = idx_v[slot, pl.ds(q * SC_LANES, SC_LANES)]
                x_copy(tok, slot).wait()
                gate_copy(tok, slot).wait()

                @pl.when(i > 0)
                def _():
                    act_copy(tok, slot).wait()

                if first:
                    @pl.when(i > 2)
                    def _():
                        f_copy(tok, other).wait()
                else:
                    @pl.when(i > 0)
                    def _():
                        f_copy(tok, other).wait()
                zero_f(other)

                idx_copy(nxt, other).wait()
                for g in range(steps):
                    if g % 2 == 0:
                        s = g // 2
                        if first and s < 2:
                            @pl.when(i > 0)
                            def _(s=s, tok=tok):
                                vstage_write(tok, s).wait()
                        else:
                            vstage_write(tok, s).wait()
                        vstage_gather(slot, s).start()
                    if g % per_win == 0:
                        u_gather(slot, g // per_win).wait()
                    step_dots(slot, g)
                    if g % per_win == per_win - 1:
                        u_gather(other, g // per_win).start()
                    if g % 2 == 0:
                        sp = (g // 2 - 1) % n_vs
                        tokp = tok if g >= 2 else tok - 1

                        def finish(slot=slot, sp=sp, tokp=tokp):
                            vstage_gather(slot, sp).wait()
                            vstage_write(tokp, sp).start()

                        if first and g < 2:
                            pl.when(i > 0)(finish)
                        else:
                            finish()
                    else:
                        j = g // 2

                        def consume(j=j, other=other):
                            vsum_gather(other, j).wait()
                            window_sum(other, j)

                        if first:
                            pl.when(i > 0)(consume)
                        else:
                            consume()
                        vsum_gather(slot, j).start()
                token_coefs(slot)
                act_copy(tok, slot).start()

                def send_prev(tok=tok, other=other):
                    f_copy(tok - 1, other).start()

                if first:
                    pl.when(i > 0)(send_prev)
                else:
                    send_prev()

        vstage_gather(1, n_vs - 1).wait()
        vstage_write(last, n_vs - 1).start()
        f_copy(last, 1).wait()
        zero_f(1)
        for j in range(n_vc):
            vsum_gather(1, j).wait()
            window_sum(1, j)
        f_copy(last, 1).start()
        for b in range(2):
            vstage_write(last, b).wait()
        for w in range(n_win):
            u_gather(0, w).wait()
        x_copy(last, 0).wait()
        gate_copy(last, 0).wait()
        for slot in range(2):
            f_copy(last, slot).wait()
            act_copy(last, slot).wait()

    return peer(u_tab, v_tab, idx, x, gates)


def _peer_kernel(vg_ref, act2_ref, w2_ref, fsc_ref, h1_ref, gf_ref, l2g_ref, l2b_ref, o_ref, f_ref, *, tb):
    sub = 8
    span = sub * TC_PAIRS
    pair_w = 2 * TC_PAIRS
    half = D_MODEL // 2
    lane = lax.broadcasted_iota(jnp.int32, (sub, 2 * span), 1)
    diag = (lane // pair_w) == lax.broadcasted_iota(jnp.int32, (sub, 2 * span), 0)
    even = (lane % 2) == 0
    for s in range(tb // sub):
        rows = slice(s * span, (s + 1) * span)
        toks = slice(s * sub, (s + 1) * sub)
        coef = w2_ref[toks, :] * _gelu(act2_ref[toks, :])
        cdiag = jnp.where(diag, jnp.concatenate([coef] * sub, axis=1), 0.0)
        c2 = jnp.concatenate([jnp.where(even, cdiag, 0.0), jnp.where(even, 0.0, cdiag)], axis=0).astype(BF16)
        f2 = _dot(c2, pltpu.bitcast(vg_ref[rows, :], BF16))
        f_ref[toks, :half] = f2[:sub]
        f_ref[toks, half:] = f2[sub:]
    r = DEEPNORM_ALPHA * h1_ref[...] + gf_ref[0] * (f_ref[...] + fsc_ref[...])
    o_ref[...] = _ln_plain(r) * l2g_ref[...] + l2b_ref[...]


def _peer_eval(vg, act2, w2, f_sc, h1, g_f, l2g, l2b, *, tb, tok0, length):
    n = h1.shape[0]
    tok_blk0 = tok0 // tb
    per_b = length // tb
    tok_spec = lambda width: pl.BlockSpec((tb, width), lambda i: (i, 0))
    return pl.pallas_call(
        functools.partial(_peer_kernel, tb=tb),
        grid=(n // tb,),
        in_specs=[
            pl.BlockSpec((tb * TC_PAIRS, D_MODEL // 2), lambda i: (i, 0)),
            tok_spec(2 * TC_PAIRS), tok_spec(2 * TC_PAIRS), tok_spec(D_MODEL), tok_spec(D_MODEL),
            pl.BlockSpec((1, 1, D_MODEL), lambda i: ((tok_blk0 + i) // per_b, 0, 0)),
            pl.BlockSpec((1, D_MODEL), lambda i: (0, 0)),
            pl.BlockSpec((1, D_MODEL), lambda i: (0, 0)),
        ],
        out_specs=tok_spec(D_MODEL),
        out_shape=jax.ShapeDtypeStruct((n, D_MODEL), F32),
        scratch_shapes=[pltpu.VMEM((tb, D_MODEL), F32)],
        compiler_params=_params("parallel"),
        name="peer_eval",
    )(vg, act2, w2, f_sc, h1, g_f, l2g, l2b)


def _rope_perm():
    half = ROPE_AXIS_DIM // 2
    base = jnp.arange(QK_ROPE_DIM)
    return jnp.where((base % ROPE_AXIS_DIM) < half, base + half, base - half)


def _rope_tables(length):
    pos = jnp.arange(length)
    row = (pos // GRID_W).astype(F32)
    col = (pos % GRID_W).astype(F32)
    inv_freq = jnp.power(ROPE_BASE, -jnp.arange(0, ROPE_AXIS_DIM, 2, dtype=F32) / ROPE_AXIS_DIM)
    ang_r = row[:, None] * inv_freq
    ang_c = col[:, None] * inv_freq
    zeros = jnp.zeros((length, QK_ROPE_DIM), F32)
    c_tab = jnp.concatenate([jnp.cos(ang_r), jnp.cos(ang_r), jnp.cos(ang_c), jnp.cos(ang_c), zeros], axis=1)
    s_tab = jnp.concatenate([-jnp.sin(ang_r), jnp.sin(ang_r), -jnp.sin(ang_c), jnp.sin(ang_c), zeros], axis=1)
    return c_tab, s_tab


def _identity_tables(length):
    ones = jnp.ones((length, QK_ROPE_DIM), F32)
    zeros = jnp.zeros((length, QK_ROPE_DIM), F32)
    return jnp.concatenate([ones, zeros], axis=1), jnp.zeros((length, 2 * QK_ROPE_DIM), F32)


def _pack_rows(table):
    bits = lax.bitcast_convert_type(table.astype(BF16), jnp.uint16).astype(jnp.uint32)
    return lax.bitcast_convert_type(bits[:, :HALF_W] | (bits[:, HALF_W:] << 16), jnp.int32)


def _layer_weights(w_in, q_norm_g, kv_norm_g, w_uq, w_ukv):
    perm = _rope_perm()
    p0 = Q_RANK
    p1 = p0 + KV_RANK
    p2 = p1 + QK_ROPE_DIM
    p3 = p2 + S5_WIDTH
    p4 = p3 + D_MODEL
    w_kpe = w_in[:, p1:p2]
    wckv = jnp.concatenate([w_in[:, p0:p1], w_kpe, w_kpe[:, perm]], axis=1)
    uq = w_uq.reshape(Q_RANK, N_HEADS, QK_NOPE_DIM + QK_ROPE_DIM)
    pe = uq[:, :, QK_NOPE_DIM:]
    wuq = jnp.concatenate([uq[:, :, :QK_NOPE_DIM], pe, pe[:, :, perm]], axis=2).reshape(Q_RANK, N_HEADS * HEAD_PAD)
    cast = lambda a: a.astype(BF16)
    return (cast(w_in[:, :p0]), cast(wckv), cast(w_in[:, p2:p3]), cast(w_in[:, p3:p4]), cast(w_in[:, p4:]),
            q_norm_g.reshape(1, Q_RANK), kv_norm_g.reshape(1, KV_RANK), cast(wuq), cast(w_ukv))


def kernel(x, c, ctx, c_ctx, w_mod, b_mod, w_in, q_norm_g, kv_norm_g, w_uq, w_ukv, s5_a_re, s5_a_im, s5_log_dt, s5_b_re, s5_b_im, s5_c_re, s5_c_im, s5_d, w_glu, w_out, ln1_g, ln1_b, peer_wq, peer_keys, peer_u, peer_v, ln2_g, ln2_b):
    bsz, length, _ = x.shape
    l_ctx = ctx.shape[1]
    n_tok = bsz * length
    layer = 0
    tm = 256
    s5_tt = 64
    peer_tb = 32

    cond = jnp.concatenate([c, c_ctx[None, :], jnp.zeros((7, D_MODEL), F32)], axis=0)
    mod = _modulation(cond, w_mod[layer], b_mod[layer]).reshape(cond.shape[0], N_MOD, 1, D_MODEL)
    sh_a, sc_a, g_a, sh_f, sc_f, g_f = (mod[:bsz, i] for i in range(N_MOD))
    csh_a, csc_a = mod[bsz:bsz + 1, 0], mod[bsz:bsz + 1, 1]

    wts = _layer_weights(w_in[layer], q_norm_g[layer], kv_norm_g[layer], w_uq[layer], w_ukv[layer])
    c_lat, s_lat = _rope_tables(length)
    c_id, s_id = _identity_tables(l_ctx)
    keys = peer_keys[layer].reshape(2 * PEER_HEADS, PEER_N_KEYS, PEER_HALF).astype(BF16)
    wglu, wout, wq = w_glu[layer].astype(BF16), w_out[layer].astype(BF16), peer_wq[layer].astype(BF16)
    u_pack = _pack_rows(peer_u[layer])
    v_pack = _pack_rows(peer_v[layer])
    l2g = ln2_g[layer].reshape(1, D_MODEL)
    l2b = ln2_b[layer].reshape(1, D_MODEL)

    n_groups = PEER_GROUPS if bsz % PEER_GROUPS == 0 else 1
    halves = 2 if (n_groups % 2 == 0 and bsz % 16 == 0) else 1
    per_half = n_groups // halves
    hb = bsz // halves
    gb = bsz // n_groups
    g_tok = gb * length
    mats = _s5_matrices(s5_a_re[layer], s5_a_im[layer], s5_log_dt[layer], s5_b_re[layer], s5_b_im[layer],
                        s5_c_re[layer], s5_c_im[layer], hb)
    outs = []
    ready = (u_pack, v_pack)
    for g in range(n_groups):
        b0 = g * gb
        h0 = (g % per_half) * gb
        if g % per_half == 0:
            x_h, ctx_h, _ = lax.optimization_barrier((x, ctx, ready))
            k_ctx, v_ctx, su_ctx = _inproj(ctx_h, csh_a, csc_a, c_id, s_id, wts, is_ctx=True, tm=min(tm, l_ctx), b0=b0, bsz=hb)
            q, k_lat, v_lat, su_lat, gm, gs = _inproj(x_h, sh_a, sc_a, c_lat, s_lat, wts, is_ctx=False, tm=tm, b0=b0, bsz=hb)
            su_all = jnp.concatenate([su_ctx, su_lat], axis=0)
            y = _s5_scan(su_all.reshape(-1, S5_WIDTH), mats, bsz=hb, l_ctx=l_ctx, tt=s5_tt)
            y2d = y.reshape(2, length, hb * S5_WIDTH)
        q_g, _ = lax.optimization_barrier((q, ready))
        att = _attention(q_g, k_ctx, v_ctx, k_lat, v_lat, tq=min(512, length), b0=h0, bsz=gb)
        h1, xm, st = _merge(x, att, gm, gs, su_all, y2d, s5_d[layer].reshape(1, S5_WIDTH), wglu, wout, g_a,
                            ln1_g[layer].reshape(1, D_MODEL), ln1_b[layer].reshape(1, D_MODEL), sh_f, sc_f, wq, keys,
                            tm=tm, ctx_tiles=l_ctx // tm, b0=b0, h0=h0)
        experts, gates = _peer_topk(st, lanes=128)
        idx = experts.transpose(2, 0, 1).reshape(g_tok * PEER_PAIRS)
        gate = gates.transpose(2, 0, 1).reshape(g_tok, PEER_PAIRS)
        xmf = xm.reshape(g_tok, D_MODEL)
        acts, vg, f_sc = _sc_peer(u_pack, v_pack, idx, xmf, gate.reshape(g_tok * PEER_PAIRS))
        act2 = jnp.repeat(acts.reshape(g_tok, PEER_PAIRS)[:, SC_PAIRS:], 2, axis=1)
        w2 = jnp.repeat(gate[:, SC_PAIRS:], 2, axis=1)
        if g >= 1:
            w2, _ = lax.optimization_barrier((w2, outs[g - 1]))
        outs.append(_peer_eval(vg, act2, w2, f_sc, h1.reshape(g_tok, D_MODEL), g_f, l2g, l2b,
                               tb=peer_tb, tok0=g * g_tok, length=length))
        ready = (idx, gate, xmf) + ((outs[g - 1],) if g >= 1 else ())
    return jnp.concatenate(outs, axis=0).reshape(bsz, length, D_MODEL)
```

```python
import functools
import math

import jax
import jax.numpy as jnp
from jax import lax
from jax.experimental import pallas as pl
from jax.experimental.pallas import tpu as pltpu
from jax.experimental.pallas import tpu_sc as plsc

F32 = jnp.float32
BF16 = jnp.bfloat16

D_MODEL = 1024
DEPTH = 1
GRID_W = 64
N_HEADS = 8
QK_NOPE_DIM = 128
QK_ROPE_DIM = 64
V_HEAD_DIM = 128
Q_RANK = 384
KV_RANK = 256
ROPE_AXIS_DIM = QK_ROPE_DIM // 2
ROPE_BASE = 10000.0
S5_WIDTH = D_MODEL // 2
S5_GROUP = 16
S5_GROUPS = S5_WIDTH // S5_GROUP
S5_STATE = 64
PEER_HEADS = 8
PEER_N_KEYS = 128
PEER_TOPK = 16
PEER_HALF = 128
PEER_PAIRS = PEER_HEADS * PEER_TOPK
DEEPNORM_ALPHA = (2.0 * DEPTH) ** 0.25
LN_EPS = 1e-6
N_MOD = 6
ATT_SCALE = (QK_NOPE_DIM + QK_ROPE_DIM) ** -0.5

HEAD_PAD = 256
S5_COLS = S5_GROUPS * S5_STATE
S5_CHUNKS = 2
SC_WORKERS = 32
SC_LANES = 16
DOT_WIN = 32
SC_PAIRS = 48
STAGE_STEPS = (0, 2, 3, 5, 6)
SUM_STEPS = (1, 4, 7)
TC_PAIRS = PEER_PAIRS - SC_PAIRS
CODE_NONE = 1.0e9
PEER_GROUPS = 8
HALF_W = D_MODEL // 2
VMEM_LIMIT = 48 * 1024 * 1024


def _dot(a, b):
    return jnp.dot(a, b, preferred_element_type=F32)


def _dot_nt(a, b):
    return lax.dot_general(a, b, (((1,), (1,)), ((), ())), preferred_element_type=F32)


def _gelu(x):
    return 0.5 * x * (1.0 + jnp.tanh(0.7978845608028654 * (x + 0.044715 * (x * x * x))))


def _ln_plain(x):
    mu = jnp.mean(x, axis=-1, keepdims=True)
    xc = x - mu
    var = jnp.mean(xc * xc, axis=-1, keepdims=True)
    return xc * lax.rsqrt(var + LN_EPS)


def _rms(x, g):
    return x * lax.rsqrt(jnp.mean(x * x, axis=-1, keepdims=True) + LN_EPS) * g


def _params(*sem):
    return pltpu.CompilerParams(dimension_semantics=sem, vmem_limit_bytes=VMEM_LIMIT)


def _mod_kernel(cond_ref, w_ref, b_ref, o_ref):
    a = cond_ref[...]
    a = a * jax.nn.sigmoid(a)
    a_hi = a.astype(BF16)
    a_lo = (a - a_hi.astype(F32)).astype(BF16)
    w = w_ref[...]
    w_hi = w.astype(BF16)
    w_lo = (w - w_hi.astype(F32)).astype(BF16)
    o_ref[...] = _dot(a_hi, w_hi) + _dot(a_lo, w_hi) + _dot(a_hi, w_lo) + b_ref[...]


def _modulation(cond, w_mod, b_mod):
    rows = cond.shape[0]
    n = w_mod.shape[1]
    blk = D_MODEL
    return pl.pallas_call(
        _mod_kernel,
        grid=(n // blk,),
        in_specs=[
            pl.BlockSpec((rows, D_MODEL), lambda j: (0, 0)),
            pl.BlockSpec((D_MODEL, blk), lambda j: (0, j)),
            pl.BlockSpec((1, blk), lambda j: (0, j)),
        ],
        out_specs=pl.BlockSpec((rows, blk), lambda j: (0, j)),
        out_shape=jax.ShapeDtypeStruct((rows, n), F32),
        compiler_params=_params("arbitrary"),
        name="modulation",
    )(cond, w_mod, b_mod.reshape(1, n))


def _rope128(t, c_tab, s_tab):
    return t * c_tab + pltpu.roll(t, 64, axis=1) * s_tab


def _inproj_kernel(x_ref, sh_ref, sc_ref, c_ref, s_ref, wcq_ref, wckv_ref, wsu_ref, wgm_ref, wgs_ref,
                   qg_ref, kvg_ref, wuq_ref, wukv_ref, *out_refs, is_ctx):
    if is_ctx:
        k_ref, v_ref, su_ref = out_refs
    else:
        q_ref, k_ref, v_ref, su_ref, gm_ref, gs_ref = out_refs
    xm = _ln_plain(x_ref[0]) * (1.0 + sc_ref[0]) + sh_ref[0]
    xb = xm.astype(BF16)
    c_tab = c_ref[...]
    s_tab = s_ref[...]

    ckvpe = _dot(xb, wckv_ref[...])
    ckv_n = _rms(ckvpe[:, :KV_RANK], kvg_ref[...]).astype(BF16)
    kpe = _rope128(ckvpe[:, KV_RANK:], c_tab, s_tab).astype(BF16)
    kv = _dot(ckv_n, wukv_ref[...])
    for h in range(N_HEADS):
        k_ref[0, h, :, 0:128] = kv[:, h * 256:h * 256 + 128].astype(BF16)
        k_ref[0, h, :, 128:256] = kpe
        v_ref[0, h] = kv[:, h * 256 + 128:(h + 1) * 256].astype(BF16)

    su_ref[...] = _dot(xb, wsu_ref[...])

    if not is_ctx:
        cq_n = _rms(_dot(xb, wcq_ref[...]), qg_ref[...]).astype(BF16)
        q = _dot(cq_n, wuq_ref[...])
        for h in range(N_HEADS):
            q_ref[0, h, :, 0:128] = (q[:, h * 256:h * 256 + 128] * ATT_SCALE).astype(BF16)
            q_ref[0, h, :, 128:256] = (_rope128(q[:, h * 256 + 128:(h + 1) * 256], c_tab, s_tab) * ATT_SCALE).astype(BF16)
        gm_ref[0] = jax.nn.sigmoid(_dot(xb, wgm_ref[...])).astype(BF16)
        gs_ref[0] = jax.nn.sigmoid(_dot(xb, wgs_ref[...])).astype(BF16)


def _inproj(x, shift, scale, c_tab, s_tab, wts, *, is_ctx, tm, b0, bsz):
    length = x.shape[1]
    mod_map = (lambda b, t: (0, 0, 0)) if shift.shape[0] == 1 else (lambda b, t: (b0 + b, 0, 0))
    const2 = lambda b, t: (0, 0)
    wcq, wckv, wsu, wgm, wgs, qg, kvg, wuq, wukv = wts
    in_specs = [
        pl.BlockSpec((1, tm, D_MODEL), lambda b, t: (b0 + b, t, 0)),
        pl.BlockSpec((1, 1, D_MODEL), mod_map),
        pl.BlockSpec((1, 1, D_MODEL), mod_map),
        pl.BlockSpec((tm, 128), lambda b, t: (t, 0)),
        pl.BlockSpec((tm, 128), lambda b, t: (t, 0)),
    ] + [pl.BlockSpec(w.shape, const2) for w in (wcq, wckv, wsu, wgm, wgs, qg, kvg, wuq, wukv)]
    head_spec = lambda width: pl.BlockSpec((1, N_HEADS, tm, width), lambda b, t: (b, 0, t, 0))
    row_spec = pl.BlockSpec((1, tm, D_MODEL), lambda b, t: (b, t, 0))
    su_spec = pl.BlockSpec((tm, S5_WIDTH), lambda b, t: (t, b))
    k_shape = jax.ShapeDtypeStruct((bsz, N_HEADS, length, HEAD_PAD), BF16)
    v_shape = jax.ShapeDtypeStruct((bsz, N_HEADS, length, V_HEAD_DIM), BF16)
    su_shape = jax.ShapeDtypeStruct((length, bsz * S5_WIDTH), F32)
    g_shape = jax.ShapeDtypeStruct((bsz, length, D_MODEL), BF16)
    if is_ctx:
        out_specs = [head_spec(HEAD_PAD), head_spec(V_HEAD_DIM), su_spec]
        out_shape = [k_shape, v_shape, su_shape]
    else:
        out_specs = [head_spec(HEAD_PAD), head_spec(HEAD_PAD), head_spec(V_HEAD_DIM), su_spec, row_spec, row_spec]
        out_shape = [k_shape, k_shape, v_shape, su_shape, g_shape, g_shape]
    return pl.pallas_call(
        functools.partial(_inproj_kernel, is_ctx=is_ctx),
        grid=(bsz, length // tm),
        in_specs=in_specs,
        out_specs=out_specs,
        out_shape=out_shape,
        compiler_params=_params("parallel", "parallel"),
        name="inproj_ctx" if is_ctx else "inproj_lat",
    )(x, shift, scale, c_tab, s_tab, wcq, wckv, wsu, wgm, wgs, qg, kvg, wuq, wukv)


def _attn_kernel(q_ref, kc_ref, vc_ref, kl_ref, vl_ref, o_ref):
    q = q_ref[0, 0]
    s_c = _dot_nt(q, kc_ref[0, 0])
    s_l = _dot_nt(q, kl_ref[0, 0])
    m = jnp.maximum(jnp.max(s_c, axis=-1, keepdims=True), jnp.max(s_l, axis=-1, keepdims=True))
    p_c = jnp.exp(s_c - m)
    p_l = jnp.exp(s_l - m)
    denom = jnp.sum(p_c, axis=-1, keepdims=True) + jnp.sum(p_l, axis=-1, keepdims=True)
    o = _dot(p_c.astype(BF16), vc_ref[0, 0]) + _dot(p_l.astype(BF16), vl_ref[0, 0])
    o_ref[0] = (o / denom).astype(BF16)


def _attention(q, k_ctx, v_ctx, k_lat, v_lat, *, tq, b0, bsz):
    _, heads, length, _ = q.shape
    l_ctx = k_ctx.shape[2]
    full = lambda b, h, i: (b0 + b, h, 0, 0)
    return pl.pallas_call(
        _attn_kernel,
        grid=(bsz, heads, length // tq),
        in_specs=[
            pl.BlockSpec((1, 1, tq, HEAD_PAD), lambda b, h, i: (b0 + b, h, i, 0)),
            pl.BlockSpec((1, 1, l_ctx, HEAD_PAD), full),
            pl.BlockSpec((1, 1, l_ctx, V_HEAD_DIM), full),
            pl.BlockSpec((1, 1, length, HEAD_PAD), full),
            pl.BlockSpec((1, 1, length, V_HEAD_DIM), full),
        ],
        out_specs=pl.BlockSpec((1, tq, V_HEAD_DIM), lambda b, h, i: (b, i, h)),
        out_shape=jax.ShapeDtypeStruct((bsz, length, heads * V_HEAD_DIM), BF16),
        compiler_params=_params("parallel", "parallel", "arbitrary"),
        name="attention",
    )(q, k_ctx, v_ctx, k_lat, v_lat)


def _s5_kernel(u_ref, are_ref, aim_ref, bre_ref, bim_ref, cre_ref, cim_ref, y_ref,
               bu_re, bu_im, h_re, h_im, *, tt, bsz, n_ctx_tiles):
    d = pl.program_id(0)
    i = pl.program_id(1)

    @pl.when(i == 0)
    def _():
        h_re[...] = jnp.zeros_like(h_re)
        h_im[...] = jnp.zeros_like(h_im)

    u = u_ref[...].astype(BF16)
    half = S5_COLS // S5_CHUNKS
    cw = S5_WIDTH // S5_CHUNKS
    for c in range(S5_CHUNKS):
        uc = u[:, c * cw:(c + 1) * cw]
        bu_re[:, c * half:(c + 1) * half] = _dot(uc, bre_ref[0, c])
        bu_im[:, c * half:(c + 1) * half] = _dot(uc, bim_ref[0, c])

    col_w = 512
    for cc in range(S5_COLS // col_w):
        cols = slice(cc * col_w, (cc + 1) * col_w)
        a_r = are_ref[0, :, cols]
        a_i = aim_ref[0, :, cols]

        def step(t, carry, cols=cols, a_r=a_r, a_i=a_i):
            hr, hi = carry
            pos = jnp.where(d == 0, t, tt - 1 - t)
            r = pl.multiple_of(pos * bsz, bsz)
            nr = a_r * hr - a_i * hi + bu_re[pl.ds(r, bsz), cols]
            ni = a_r * hi + a_i * hr + bu_im[pl.ds(r, bsz), cols]
            bu_re[pl.ds(r, bsz), cols] = nr
            bu_im[pl.ds(r, bsz), cols] = ni
            return nr, ni

        hr, hi = lax.fori_loop(0, tt, step, (h_re[:, cols], h_im[:, cols]))
        h_re[:, cols] = hr
        h_im[:, cols] = hi

    @pl.when(i >= n_ctx_tiles)
    def _():
        for c in range(S5_CHUNKS):
            sl = slice(c * half, (c + 1) * half)
            y = _dot(bu_re[:, sl].astype(BF16), cre_ref[0, c]) + _dot(bu_im[:, sl].astype(BF16), cim_ref[0, c])
            y_ref[0, :, c * cw:(c + 1) * cw] = y


def _s5_scan(u_all, mats, *, bsz, l_ctx, tt):
    a_re_b, a_im_b, b_re_bd, b_im_bd, c_re_bd, c_im_bd = mats
    rows = tt * bsz
    n_tiles = u_all.shape[0] // rows
    n_ctx_tiles = l_ctx // tt
    n_lat_tiles = n_tiles - n_ctx_tiles

    def u_map(d, i):
        bwd = jnp.where(i < n_ctx_tiles, n_ctx_tiles - 1 - i, n_tiles - 1 - (i - n_ctx_tiles))
        return (jnp.where(d == 0, i, bwd), 0)

    def y_map(d, i):
        j = jnp.maximum(i - n_ctx_tiles, 0)
        return (d, jnp.where(d == 0, j, n_lat_tiles - 1 - j), 0)

    dmap3 = lambda d, i: (d, 0, 0)
    dmap4 = lambda d, i: (d, 0, 0, 0)
    return pl.pallas_call(
        functools.partial(_s5_kernel, tt=tt, bsz=bsz, n_ctx_tiles=n_ctx_tiles),
        grid=(2, n_tiles),
        in_specs=[
            pl.BlockSpec((rows, S5_WIDTH), u_map),
            pl.BlockSpec((1,) + a_re_b.shape[1:], dmap3),
            pl.BlockSpec((1,) + a_im_b.shape[1:], dmap3),
            pl.BlockSpec((1,) + b_re_bd.shape[1:], dmap4),
            pl.BlockSpec((1,) + b_im_bd.shape[1:], dmap4),
            pl.BlockSpec((1,) + c_re_bd.shape[1:], dmap4),
            pl.BlockSpec((1,) + c_im_bd.shape[1:], dmap4),
        ],
        out_specs=pl.BlockSpec((1, rows, S5_WIDTH), y_map),
        out_shape=jax.ShapeDtypeStruct((2, n_lat_tiles * rows, S5_WIDTH), F32),
        scratch_shapes=[
            pltpu.VMEM((rows, S5_COLS), F32),
            pltpu.VMEM((rows, S5_COLS), F32),
            pltpu.VMEM((bsz, S5_COLS), F32),
            pltpu.VMEM((bsz, S5_COLS), F32),
        ],
        compiler_params=_params("arbitrary", "arbitrary"),
        name="s5_scan",
    )(u_all, a_re_b, a_im_b, b_re_bd, b_im_bd, c_re_bd, c_im_bd)


def _s5_matrices(a_re, a_im, log_dt, b_re, b_im, c_re, c_im, bsz):
    dt = jnp.exp(log_dt)[..., None]
    mag = jnp.exp(a_re * dt)
    ab_re = mag * jnp.cos(a_im * dt)
    ab_im = mag * jnp.sin(a_im * dt)
    den = a_re * a_re + a_im * a_im
    f_re = ((ab_re - 1.0) * a_re + ab_im * a_im) / den
    f_im = (ab_im * a_re - (ab_re - 1.0) * a_im) / den
    bb_re = f_re[..., None] * b_re - f_im[..., None] * b_im
    bb_im = f_re[..., None] * b_im + f_im[..., None] * b_re
    gl = S5_GROUPS // S5_CHUNKS
    eye = jnp.eye(gl, dtype=F32)

    def in_bd(bb):
        t = bb.reshape(2, S5_CHUNKS, gl, S5_STATE, S5_GROUP)
        t = jnp.einsum('dkgpc,gh->dkgchp', t, eye)
        return t.reshape(2, S5_CHUNKS, gl * S5_GROUP, gl * S5_STATE).astype(BF16)

    def out_bd(cc):
        t = cc.reshape(2, S5_CHUNKS, gl, S5_GROUP, S5_STATE)
        t = jnp.einsum('dkgcp,gh->dkgphc', t, eye)
        return t.reshape(2, S5_CHUNKS, gl * S5_STATE, gl * S5_GROUP).astype(BF16)

    bcast = lambda a: jnp.broadcast_to(a.reshape(2, 1, S5_COLS), (2, bsz, S5_COLS))
    return bcast(ab_re), bcast(ab_im), in_bd(bb_re), in_bd(bb_im), out_bd(c_re), out_bd(-c_im)


def _merge_kernel(x_ref, att_ref, gm_ref, gs_ref, su_ref, y_ref, dsk_ref, wglu_ref, wout_ref, ga_ref,
                  l1g_ref, l1b_ref, shf_ref, scf_ref, wq_ref, keys_ref, h1_ref, xm_ref, st_ref):
    y = su_ref[...] * dsk_ref[...] + y_ref[0] + y_ref[1]
    gl = _dot(_gelu(y).astype(BF16), wglu_ref[...])
    s5_out = gl[:, :D_MODEL] * jax.nn.sigmoid(gl[:, D_MODEL:])
    merged = gm_ref[0].astype(F32) * att_ref[0].astype(F32) + gs_ref[0].astype(F32) * s5_out
    out = _dot(merged.astype(BF16), wout_ref[...])
    h1 = _ln_plain(DEEPNORM_ALPHA * x_ref[0] + ga_ref[0] * out) * l1g_ref[...] + l1b_ref[...]
    h1_ref[0] = h1
    xm = _ln_plain(h1) * (1.0 + scf_ref[0]) + shf_ref[0]
    xm_ref[0] = xm
    qp = _dot(xm.astype(BF16), wq_ref[...]).astype(BF16)
    for j in range(2 * PEER_HEADS):
        st_ref[j] = _dot_nt(keys_ref[j], qp[:, j * PEER_HALF:(j + 1) * PEER_HALF])


def _merge(x, att, gm, gs, su_all2d, y2d, dsk, wglu, wout, g_a, l1g, l1b, sh_f, sc_f, wq, keys, *, tm, ctx_tiles, b0, h0):
    bsz, length, _ = att.shape
    nt = length // tm
    row = pl.BlockSpec((1, tm, D_MODEL), lambda b, t: (b, t, 0))
    xrow = pl.BlockSpec((1, tm, D_MODEL), lambda b, t: (b0 + b, t, 0))
    hrow = pl.BlockSpec((1, tm, D_MODEL), lambda b, t: (h0 + b, t, 0))
    modb = pl.BlockSpec((1, 1, D_MODEL), lambda b, t: (b0 + b, 0, 0))
    const = lambda a: pl.BlockSpec(a.shape, lambda b, t: (0,) * a.ndim)
    return pl.pallas_call(
        _merge_kernel,
        grid=(bsz, nt),
        in_specs=[
            xrow, row, hrow, hrow,
            pl.BlockSpec((tm, S5_WIDTH), lambda b, t: (ctx_tiles + t, h0 + b)),
            pl.BlockSpec((2, tm, S5_WIDTH), lambda b, t: (0, t, h0 + b)),
            const(dsk), const(wglu), const(wout), modb, const(l1g), const(l1b), modb, modb, const(wq), const(keys),
        ],
        out_specs=[row, row, pl.BlockSpec((2 * PEER_HEADS, PEER_N_KEYS, tm), lambda b, t: (0, 0, b * nt + t))],
        out_shape=[
            jax.ShapeDtypeStruct((bsz, length, D_MODEL), F32),
            jax.ShapeDtypeStruct((bsz, length, D_MODEL), F32),
            jax.ShapeDtypeStruct((2 * PEER_HEADS, PEER_N_KEYS, bsz * length), F32),
        ],
        compiler_params=_params("parallel", "parallel"),
        name="merge_peer_query",
    )(x, att, gm, gs, su_all2d, y2d, dsk, wglu, wout, g_a, l1g, l1b, sh_f, sc_f, wq, keys)


def _take_top(vals, codes, payload, k):
    rows = lax.broadcasted_iota(jnp.int32, (k, vals.shape[1]), 0)
    top_v = jnp.zeros((k, vals.shape[1]), F32)
    top_p = jnp.zeros((k, vals.shape[1]), F32)
    for r in range(k):
        m = jnp.max(vals, axis=0, keepdims=True)
        cm = jnp.min(jnp.where(vals == m, codes, CODE_NONE), axis=0, keepdims=True)
        sel = codes == cm
        if payload is None:
            p = cm
        else:
            p = jnp.max(jnp.where(sel, payload, -1.0), axis=0, keepdims=True)
        vals = jnp.where(sel, -jnp.inf, vals)
        top_v = jnp.where(rows == r, m, top_v)
        top_p = jnp.where(rows == r, p, top_p)
    return top_v, top_p


def _topk_kernel(st_ref, e_ref, g_ref):
    lanes = st_ref.shape[2]
    key_idx = lax.broadcasted_iota(jnp.int32, (PEER_N_KEYS, lanes), 0).astype(F32)
    sub_idx = lax.broadcasted_iota(jnp.int32, (PEER_TOPK, lanes), 0).astype(F32)
    half_k = PEER_TOPK // 2
    for h in range(PEER_HEADS):
        sv0, si0 = _take_top(st_ref[2 * h], key_idx, None, PEER_TOPK)
        sv1, si1 = _take_top(st_ref[2 * h + 1], key_idx, None, PEER_TOPK)
        cs, ce, cc = [], [], []
        for i in range(half_k):
            nj = PEER_TOPK if i == 0 else half_k
            cs.append(sv0[i:i + 1, :] + sv1[0:nj, :])
            ce.append(si0[i:i + 1, :] * PEER_N_KEYS + si1[0:nj, :])
            cc.append(sub_idx[0:nj, :] + i * PEER_TOPK)
        cs.append(sv0[half_k:, :] + sv1[0:1, :])
        ce.append(si0[half_k:, :] * PEER_N_KEYS + si1[0:1, :])
        cc.append((sub_idx[0:half_k, :] + half_k) * PEER_TOPK)
        top_s, top_e = _take_top(jnp.concatenate(cs, axis=0), jnp.concatenate(cc, axis=0),
                                 jnp.concatenate(ce, axis=0), PEER_TOPK)
        p = jnp.exp(top_s - jnp.max(top_s, axis=0, keepdims=True))
        g_ref[h] = p / jnp.sum(p, axis=0, keepdims=True)
        e_ref[h] = top_e.astype(jnp.int32)


def _peer_topk(st, *, lanes):
    n = st.shape[2]
    out_spec = pl.BlockSpec((PEER_HEADS, PEER_TOPK, lanes), lambda i: (0, 0, i))
    return pl.pallas_call(
        _topk_kernel,
        grid=(n // lanes,),
        in_specs=[pl.BlockSpec((2 * PEER_HEADS, PEER_N_KEYS, lanes), lambda i: (0, 0, i))],
        out_specs=[out_spec, out_spec],
        out_shape=[jax.ShapeDtypeStruct((PEER_HEADS, PEER_TOPK, n), jnp.int32),
                   jax.ShapeDtypeStruct((PEER_HEADS, PEER_TOPK, n), F32)],
        compiler_params=_params("parallel"),
        name="peer_topk",
    )(st)


def _sc_peer(u_tab, v_tab, idx, x, gates):
    n_tok = x.shape[0]
    per_worker = n_tok // SC_WORKERS
    assert per_worker % 2 == 0 and per_worker >= 4
    assert len(STAGE_STEPS) == (PEER_PAIRS - SC_PAIRS) // SC_LANES and len(SUM_STEPS) == SC_PAIRS // SC_LANES
    n_win = PEER_PAIRS // DOT_WIN
    steps = PEER_PAIRS // SC_LANES
    per_win = DOT_WIN // SC_LANES
    n_vc = SC_PAIRS // SC_LANES
    n_vs = (PEER_PAIRS - SC_PAIRS) // SC_LANES
    tc_pairs = PEER_PAIRS - SC_PAIRS
    chunks = HALF_W // SC_LANES
    mesh = plsc.VectorSubcoreMesh(core_axis_name="c", subcore_axis_name="s")

    @functools.partial(
        pl.kernel, mesh=mesh,
        out_type=(jax.ShapeDtypeStruct((n_tok * PEER_PAIRS,), F32),
                  jax.ShapeDtypeStruct((n_tok * tc_pairs, HALF_W), jnp.int32),
                  jax.ShapeDtypeStruct((n_tok, 2 * HALF_W), F32)),
        scratch_types=[
            pltpu.VMEM((2, PEER_PAIRS), jnp.int32),
            pltpu.VMEM((2, PEER_PAIRS), jnp.int32),
            pltpu.VMEM((2, 2 * HALF_W), F32),
            pltpu.VMEM((2, SC_PAIRS), F32),
            pltpu.VMEM((2, SC_PAIRS), F32),
            pltpu.VMEM((2, PEER_PAIRS), F32),
            pltpu.VMEM((2, 2 * HALF_W), F32),
            pltpu.VMEM((n_win, DOT_WIN, HALF_W), jnp.int32),
            pltpu.VMEM((n_vc, SC_LANES, HALF_W), jnp.int32),
            pltpu.VMEM((2, SC_LANES, HALF_W), jnp.int32),
            pltpu.SemaphoreType.DMA((n_win,)),
            pltpu.SemaphoreType.DMA((n_vc,)),
            pltpu.SemaphoreType.DMA((2,)),
            pltpu.SemaphoreType.DMA((2,)),
            pltpu.SemaphoreType.DMA((2,)),
            pltpu.SemaphoreType.DMA((2,)),
            pltpu.SemaphoreType.DMA((2,)),
            pltpu.SemaphoreType.DMA((2,)),
            pltpu.SemaphoreType.DMA((2,)),
        ],
        compiler_params=pltpu.CompilerParams(needs_layout_passes=False),
    )
    def peer(u_hbm, v_hbm, idx_hbm, x_hbm, gate_hbm, act_hbm, vrows_hbm, f_hbm,
             idx_v, vidx_v, x_v, gate_v, coef_v, act_v, f_v, urows_v, vsum_v, vstage_v,
             u_sems, vc_sems, vg_sems, vw_sems, idx_sems, x_sems, gate_sems, act_sems, f_sems):
        wid = lax.axis_index("s") * 2 + lax.axis_index("c")
        base = wid * per_worker
        last = base + per_worker - 1
        lane = lax.iota(jnp.int32, SC_LANES)

        def pair_off(tok):
            return pl.multiple_of(tok * PEER_PAIRS, PEER_PAIRS)

        def idx_copy(tok, slot):
            return pltpu.make_async_copy(idx_hbm.at[pl.ds(pair_off(tok), PEER_PAIRS)], idx_v.at[slot], idx_sems.at[slot])

        def x_copy(tok, slot):
            return pltpu.make_async_copy(x_hbm.at[tok], x_v.at[slot], x_sems.at[slot])

        def gate_copy(tok, slot):
            return pltpu.make_async_copy(gate_hbm.at[pl.ds(pair_off(tok), SC_PAIRS)], gate_v.at[slot], gate_sems.at[slot])

        def u_gather(slot, w):
            return pltpu.make_async_copy(u_hbm.at[idx_v.at[slot, pl.ds(w * DOT_WIN, DOT_WIN)]], urows_v.at[w],
                                         u_sems.at[w])

        def vsum_gather(slot, j):
            return pltpu.make_async_copy(v_hbm.at[vidx_v.at[slot, pl.ds(j * SC_LANES, SC_LANES)]], vsum_v.at[j],
                                         vc_sems.at[j])

        def stage_buf(slot, s):
            return (slot * n_vs + s) % 2

        def vstage_gather(slot, s):
            b = stage_buf(slot, s)
            return pltpu.make_async_copy(v_hbm.at[vidx_v.at[slot, pl.ds(SC_PAIRS + s * SC_LANES, SC_LANES)]],
                                         vstage_v.at[b], vg_sems.at[b])

        def vstage_write(tok, s, slot):
            b = stage_buf(slot, s)
            row0 = pl.multiple_of(tok * tc_pairs + s * SC_LANES, SC_LANES)
            return pltpu.make_async_copy(vstage_v.at[b], vrows_hbm.at[pl.ds(row0, SC_LANES)], vw_sems.at[b])

        def act_copy(tok, slot):
            return pltpu.make_async_copy(act_v.at[slot], act_hbm.at[pl.ds(pair_off(tok), PEER_PAIRS)], act_sems.at[slot])

        def f_copy(tok, slot):
            return pltpu.make_async_copy(f_v.at[slot], f_hbm.at[tok], f_sems.at[slot])

        def unpack(wv):
            return (lax.bitcast_convert_type(wv << 16, F32), lax.bitcast_convert_type(wv & jnp.int32(-65536), F32))

        def step_dots(slot, g):
            w, sub = g // per_win, g % per_win

            def body(c, accs):
                xlo = x_v[slot, pl.ds(c * SC_LANES, SC_LANES)]
                xhi = x_v[slot, pl.ds(HALF_W + c * SC_LANES, SC_LANES)]
                new = []
                for r in range(SC_LANES):
                    lo, hi = unpack(urows_v[w, sub * SC_LANES + r, pl.ds(c * SC_LANES, SC_LANES)])
                    new.append(accs[r] + lo * xlo + hi * xhi)
                return tuple(new)

            accs = lax.fori_loop(0, chunks, body, tuple(jnp.zeros((SC_LANES,), F32) for _ in range(SC_LANES)))
            v = jnp.zeros((SC_LANES,), F32)
            for r in range(SC_LANES):
                v = jnp.where(lane == r, jnp.sum(accs[r]), v)
            act_v[slot, pl.ds(g * SC_LANES, SC_LANES)] = v

        def window_sum(fslot, j):
            cvec = coef_v[fslot, pl.ds(j * SC_LANES, SC_LANES)]
            cb = [jnp.zeros((SC_LANES,), F32) + jnp.sum(jnp.where(lane == r, cvec, 0.0)) for r in range(SC_LANES)]

            def body(c, carry):
                alo = f_v[fslot, pl.ds(c * SC_LANES, SC_LANES)]
                ahi = f_v[fslot, pl.ds(HALF_W + c * SC_LANES, SC_LANES)]
                for r in range(SC_LANES):
                    lo, hi = unpack(vsum_v[j, r, pl.ds(c * SC_LANES, SC_LANES)])
                    alo = alo + cb[r] * lo
                    ahi = ahi + cb[r] * hi
                f_v[fslot, pl.ds(c * SC_LANES, SC_LANES)] = alo
                f_v[fslot, pl.ds(HALF_W + c * SC_LANES, SC_LANES)] = ahi
                return carry

            lax.fori_loop(0, chunks, body, 0)

        def token_coefs(slot):
            for q in range(n_vc):
                a = act_v[slot, pl.ds(q * SC_LANES, SC_LANES)]
                z2 = 1.5957691216057308 * (a + 0.044715 * (a * a * a))
                coef_v[slot, pl.ds(q * SC_LANES, SC_LANES)] = gate_v[slot, pl.ds(q * SC_LANES, SC_LANES)] * a / (1.0 + jnp.exp(-z2))

        def zero_f(fslot):
            for q in range(2 * HALF_W // SC_LANES):
                f_v[fslot, pl.ds(q * SC_LANES, SC_LANES)] = jnp.zeros((SC_LANES,), F32)

        idx_copy(base, 0).start()
        x_copy(base, 0).start()
        gate_copy(base, 0).start()
        idx_copy(base, 0).wait()
        for w in range(n_win):
            u_gather(0, w).start()

        @pl.loop(0, per_worker, step=2)
        def _(i):
            for slot in range(2):
                tok = base + i + slot
                nxt = jnp.minimum(tok + 1, last)
                other = 1 - slot
                first = slot == 0
                idx_copy(nxt, other).start()
                x_copy(nxt, other).start()
                gate_copy(nxt, other).start()
                for q in range(PEER_PAIRS // SC_LANES):
                    vidx_v[slot, pl.ds(q * SC_LANES, SC_LANES)] = idx_v[slot, pl.ds(q * SC_LANES, SC_LANES)]
                x_copy(tok, slot).wait()
                gate_copy(tok, slot).wait()

                @pl.when(i > 0)
                def _():
                    act_copy(tok, slot).wait()

                if first:
                    @pl.when(i > 2)
                    def _():
                        f_copy(tok, other).wait()
                else:
                    @pl.when(i > 0)
                    def _():
                        f_copy(tok, other).wait()
                zero_f(other)

                idx_copy(nxt, other).wait()
                for g in range(steps):
                    s = STAGE_STEPS.index(g) if g in STAGE_STEPS else None
                    if s is not None:
                        if first and s < 2:
                            @pl.when(i > 0)
                            def _(s=s, tok=tok, slot=slot):
                                vstage_write(tok, s, slot).wait()
                        else:
                            vstage_write(tok, s, slot).wait()
                        vstage_gather(slot, s).start()
                    if g % per_win == 0:
                        u_gather(slot, g // per_win).wait()
                    step_dots(slot, g)
                    if g % per_win == per_win - 1:
                        u_gather(other, g // per_win).start()
                    if s is not None:
                        sp = (s - 1) % n_vs
                        tokp, slotp = (tok, slot) if s >= 1 else (tok - 1, other)

                        def finish(sp=sp, tokp=tokp, slotp=slotp):
                            vstage_gather(slotp, sp).wait()
                            vstage_write(tokp, sp, slotp).start()

                        if first and s < 1:
                            pl.when(i > 0)(finish)
                        else:
                            finish()
                    if g in SUM_STEPS:
                        j = SUM_STEPS.index(g)

                        def consume(j=j, other=other):
                            vsum_gather(other, j).wait()
                            window_sum(other, j)

                        if first:
                            pl.when(i > 0)(consume)
                        else:
                            consume()
                        vsum_gather(slot, j).start()
                token_coefs(slot)
                act_copy(tok, slot).start()

                def send_prev(tok=tok, other=other):
                    f_copy(tok - 1, other).start()

                if first:
                    pl.when(i > 0)(send_prev)
                else:
                    send_prev()

        vstage_gather(1, n_vs - 1).wait()
        vstage_write(last, n_vs - 1, 1).start()
        f_copy(last, 1).wait()
        zero_f(1)
        for j in range(n_vc):
            vsum_gather(1, j).wait()
            window_sum(1, j)
        f_copy(last, 1).start()
        for s in (n_vs - 2, n_vs - 1):
            vstage_write(last, s, 1).wait()
        for w in range(n_win):
            u_gather(0, w).wait()
        x_copy(last, 0).wait()
        gate_copy(last, 0).wait()
        for slot in range(2):
            f_copy(last, slot).wait()
            act_copy(last, slot).wait()

    return peer(u_tab, v_tab, idx, x, gates)


def _peer_kernel(vg_ref, act_ref, w_ref, fsc_ref, h1_ref, gf_ref, l2g_ref, l2b_ref, o_ref, f_ref, *, tb):
    sub = 8
    span = sub * TC_PAIRS
    half = D_MODEL // 2
    even = (lax.broadcasted_iota(jnp.int32, (sub, 2 * span), 1) % 2) == 0
    for s in range(tb // sub):
        rows = slice(s * span, (s + 1) * span)
        toks = slice(s * sub, (s + 1) * sub)
        coef = w_ref[toks, :] * _gelu(act_ref[toks, :])
        c2 = jnp.concatenate([jnp.where(even, coef, 0.0), jnp.where(even, 0.0, coef)], axis=0).astype(BF16)
        f2 = _dot(c2, pltpu.bitcast(vg_ref[rows, :], BF16))
        f_ref[toks, :half] = f2[:sub]
        f_ref[toks, half:] = f2[sub:]
    r = DEEPNORM_ALPHA * h1_ref[...] + gf_ref[0] * (f_ref[...] + fsc_ref[...])
    o_ref[...] = _ln_plain(r) * l2g_ref[...] + l2b_ref[...]


def _peer_eval(vg, act2, w2, f_sc, h1, g_f, l2g, l2b, *, tb, tok0, length):
    n = h1.shape[0]
    tok_blk0 = tok0 // tb
    per_b = length // tb
    tok_spec = lambda width: pl.BlockSpec((tb, width), lambda i: (i, 0))
    return pl.pallas_call(
        functools.partial(_peer_kernel, tb=tb),
        grid=(n // tb,),
        in_specs=[
            pl.BlockSpec((tb * TC_PAIRS, D_MODEL // 2), lambda i: (i, 0)),
            tok_spec(16 * TC_PAIRS), tok_spec(16 * TC_PAIRS), tok_spec(D_MODEL), tok_spec(D_MODEL),
            pl.BlockSpec((1, 1, D_MODEL), lambda i: ((tok_blk0 + i) // per_b, 0, 0)),
            pl.BlockSpec((1, D_MODEL), lambda i: (0, 0)),
            pl.BlockSpec((1, D_MODEL), lambda i: (0, 0)),
        ],
        out_specs=tok_spec(D_MODEL),
        out_shape=jax.ShapeDtypeStruct((n, D_MODEL), F32),
        scratch_shapes=[pltpu.VMEM((tb, D_MODEL), F32)],
        compiler_params=_params("parallel"),
        name="peer_eval",
    )(vg, act2, w2, f_sc, h1, g_f, l2g, l2b)


def _rope_perm():
    half = ROPE_AXIS_DIM // 2
    base = jnp.arange(QK_ROPE_DIM)
    return jnp.where((base % ROPE_AXIS_DIM) < half, base + half, base - half)


def _rope_tables(length):
    pos = jnp.arange(length)
    row = (pos // GRID_W).astype(F32)
    col = (pos % GRID_W).astype(F32)
    inv_freq = jnp.power(ROPE_BASE, -jnp.arange(0, ROPE_AXIS_DIM, 2, dtype=F32) / ROPE_AXIS_DIM)
    ang_r = row[:, None] * inv_freq
    ang_c = col[:, None] * inv_freq
    zeros = jnp.zeros((length, QK_ROPE_DIM), F32)
    c_tab = jnp.concatenate([jnp.cos(ang_r), jnp.cos(ang_r), jnp.cos(ang_c), jnp.cos(ang_c), zeros], axis=1)
    s_tab = jnp.concatenate([-jnp.sin(ang_r), jnp.sin(ang_r), -jnp.sin(ang_c), jnp.sin(ang_c), zeros], axis=1)
    return c_tab, s_tab


def _identity_tables(length):
    ones = jnp.ones((length, QK_ROPE_DIM), F32)
    zeros = jnp.zeros((length, QK_ROPE_DIM), F32)
    return jnp.concatenate([ones, zeros], axis=1), jnp.zeros((length, 2 * QK_ROPE_DIM), F32)


def _block_diag8(a):
    n, k = a.shape
    own = (jnp.arange(n) % 8)[:, None] == jnp.arange(8)[None, :]
    return jnp.where(own[:, :, None], jnp.repeat(a, 2, axis=1)[:, None, :], 0.0).reshape(n, 16 * k)


def _pack_rows(table):
    bits = lax.bitcast_convert_type(table.astype(BF16), jnp.uint16).astype(jnp.uint32)
    return lax.bitcast_convert_type(bits[:, :HALF_W] | (bits[:, HALF_W:] << 16), jnp.int32)


def _layer_weights(w_in, q_norm_g, kv_norm_g, w_uq, w_ukv):
    perm = _rope_perm()
    p0 = Q_RANK
    p1 = p0 + KV_RANK
    p2 = p1 + QK_ROPE_DIM
    p3 = p2 + S5_WIDTH
    p4 = p3 + D_MODEL
    w_kpe = w_in[:, p1:p2]
    wckv = jnp.concatenate([w_in[:, p0:p1], w_kpe, w_kpe[:, perm]], axis=1)
    uq = w_uq.reshape(Q_RANK, N_HEADS, QK_NOPE_DIM + QK_ROPE_DIM)
    pe = uq[:, :, QK_NOPE_DIM:]
    wuq = jnp.concatenate([uq[:, :, :QK_NOPE_DIM], pe, pe[:, :, perm]], axis=2).reshape(Q_RANK, N_HEADS * HEAD_PAD)
    cast = lambda a: a.astype(BF16)
    return (cast(w_in[:, :p0]), cast(wckv), cast(w_in[:, p2:p3]), cast(w_in[:, p3:p4]), cast(w_in[:, p4:]),
            q_norm_g.reshape(1, Q_RANK), kv_norm_g.reshape(1, KV_RANK), cast(wuq), cast(w_ukv))


def kernel(x, c, ctx, c_ctx, w_mod, b_mod, w_in, q_norm_g, kv_norm_g, w_uq, w_ukv, s5_a_re, s5_a_im, s5_log_dt, s5_b_re, s5_b_im, s5_c_re, s5_c_im, s5_d, w_glu, w_out, ln1_g, ln1_b, peer_wq, peer_keys, peer_u, peer_v, ln2_g, ln2_b):
    bsz, length, _ = x.shape
    l_ctx = ctx.shape[1]
    n_tok = bsz * length
    layer = 0
    tm = 256
    s5_tt = 64
    peer_tb = 32

    cond = jnp.concatenate([c, c_ctx[None, :], jnp.zeros((7, D_MODEL), F32)], axis=0)
    mod = _modulation(cond, w_mod[layer], b_mod[layer]).reshape(cond.shape[0], N_MOD, 1, D_MODEL)
    sh_a, sc_a, g_a, sh_f, sc_f, g_f = (mod[:bsz, i] for i in range(N_MOD))
    csh_a, csc_a = mod[bsz:bsz + 1, 0], mod[bsz:bsz + 1, 1]

    wts = _layer_weights(w_in[layer], q_norm_g[layer], kv_norm_g[layer], w_uq[layer], w_ukv[layer])
    c_lat, s_lat = _rope_tables(length)
    c_id, s_id = _identity_tables(l_ctx)
    keys = peer_keys[layer].reshape(2 * PEER_HEADS, PEER_N_KEYS, PEER_HALF).astype(BF16)
    wglu, wout, wq = w_glu[layer].astype(BF16), w_out[layer].astype(BF16), peer_wq[layer].astype(BF16)
    u_pack = _pack_rows(peer_u[layer])
    v_pack = _pack_rows(peer_v[layer])
    l2g = ln2_g[layer].reshape(1, D_MODEL)
    l2b = ln2_b[layer].reshape(1, D_MODEL)

    n_groups = PEER_GROUPS if bsz % PEER_GROUPS == 0 else 1
    halves = 2 if (n_groups % 2 == 0 and bsz % 16 == 0) else 1
    per_half = n_groups // halves
    hb = bsz // halves
    gb = bsz // n_groups
    g_tok = gb * length
    mats = _s5_matrices(s5_a_re[layer], s5_a_im[layer], s5_log_dt[layer], s5_b_re[layer], s5_b_im[layer],
                        s5_c_re[layer], s5_c_im[layer], hb)
    def half_stages(h, ready):
        x_h, ctx_h, _ = lax.optimization_barrier((x, ctx, ready))
        k_ctx, v_ctx, su_ctx = _inproj(ctx_h, csh_a, csc_a, c_id, s_id, wts, is_ctx=True, tm=min(tm, l_ctx), b0=h * hb, bsz=hb)
        q, k_lat, v_lat, su_lat, gm, gs = _inproj(x_h, sh_a, sc_a, c_lat, s_lat, wts, is_ctx=False, tm=tm, b0=h * hb, bsz=hb)
        su_all = jnp.concatenate([su_ctx, su_lat], axis=0)
        y = _s5_scan(su_all.reshape(-1, S5_WIDTH), mats, bsz=hb, l_ctx=l_ctx, tt=s5_tt)
        return k_ctx, v_ctx, q, k_lat, v_lat, gm, gs, su_all, y.reshape(2, length, hb * S5_WIDTH)

    outs = []
    ready = (u_pack, v_pack)
    nxt_half = half_stages(0, ready)
    for g in range(n_groups):
        b0 = g * gb
        h0 = (g % per_half) * gb
        if g % per_half == 0:
            k_ctx, v_ctx, q, k_lat, v_lat, gm, gs, su_all, y2d = nxt_half
        q_g, _ = lax.optimization_barrier((q, ready))
        att = _attention(q_g, k_ctx, v_ctx, k_lat, v_lat, tq=min(512, length), b0=h0, bsz=gb)
        h1, xm, st = _merge(x, att, gm, gs, su_all, y2d, s5_d[layer].reshape(1, S5_WIDTH), wglu, wout, g_a,
                            ln1_g[layer].reshape(1, D_MODEL), ln1_b[layer].reshape(1, D_MODEL), sh_f, sc_f, wq, keys,
                            tm=tm, ctx_tiles=l_ctx // tm, b0=b0, h0=h0)
        experts, gates = _peer_topk(st, lanes=128)
        idx = experts.transpose(2, 0, 1).reshape(g_tok * PEER_PAIRS)
        gate = gates.transpose(2, 0, 1).reshape(g_tok, PEER_PAIRS)
        xmf = xm.reshape(g_tok, D_MODEL)
        acts, vg, f_sc = _sc_peer(u_pack, v_pack, idx, xmf, gate.reshape(g_tok * PEER_PAIRS))
        act2 = _block_diag8(acts.reshape(g_tok, PEER_PAIRS)[:, SC_PAIRS:])
        w2 = _block_diag8(gate[:, SC_PAIRS:])
        if g >= 1:
            w2, _ = lax.optimization_barrier((w2, outs[g - 1]))
        outs.append(_peer_eval(vg, act2, w2, f_sc, h1.reshape(g_tok, D_MODEL), g_f, l2g, l2b,
                               tb=peer_tb, tok0=g * g_tok, length=length))
        ready = (idx, gate, xmf) + ((outs[g - 1],) if g >= 1 else ())
        if g % per_half == max(per_half - 3, 0) and g // per_half + 1 < halves:
            nxt_half = half_stages(g // per_half + 1, ready)
            ready = ready + (nxt_half[-1],)
    return jnp.concatenate(outs, axis=0).reshape(bsz, length, D_MODEL)
```

```python
import functools

import jax
import jax.numpy as jnp
from jax import lax
from jax.experimental import pallas as pl
from jax.experimental.pallas import tpu as pltpu
from jax.experimental.pallas import tpu_sc as plsc

F32 = jnp.float32
BF16 = jnp.bfloat16

D_MODEL = 1024
DEPTH = 1
GRID_W = 64
N_HEADS = 8
QK_NOPE_DIM = 128
QK_ROPE_DIM = 64
V_HEAD_DIM = 128
Q_RANK = 384
KV_RANK = 256
ROPE_AXIS_DIM = QK_ROPE_DIM // 2
ROPE_BASE = 10000.0
S5_WIDTH = D_MODEL // 2
S5_GROUP = 16
S5_GROUPS = S5_WIDTH // S5_GROUP
S5_STATE = 64
PEER_HEADS = 8
PEER_N_KEYS = 128
PEER_TOPK = 16
PEER_HALF = 128
PEER_PAIRS = PEER_HEADS * PEER_TOPK
DEEPNORM_ALPHA = (2.0 * DEPTH) ** 0.25
LN_EPS = 1e-6
N_MOD = 6
ATT_SCALE = (QK_NOPE_DIM + QK_ROPE_DIM) ** -0.5

HEAD_PAD = 256
S5_COLS = S5_GROUPS * S5_STATE
S5_CHUNKS = 2
SC_WORKERS = 32
SC_LANES = 16
U_RING = 4
STAGE_LAG = 2
SC_PAIRS = 64
STAGE_STEPS = (0, 2, 4, 6)
SUM_STEPS = (1, 3, 5, 7)
TC_PAIRS = PEER_PAIRS - SC_PAIRS
CODE_NONE = 1.0e9
PEER_GROUPS = 8
HALF_W = D_MODEL // 2
VMEM_LIMIT = 48 * 1024 * 1024
ROW_TILE = 256
ATT_Q_TILE = 512
S5_TIME_TILE = 64
TOPK_LANES = 128
EVAL_TOKENS = 32


def _dot(a, b):
    return jnp.dot(a, b, preferred_element_type=F32)


def _dot_nt(a, b):
    return lax.dot_general(a, b, (((1,), (1,)), ((), ())), preferred_element_type=F32)


def _gelu(x):
    return 0.5 * x * (1.0 + jnp.tanh(0.7978845608028654 * (x + 0.044715 * (x * x * x))))


def _ln_plain(x):
    mu = jnp.mean(x, axis=-1, keepdims=True)
    xc = x - mu
    var = jnp.mean(xc * xc, axis=-1, keepdims=True)
    return xc * lax.rsqrt(var + LN_EPS)


def _rms(x, g):
    return x * lax.rsqrt(jnp.mean(x * x, axis=-1, keepdims=True) + LN_EPS) * g


def _params(*sem):
    return pltpu.CompilerParams(dimension_semantics=sem, vmem_limit_bytes=VMEM_LIMIT)


def _mod_kernel(cond_ref, w_ref, b_ref, o_ref):
    a = cond_ref[...]
    a = a * jax.nn.sigmoid(a)
    a_hi = a.astype(BF16)
    a_lo = (a - a_hi.astype(F32)).astype(BF16)
    w = w_ref[...]
    w_hi = w.astype(BF16)
    w_lo = (w - w_hi.astype(F32)).astype(BF16)
    o_ref[...] = _dot(a_hi, w_hi) + _dot(a_lo, w_hi) + _dot(a_hi, w_lo) + b_ref[...]


def _modulation(cond, w_mod, b_mod):
    rows = cond.shape[0]
    n = w_mod.shape[1]
    blk = D_MODEL
    return pl.pallas_call(
        _mod_kernel,
        grid=(n // blk,),
        in_specs=[
            pl.BlockSpec((rows, D_MODEL), lambda j: (0, 0)),
            pl.BlockSpec((D_MODEL, blk), lambda j: (0, j)),
            pl.BlockSpec((1, blk), lambda j: (0, j)),
        ],
        out_specs=pl.BlockSpec((rows, blk), lambda j: (0, j)),
        out_shape=jax.ShapeDtypeStruct((rows, n), F32),
        compiler_params=_params("arbitrary"),
        name="modulation",
    )(cond, w_mod, b_mod.reshape(1, n))


def _rope128(t, c_tab, s_tab):
    return t * c_tab + pltpu.roll(t, 64, axis=1) * s_tab


def _inproj_kernel(x_ref, sh_ref, sc_ref, c_ref, s_ref, wcq_ref, wckv_ref, wsu_ref, wgm_ref, wgs_ref,
                   qg_ref, kvg_ref, wuq_ref, wukv_ref, *out_refs, is_ctx):
    if is_ctx:
        k_ref, v_ref, su_ref = out_refs
    else:
        q_ref, k_ref, v_ref, su_ref, gm_ref, gs_ref = out_refs
    xm = _ln_plain(x_ref[0]) * (1.0 + sc_ref[0]) + sh_ref[0]
    xb = xm.astype(BF16)
    c_tab = c_ref[...]
    s_tab = s_ref[...]

    ckvpe = _dot(xb, wckv_ref[...])
    ckv_n = _rms(ckvpe[:, :KV_RANK], kvg_ref[...]).astype(BF16)
    kpe = _rope128(ckvpe[:, KV_RANK:], c_tab, s_tab).astype(BF16)
    kv = _dot(ckv_n, wukv_ref[...])
    for h in range(N_HEADS):
        k_ref[0, h, :, 0:128] = kv[:, h * 256:h * 256 + 128].astype(BF16)
        k_ref[0, h, :, 128:256] = kpe
        v_ref[0, h] = kv[:, h * 256 + 128:(h + 1) * 256].astype(BF16)

    su_ref[...] = _dot(xb, wsu_ref[...])

    if not is_ctx:
        cq_n = _rms(_dot(xb, wcq_ref[...]), qg_ref[...]).astype(BF16)
        q = _dot(cq_n, wuq_ref[...])
        for h in range(N_HEADS):
            q_ref[0, h, :, 0:128] = (q[:, h * 256:h * 256 + 128] * ATT_SCALE).astype(BF16)
            q_ref[0, h, :, 128:256] = (_rope128(q[:, h * 256 + 128:(h + 1) * 256], c_tab, s_tab) * ATT_SCALE).astype(BF16)
        gm_ref[0] = jax.nn.sigmoid(_dot(xb, wgm_ref[...])).astype(BF16)
        gs_ref[0] = jax.nn.sigmoid(_dot(xb, wgs_ref[...])).astype(BF16)


def _inproj(x, shift, scale, c_tab, s_tab, wts, *, is_ctx, tm, b0, bsz):
    length = x.shape[1]
    mod_map = (lambda b, t: (0, 0, 0)) if shift.shape[0] == 1 else (lambda b, t: (b0 + b, 0, 0))
    const2 = lambda b, t: (0, 0)
    wcq, wckv, wsu, wgm, wgs, qg, kvg, wuq, wukv = wts
    in_specs = [
        pl.BlockSpec((1, tm, D_MODEL), lambda b, t: (b0 + b, t, 0)),
        pl.BlockSpec((1, 1, D_MODEL), mod_map),
        pl.BlockSpec((1, 1, D_MODEL), mod_map),
        pl.BlockSpec((tm, 128), lambda b, t: (t, 0)),
        pl.BlockSpec((tm, 128), lambda b, t: (t, 0)),
    ] + [pl.BlockSpec(w.shape, const2) for w in (wcq, wckv, wsu, wgm, wgs, qg, kvg, wuq, wukv)]
    head_spec = lambda width: pl.BlockSpec((1, N_HEADS, tm, width), lambda b, t: (b, 0, t, 0))
    row_spec = pl.BlockSpec((1, tm, D_MODEL), lambda b, t: (b, t, 0))
    su_spec = pl.BlockSpec((tm, S5_WIDTH), lambda b, t: (t, b))
    k_shape = jax.ShapeDtypeStruct((bsz, N_HEADS, length, HEAD_PAD), BF16)
    v_shape = jax.ShapeDtypeStruct((bsz, N_HEADS, length, V_HEAD_DIM), BF16)
    su_shape = jax.ShapeDtypeStruct((length, bsz * S5_WIDTH), F32)
    g_shape = jax.ShapeDtypeStruct((bsz, length, D_MODEL), BF16)
    if is_ctx:
        out_specs = [head_spec(HEAD_PAD), head_spec(V_HEAD_DIM), su_spec]
        out_shape = [k_shape, v_shape, su_shape]
    else:
        out_specs = [head_spec(HEAD_PAD), head_spec(HEAD_PAD), head_spec(V_HEAD_DIM), su_spec, row_spec, row_spec]
        out_shape = [k_shape, k_shape, v_shape, su_shape, g_shape, g_shape]
    return pl.pallas_call(
        functools.partial(_inproj_kernel, is_ctx=is_ctx),
        grid=(bsz, length // tm),
        in_specs=in_specs,
        out_specs=out_specs,
        out_shape=out_shape,
        compiler_params=_params("parallel", "parallel"),
        name="inproj_ctx" if is_ctx else "inproj_lat",
    )(x, shift, scale, c_tab, s_tab, wcq, wckv, wsu, wgm, wgs, qg, kvg, wuq, wukv)


def _attn_kernel(q_ref, kc_ref, vc_ref, kl_ref, vl_ref, o_ref):
    q = q_ref[0, 0]
    s_c = _dot_nt(q, kc_ref[0, 0])
    s_l = _dot_nt(q, kl_ref[0, 0])
    m = jnp.maximum(jnp.max(s_c, axis=-1, keepdims=True), jnp.max(s_l, axis=-1, keepdims=True))
    p_c = jnp.exp(s_c - m)
    p_l = jnp.exp(s_l - m)
    denom = jnp.sum(p_c, axis=-1, keepdims=True) + jnp.sum(p_l, axis=-1, keepdims=True)
    o = _dot(p_c.astype(BF16), vc_ref[0, 0]) + _dot(p_l.astype(BF16), vl_ref[0, 0])
    o_ref[0] = (o / denom).astype(BF16)


def _attention(q, k_ctx, v_ctx, k_lat, v_lat, *, tq, b0, bsz):
    _, heads, length, _ = q.shape
    l_ctx = k_ctx.shape[2]
    full = lambda b, h, i: (b0 + b, h, 0, 0)
    return pl.pallas_call(
        _attn_kernel,
        grid=(bsz, heads, length // tq),
        in_specs=[
            pl.BlockSpec((1, 1, tq, HEAD_PAD), lambda b, h, i: (b0 + b, h, i, 0)),
            pl.BlockSpec((1, 1, l_ctx, HEAD_PAD), full),
            pl.BlockSpec((1, 1, l_ctx, V_HEAD_DIM), full),
            pl.BlockSpec((1, 1, length, HEAD_PAD), full),
            pl.BlockSpec((1, 1, length, V_HEAD_DIM), full),
        ],
        out_specs=pl.BlockSpec((1, tq, V_HEAD_DIM), lambda b, h, i: (b, i, h)),
        out_shape=jax.ShapeDtypeStruct((bsz, length, heads * V_HEAD_DIM), BF16),
        compiler_params=_params("parallel", "parallel", "arbitrary"),
        name="attention",
    )(q, k_ctx, v_ctx, k_lat, v_lat)


def _s5_kernel(u_ref, are_ref, aim_ref, bre_ref, bim_ref, cre_ref, cim_ref, y_ref,
               bu_re, bu_im, h_re, h_im, *, tt, bsz, n_ctx_tiles):
    d = pl.program_id(0)
    i = pl.program_id(1)

    @pl.when(i == 0)
    def _():
        h_re[...] = jnp.zeros_like(h_re)
        h_im[...] = jnp.zeros_like(h_im)

    u = u_ref[...].astype(BF16)
    half = S5_COLS // S5_CHUNKS
    cw = S5_WIDTH // S5_CHUNKS
    for c in range(S5_CHUNKS):
        uc = u[:, c * cw:(c + 1) * cw]
        bu_re[:, c * half:(c + 1) * half] = _dot(uc, bre_ref[0, c])
        bu_im[:, c * half:(c + 1) * half] = _dot(uc, bim_ref[0, c])

    col_w = 512
    for cc in range(S5_COLS // col_w):
        cols = slice(cc * col_w, (cc + 1) * col_w)
        a_r = are_ref[0, :, cols]
        a_i = aim_ref[0, :, cols]

        def step(t, carry, cols=cols, a_r=a_r, a_i=a_i):
            hr, hi = carry
            pos = jnp.where(d == 0, t, tt - 1 - t)
            r = pl.multiple_of(pos * bsz, bsz)
            nr = a_r * hr - a_i * hi + bu_re[pl.ds(r, bsz), cols]
            ni = a_r * hi + a_i * hr + bu_im[pl.ds(r, bsz), cols]
            bu_re[pl.ds(r, bsz), cols] = nr
            bu_im[pl.ds(r, bsz), cols] = ni
            return nr, ni

        hr, hi = lax.fori_loop(0, tt, step, (h_re[:, cols], h_im[:, cols]))
        h_re[:, cols] = hr
        h_im[:, cols] = hi

    @pl.when(i >= n_ctx_tiles)
    def _():
        for c in range(S5_CHUNKS):
            sl = slice(c * half, (c + 1) * half)
            y = _dot(bu_re[:, sl].astype(BF16), cre_ref[0, c]) + _dot(bu_im[:, sl].astype(BF16), cim_ref[0, c])
            y_ref[0, :, c * cw:(c + 1) * cw] = y


def _s5_scan(u_all, mats, *, bsz, l_ctx, tt):
    a_re_b, a_im_b, b_re_bd, b_im_bd, c_re_bd, c_im_bd = mats
    rows = tt * bsz
    n_tiles = u_all.shape[0] // rows
    n_ctx_tiles = l_ctx // tt
    n_lat_tiles = n_tiles - n_ctx_tiles

    def u_map(d, i):
        bwd = jnp.where(i < n_ctx_tiles, n_ctx_tiles - 1 - i, n_tiles - 1 - (i - n_ctx_tiles))
        return (jnp.where(d == 0, i, bwd), 0)

    def y_map(d, i):
        j = jnp.maximum(i - n_ctx_tiles, 0)
        return (d, jnp.where(d == 0, j, n_lat_tiles - 1 - j), 0)

    dmap3 = lambda d, i: (d, 0, 0)
    dmap4 = lambda d, i: (d, 0, 0, 0)
    return pl.pallas_call(
        functools.partial(_s5_kernel, tt=tt, bsz=bsz, n_ctx_tiles=n_ctx_tiles),
        grid=(2, n_tiles),
        in_specs=[
            pl.BlockSpec((rows, S5_WIDTH), u_map),
            pl.BlockSpec((1,) + a_re_b.shape[1:], dmap3),
            pl.BlockSpec((1,) + a_im_b.shape[1:], dmap3),
            pl.BlockSpec((1,) + b_re_bd.shape[1:], dmap4),
            pl.BlockSpec((1,) + b_im_bd.shape[1:], dmap4),
            pl.BlockSpec((1,) + c_re_bd.shape[1:], dmap4),
            pl.BlockSpec((1,) + c_im_bd.shape[1:], dmap4),
        ],
        out_specs=pl.BlockSpec((1, rows, S5_WIDTH), y_map),
        out_shape=jax.ShapeDtypeStruct((2, n_lat_tiles * rows, S5_WIDTH), F32),
        scratch_shapes=[
            pltpu.VMEM((rows, S5_COLS), F32),
            pltpu.VMEM((rows, S5_COLS), F32),
            pltpu.VMEM((bsz, S5_COLS), F32),
            pltpu.VMEM((bsz, S5_COLS), F32),
        ],
        compiler_params=_params("arbitrary", "arbitrary"),
        name="s5_scan",
    )(u_all, a_re_b, a_im_b, b_re_bd, b_im_bd, c_re_bd, c_im_bd)


def _s5_matrices(a_re, a_im, log_dt, b_re, b_im, c_re, c_im, bsz):
    dt = jnp.exp(log_dt)[..., None]
    mag = jnp.exp(a_re * dt)
    ab_re = mag * jnp.cos(a_im * dt)
    ab_im = mag * jnp.sin(a_im * dt)
    den = a_re * a_re + a_im * a_im
    f_re = ((ab_re - 1.0) * a_re + ab_im * a_im) / den
    f_im = (ab_im * a_re - (ab_re - 1.0) * a_im) / den
    bb_re = f_re[..., None] * b_re - f_im[..., None] * b_im
    bb_im = f_re[..., None] * b_im + f_im[..., None] * b_re
    gl = S5_GROUPS // S5_CHUNKS
    eye = jnp.eye(gl, dtype=F32)

    def in_bd(bb):
        t = bb.reshape(2, S5_CHUNKS, gl, S5_STATE, S5_GROUP)
        t = jnp.einsum('dkgpc,gh->dkgchp', t, eye)
        return t.reshape(2, S5_CHUNKS, gl * S5_GROUP, gl * S5_STATE).astype(BF16)

    def out_bd(cc):
        t = cc.reshape(2, S5_CHUNKS, gl, S5_GROUP, S5_STATE)
        t = jnp.einsum('dkgcp,gh->dkgphc', t, eye)
        return t.reshape(2, S5_CHUNKS, gl * S5_STATE, gl * S5_GROUP).astype(BF16)

    bcast = lambda a: jnp.broadcast_to(a.reshape(2, 1, S5_COLS), (2, bsz, S5_COLS))
    return bcast(ab_re), bcast(ab_im), in_bd(bb_re), in_bd(bb_im), out_bd(c_re), out_bd(-c_im)


def _merge_kernel(x_ref, att_ref, gm_ref, gs_ref, su_ref, y_ref, dsk_ref, wglu_ref, wout_ref, ga_ref,
                  l1g_ref, l1b_ref, shf_ref, scf_ref, wq_ref, keys_ref, h1_ref, xm_ref, st_ref):
    y = su_ref[...] * dsk_ref[...] + y_ref[0] + y_ref[1]
    gl = _dot(_gelu(y).astype(BF16), wglu_ref[...])
    s5_out = gl[:, :D_MODEL] * jax.nn.sigmoid(gl[:, D_MODEL:])
    merged = gm_ref[0].astype(F32) * att_ref[0].astype(F32) + gs_ref[0].astype(F32) * s5_out
    out = _dot(merged.astype(BF16), wout_ref[...])
    h1 = _ln_plain(DEEPNORM_ALPHA * x_ref[0] + ga_ref[0] * out) * l1g_ref[...] + l1b_ref[...]
    h1_ref[0] = h1
    xm = _ln_plain(h1) * (1.0 + scf_ref[0]) + shf_ref[0]
    xm_ref[0] = xm
    qp = _dot(xm.astype(BF16), wq_ref[...]).astype(BF16)
    for j in range(2 * PEER_HEADS):
        st_ref[j] = _dot_nt(keys_ref[j], qp[:, j * PEER_HALF:(j + 1) * PEER_HALF])


def _merge(x, att, gm, gs, su_all2d, y2d, dsk, wglu, wout, g_a, l1g, l1b, sh_f, sc_f, wq, keys, *, tm, ctx_tiles, b0, h0):
    bsz, length, _ = att.shape
    nt = length // tm
    row = pl.BlockSpec((1, tm, D_MODEL), lambda b, t: (b, t, 0))
    xrow = pl.BlockSpec((1, tm, D_MODEL), lambda b, t: (b0 + b, t, 0))
    hrow = pl.BlockSpec((1, tm, D_MODEL), lambda b, t: (h0 + b, t, 0))
    modb = pl.BlockSpec((1, 1, D_MODEL), lambda b, t: (b0 + b, 0, 0))
    const = lambda a: pl.BlockSpec(a.shape, lambda b, t: (0,) * a.ndim)
    return pl.pallas_call(
        _merge_kernel,
        grid=(bsz, nt),
        in_specs=[
            xrow, row, hrow, hrow,
            pl.BlockSpec((tm, S5_WIDTH), lambda b, t: (ctx_tiles + t, h0 + b)),
            pl.BlockSpec((2, tm, S5_WIDTH), lambda b, t: (0, t, h0 + b)),
            const(dsk), const(wglu), const(wout), modb, const(l1g), const(l1b), modb, modb, const(wq), const(keys),
        ],
        out_specs=[row, row, pl.BlockSpec((2 * PEER_HEADS, PEER_N_KEYS, tm), lambda b, t: (0, 0, b * nt + t))],
        out_shape=[
            jax.ShapeDtypeStruct((bsz, length, D_MODEL), F32),
            jax.ShapeDtypeStruct((bsz, length, D_MODEL), F32),
            jax.ShapeDtypeStruct((2 * PEER_HEADS, PEER_N_KEYS, bsz * length), F32),
        ],
        compiler_params=_params("parallel", "parallel"),
        name="merge_peer_query",
    )(x, att, gm, gs, su_all2d, y2d, dsk, wglu, wout, g_a, l1g, l1b, sh_f, sc_f, wq, keys)


def _take_top(vals, codes, payload, k):
    rows = lax.broadcasted_iota(jnp.int32, (k, vals.shape[1]), 0)
    top_v = jnp.zeros((k, vals.shape[1]), F32)
    top_p = jnp.zeros((k, vals.shape[1]), F32)
    for r in range(k):
        m = jnp.max(vals, axis=0, keepdims=True)
        cm = jnp.min(jnp.where(vals == m, codes, CODE_NONE), axis=0, keepdims=True)
        sel = codes == cm
        if payload is None:
            p = cm
        else:
            p = jnp.max(jnp.where(sel, payload, -1.0), axis=0, keepdims=True)
        vals = jnp.where(sel, -jnp.inf, vals)
        top_v = jnp.where(rows == r, m, top_v)
        top_p = jnp.where(rows == r, p, top_p)
    return top_v, top_p


def _topk_kernel(st_ref, e_ref, g_ref):
    lanes = st_ref.shape[2]
    key_idx = lax.broadcasted_iota(jnp.int32, (PEER_N_KEYS, lanes), 0).astype(F32)
    sub_idx = lax.broadcasted_iota(jnp.int32, (PEER_TOPK, lanes), 0).astype(F32)
    half_k = PEER_TOPK // 2
    for h in range(PEER_HEADS):
        sv0, si0 = _take_top(st_ref[2 * h], key_idx, None, PEER_TOPK)
        sv1, si1 = _take_top(st_ref[2 * h + 1], key_idx, None, PEER_TOPK)
        cs, ce, cc = [], [], []
        for i in range(half_k):
            nj = PEER_TOPK if i == 0 else half_k
            cs.append(sv0[i:i + 1, :] + sv1[0:nj, :])
            ce.append(si0[i:i + 1, :] * PEER_N_KEYS + si1[0:nj, :])
            cc.append(sub_idx[0:nj, :] + i * PEER_TOPK)
        cs.append(sv0[half_k:, :] + sv1[0:1, :])
        ce.append(si0[half_k:, :] * PEER_N_KEYS + si1[0:1, :])
        cc.append((sub_idx[0:half_k, :] + half_k) * PEER_TOPK)
        top_s, top_e = _take_top(jnp.concatenate(cs, axis=0), jnp.concatenate(cc, axis=0),
                                 jnp.concatenate(ce, axis=0), PEER_TOPK)
        p = jnp.exp(top_s - jnp.max(top_s, axis=0, keepdims=True))
        g_ref[h] = p / jnp.sum(p, axis=0, keepdims=True)
        e_ref[h] = top_e.astype(jnp.int32)


def _peer_topk(st, *, lanes):
    n = st.shape[2]
    out_spec = pl.BlockSpec((PEER_HEADS, PEER_TOPK, lanes), lambda i: (0, 0, i))
    return pl.pallas_call(
        _topk_kernel,
        grid=(n // lanes,),
        in_specs=[pl.BlockSpec((2 * PEER_HEADS, PEER_N_KEYS, lanes), lambda i: (0, 0, i))],
        out_specs=[out_spec, out_spec],
        out_shape=[jax.ShapeDtypeStruct((PEER_HEADS, PEER_TOPK, n), jnp.int32),
                   jax.ShapeDtypeStruct((PEER_HEADS, PEER_TOPK, n), F32)],
        compiler_params=_params("parallel"),
        name="peer_topk",
    )(st)


def _sc_peer(u_tab, v_tab, idx, x, gates):
    n_tok = x.shape[0]
    per_worker = n_tok // SC_WORKERS
    assert per_worker % 2 == 0 and per_worker >= 4
    assert len(STAGE_STEPS) == (PEER_PAIRS - SC_PAIRS) // SC_LANES and len(SUM_STEPS) == SC_PAIRS // SC_LANES
    steps = PEER_PAIRS // SC_LANES
    assert steps % U_RING == 0
    n_vc = SC_PAIRS // SC_LANES
    n_vs = (PEER_PAIRS - SC_PAIRS) // SC_LANES
    tc_pairs = PEER_PAIRS - SC_PAIRS
    chunks = HALF_W // SC_LANES
    mesh = plsc.VectorSubcoreMesh(core_axis_name="c", subcore_axis_name="s")

    @functools.partial(
        pl.kernel, mesh=mesh,
        out_type=(jax.ShapeDtypeStruct((n_tok * PEER_PAIRS,), F32),
                  jax.ShapeDtypeStruct((n_tok * tc_pairs, HALF_W), jnp.int32),
                  jax.ShapeDtypeStruct((n_tok, 2 * HALF_W), F32)),
        scratch_types=[
            pltpu.VMEM((2, PEER_PAIRS), jnp.int32),
            pltpu.VMEM((2, PEER_PAIRS), jnp.int32),
            pltpu.VMEM((2, 2 * HALF_W), F32),
            pltpu.VMEM((2, SC_PAIRS), F32),
            pltpu.VMEM((2, SC_PAIRS), F32),
            pltpu.VMEM((2, PEER_PAIRS), F32),
            pltpu.VMEM((2, 2 * HALF_W), F32),
            pltpu.VMEM((U_RING, SC_LANES, HALF_W), jnp.int32),
            pltpu.VMEM((n_vc, SC_LANES, HALF_W), jnp.int32),
            pltpu.VMEM((n_vs, SC_LANES, HALF_W), jnp.int32),
            pltpu.SemaphoreType.DMA((U_RING,)),
            pltpu.SemaphoreType.DMA((n_vc,)),
            pltpu.SemaphoreType.DMA((n_vs,)),
            pltpu.SemaphoreType.DMA((n_vs,)),
            pltpu.SemaphoreType.DMA((2,)),
            pltpu.SemaphoreType.DMA((2,)),
            pltpu.SemaphoreType.DMA((2,)),
            pltpu.SemaphoreType.DMA((2,)),
            pltpu.SemaphoreType.DMA((2,)),
        ],
        compiler_params=pltpu.CompilerParams(needs_layout_passes=False),
    )
    def peer(u_hbm, v_hbm, idx_hbm, x_hbm, gate_hbm, act_hbm, vrows_hbm, f_hbm,
             idx_v, vidx_v, x_v, gate_v, coef_v, act_v, f_v, urows_v, vsum_v, vstage_v,
             u_sems, vc_sems, vg_sems, vw_sems, idx_sems, x_sems, gate_sems, act_sems, f_sems):
        wid = lax.axis_index("s") * 2 + lax.axis_index("c")
        base = wid * per_worker
        last = base + per_worker - 1
        lane = lax.iota(jnp.int32, SC_LANES)

        def pair_off(tok):
            return pl.multiple_of(tok * PEER_PAIRS, PEER_PAIRS)

        def idx_copy(tok, slot):
            return pltpu.make_async_copy(idx_hbm.at[pl.ds(pair_off(tok), PEER_PAIRS)], idx_v.at[slot], idx_sems.at[slot])

        def x_copy(tok, slot):
            return pltpu.make_async_copy(x_hbm.at[tok], x_v.at[slot], x_sems.at[slot])

        def gate_copy(tok, slot):
            return pltpu.make_async_copy(gate_hbm.at[pl.ds(pair_off(tok), SC_PAIRS)], gate_v.at[slot], gate_sems.at[slot])

        def u_gather(slot, g):
            return pltpu.make_async_copy(u_hbm.at[idx_v.at[slot, pl.ds(g * SC_LANES, SC_LANES)]], urows_v.at[g % U_RING],
                                         u_sems.at[g % U_RING])

        def vsum_gather(slot, j):
            return pltpu.make_async_copy(v_hbm.at[vidx_v.at[slot, pl.ds(j * SC_LANES, SC_LANES)]], vsum_v.at[j],
                                         vc_sems.at[j])

        def vstage_gather(slot, s):
            return pltpu.make_async_copy(v_hbm.at[vidx_v.at[slot, pl.ds(SC_PAIRS + s * SC_LANES, SC_LANES)]],
                                         vstage_v.at[s], vg_sems.at[s])

        def vstage_write(tok, s):
            row0 = pl.multiple_of(tok * tc_pairs + s * SC_LANES, SC_LANES)
            return pltpu.make_async_copy(vstage_v.at[s], vrows_hbm.at[pl.ds(row0, SC_LANES)], vw_sems.at[s])

        def act_copy(tok, slot):
            return pltpu.make_async_copy(act_v.at[slot], act_hbm.at[pl.ds(pair_off(tok), PEER_PAIRS)], act_sems.at[slot])

        def f_copy(tok, slot):
            return pltpu.make_async_copy(f_v.at[slot], f_hbm.at[tok], f_sems.at[slot])

        def unpack(wv):
            return (lax.bitcast_convert_type(wv << 16, F32), lax.bitcast_convert_type(wv & jnp.int32(-65536), F32))

        def step_dots(slot, g):
            def body(c, accs):
                xlo = x_v[slot, pl.ds(c * SC_LANES, SC_LANES)]
                xhi = x_v[slot, pl.ds(HALF_W + c * SC_LANES, SC_LANES)]
                new = []
                for r in range(SC_LANES):
                    lo, hi = unpack(urows_v[g % U_RING, r, pl.ds(c * SC_LANES, SC_LANES)])
                    new.append(accs[r] + lo * xlo + hi * xhi)
                return tuple(new)

            accs = lax.fori_loop(0, chunks, body, tuple(jnp.zeros((SC_LANES,), F32) for _ in range(SC_LANES)))
            v = jnp.zeros((SC_LANES,), F32)
            for r in range(SC_LANES):
                v = jnp.where(lane == r, jnp.sum(accs[r]), v)
            act_v[slot, pl.ds(g * SC_LANES, SC_LANES)] = v

        def window_sum(fslot, j):
            cvec = coef_v[fslot, pl.ds(j * SC_LANES, SC_LANES)]
            cb = [jnp.zeros((SC_LANES,), F32) + jnp.sum(jnp.where(lane == r, cvec, 0.0)) for r in range(SC_LANES)]

            def body(c, carry):
                alo = f_v[fslot, pl.ds(c * SC_LANES, SC_LANES)]
                ahi = f_v[fslot, pl.ds(HALF_W + c * SC_LANES, SC_LANES)]
                for r in range(SC_LANES):
                    lo, hi = unpack(vsum_v[j, r, pl.ds(c * SC_LANES, SC_LANES)])
                    alo = alo + cb[r] * lo
                    ahi = ahi + cb[r] * hi
                f_v[fslot, pl.ds(c * SC_LANES, SC_LANES)] = alo
                f_v[fslot, pl.ds(HALF_W + c * SC_LANES, SC_LANES)] = ahi
                return carry

            lax.fori_loop(0, chunks, body, 0)

        def token_coefs(slot):
            for q in range(n_vc):
                a = act_v[slot, pl.ds(q * SC_LANES, SC_LANES)]
                z2 = 1.5957691216057308 * (a + 0.044715 * (a * a * a))
                coef_v[slot, pl.ds(q * SC_LANES, SC_LANES)] = gate_v[slot, pl.ds(q * SC_LANES, SC_LANES)] * a / (1.0 + jnp.exp(-z2))

        def zero_f(fslot):
            for q in range(2 * HALF_W // SC_LANES):
                f_v[fslot, pl.ds(q * SC_LANES, SC_LANES)] = jnp.zeros((SC_LANES,), F32)

        idx_copy(base, 0).start()
        x_copy(base, 0).start()
        gate_copy(base, 0).start()
        idx_copy(base, 0).wait()
        for g in range(U_RING):
            u_gather(0, g).start()

        @pl.loop(0, per_worker, step=2)
        def _(i):
            for slot in range(2):
                tok = base + i + slot
                nxt = jnp.minimum(tok + 1, last)
                other = 1 - slot
                first = slot == 0
                idx_copy(nxt, other).start()
                x_copy(nxt, other).start()
                gate_copy(nxt, other).start()
                for q in range(PEER_PAIRS // SC_LANES):
                    vidx_v[slot, pl.ds(q * SC_LANES, SC_LANES)] = idx_v[slot, pl.ds(q * SC_LANES, SC_LANES)]
                x_copy(tok, slot).wait()
                gate_copy(tok, slot).wait()

                @pl.when(i > 0)
                def _():
                    act_copy(tok, slot).wait()

                if first:
                    @pl.when(i > 2)
                    def _():
                        f_copy(tok, other).wait()
                else:
                    @pl.when(i > 0)
                    def _():
                        f_copy(tok, other).wait()
                zero_f(other)

                idx_copy(nxt, other).wait()
                for g in range(steps):
                    s = STAGE_STEPS.index(g) if g in STAGE_STEPS else None
                    if s is not None:
                        if first:
                            @pl.when(i > 0)
                            def _(s=s, tok=tok):
                                vstage_write(tok, s).wait()
                        else:
                            vstage_write(tok, s).wait()
                        vstage_gather(slot, s).start()
                    u_gather(slot, g).wait()
                    step_dots(slot, g)
                    if g + U_RING < steps:
                        u_gather(slot, g + U_RING).start()
                    else:
                        u_gather(other, g + U_RING - steps).start()
                    if s is not None:
                        sp = (s - STAGE_LAG) % n_vs
                        tokp, slotp = (tok, slot) if s >= STAGE_LAG else (tok - 1, other)

                        def finish(sp=sp, tokp=tokp, slotp=slotp):
                            vstage_gather(slotp, sp).wait()
                            vstage_write(tokp, sp).start()

                        if first and s < STAGE_LAG:
                            pl.when(i > 0)(finish)
                        else:
                            finish()
                    if g in SUM_STEPS:
                        j = SUM_STEPS.index(g)

                        def consume(j=j, other=other):
                            vsum_gather(other, j).wait()
                            window_sum(other, j)

                        if first:
                            pl.when(i > 0)(consume)
                        else:
                            consume()
                        vsum_gather(slot, j).start()
                token_coefs(slot)
                act_copy(tok, slot).start()

                def send_prev(tok=tok, other=other):
                    f_copy(tok - 1, other).start()

                if first:
                    pl.when(i > 0)(send_prev)
                else:
                    send_prev()

        for s in range(n_vs - STAGE_LAG, n_vs):
            vstage_gather(1, s).wait()
            vstage_write(last, s).start()
        f_copy(last, 1).wait()
        zero_f(1)
        for j in range(n_vc):
            vsum_gather(1, j).wait()
            window_sum(1, j)
        f_copy(last, 1).start()
        for s in range(n_vs):
            vstage_write(last, s).wait()
        for g in range(U_RING):
            u_gather(0, g).wait()
        x_copy(last, 0).wait()
        gate_copy(last, 0).wait()
        for slot in range(2):
            f_copy(last, slot).wait()
            act_copy(last, slot).wait()

    return peer(u_tab, v_tab, idx, x, gates)


def _peer_kernel(vg_ref, act_ref, gate_ref, t2_ref, mask_ref, fsc_ref, h1_ref, gf_ref, l2g_ref, l2b_ref, o_ref, f_ref, *, tb):
    sub = 8
    span = sub * TC_PAIRS
    half = D_MODEL // 2
    for s in range(tb // sub):
        rows = slice(s * span, (s + 1) * span)
        toks = slice(s * sub, (s + 1) * sub)
        coef = (gate_ref[toks, :] * _gelu(act_ref[toks, :])).astype(BF16)
        spread = _dot(coef, t2_ref[...])
        c2 = (jnp.concatenate([spread, spread], axis=0) * mask_ref[...]).astype(BF16)
        f2 = _dot(c2, pltpu.bitcast(vg_ref[rows, :], BF16))
        f_ref[toks, :half] = f2[:sub]
        f_ref[toks, half:] = f2[sub:]
    r = DEEPNORM_ALPHA * h1_ref[...] + gf_ref[0] * (f_ref[...] + fsc_ref[...])
    o_ref[...] = _ln_plain(r) * l2g_ref[...] + l2b_ref[...]


def _peer_eval(vg, act, gate, t2, mask, f_sc, h1, g_f, l2g, l2b, *, tb, tok0, length):
    n = h1.shape[0]
    tok_blk0 = tok0 // tb
    per_b = length // tb
    tok_spec = lambda width: pl.BlockSpec((tb, width), lambda i: (i, 0))
    return pl.pallas_call(
        functools.partial(_peer_kernel, tb=tb),
        grid=(n // tb,),
        in_specs=[
            pl.BlockSpec((tb * TC_PAIRS, D_MODEL // 2), lambda i: (i, 0)),
            tok_spec(PEER_PAIRS), tok_spec(PEER_PAIRS),
            pl.BlockSpec(t2.shape, lambda i: (0, 0)), pl.BlockSpec(mask.shape, lambda i: (0, 0)),
            tok_spec(D_MODEL), tok_spec(D_MODEL),
            pl.BlockSpec((1, 1, D_MODEL), lambda i: ((tok_blk0 + i) // per_b, 0, 0)),
            pl.BlockSpec((1, D_MODEL), lambda i: (0, 0)),
            pl.BlockSpec((1, D_MODEL), lambda i: (0, 0)),
        ],
        out_specs=tok_spec(D_MODEL),
        out_shape=jax.ShapeDtypeStruct((n, D_MODEL), F32),
        scratch_shapes=[pltpu.VMEM((tb, D_MODEL), F32)],
        compiler_params=_params("parallel"),
        name="peer_eval",
    )(vg, act, gate, t2, mask, f_sc, h1, g_f, l2g, l2b)


def _rope_perm():
    half = ROPE_AXIS_DIM // 2
    base = jnp.arange(QK_ROPE_DIM)
    return jnp.where((base % ROPE_AXIS_DIM) < half, base + half, base - half)


def _rope_tables(length):
    pos = jnp.arange(length)
    row = (pos // GRID_W).astype(F32)
    col = (pos % GRID_W).astype(F32)
    inv_freq = jnp.power(ROPE_BASE, -jnp.arange(0, ROPE_AXIS_DIM, 2, dtype=F32) / ROPE_AXIS_DIM)
    ang_r = row[:, None] * inv_freq
    ang_c = col[:, None] * inv_freq
    zeros = jnp.zeros((length, QK_ROPE_DIM), F32)
    c_tab = jnp.concatenate([jnp.cos(ang_r), jnp.cos(ang_r), jnp.cos(ang_c), jnp.cos(ang_c), zeros], axis=1)
    s_tab = jnp.concatenate([-jnp.sin(ang_r), jnp.sin(ang_r), -jnp.sin(ang_c), jnp.sin(ang_c), zeros], axis=1)
    return c_tab, s_tab


def _identity_tables(length):
    ones = jnp.ones((length, QK_ROPE_DIM), F32)
    zeros = jnp.zeros((length, QK_ROPE_DIM), F32)
    return jnp.concatenate([ones, zeros], axis=1), jnp.zeros((length, 2 * QK_ROPE_DIM), F32)


def _spread_consts():
    lanes = jnp.arange(16 * TC_PAIRS)
    pair = (lanes % (2 * TC_PAIRS)) // 2 + SC_PAIRS
    t2 = (jnp.arange(PEER_PAIRS)[:, None] == pair[None, :]).astype(BF16)
    r = jnp.arange(16)
    own = (lanes // (2 * TC_PAIRS))[None, :] == (r % 8)[:, None]
    parity = (lanes % 2 == 0)[None, :] == (r < 8)[:, None]
    return t2, (own & parity).astype(F32)


def _pack_rows(table):
    bits = lax.bitcast_convert_type(table.astype(BF16), jnp.uint16).astype(jnp.uint32)
    return lax.bitcast_convert_type(bits[:, :HALF_W] | (bits[:, HALF_W:] << 16), jnp.int32)


def _layer_weights(w_in, q_norm_g, kv_norm_g, w_uq, w_ukv):
    perm = _rope_perm()
    p0 = Q_RANK
    p1 = p0 + KV_RANK
    p2 = p1 + QK_ROPE_DIM
    p3 = p2 + S5_WIDTH
    p4 = p3 + D_MODEL
    w_kpe = w_in[:, p1:p2]
    wckv = jnp.concatenate([w_in[:, p0:p1], w_kpe, w_kpe[:, perm]], axis=1)
    uq = w_uq.reshape(Q_RANK, N_HEADS, QK_NOPE_DIM + QK_ROPE_DIM)
    pe = uq[:, :, QK_NOPE_DIM:]
    wuq = jnp.concatenate([uq[:, :, :QK_NOPE_DIM], pe, pe[:, :, perm]], axis=2).reshape(Q_RANK, N_HEADS * HEAD_PAD)
    cast = lambda a: a.astype(BF16)
    return (cast(w_in[:, :p0]), cast(wckv), cast(w_in[:, p2:p3]), cast(w_in[:, p3:p4]), cast(w_in[:, p4:]),
            q_norm_g.reshape(1, Q_RANK), kv_norm_g.reshape(1, KV_RANK), cast(wuq), cast(w_ukv))


def kernel(x, c, ctx, c_ctx, w_mod, b_mod, w_in, q_norm_g, kv_norm_g, w_uq, w_ukv, s5_a_re, s5_a_im, s5_log_dt, s5_b_re, s5_b_im, s5_c_re, s5_c_im, s5_d, w_glu, w_out, ln1_g, ln1_b, peer_wq, peer_keys, peer_u, peer_v, ln2_g, ln2_b):
    bsz, length, _ = x.shape
    l_ctx = ctx.shape[1]
    layer = 0
    tm = ROW_TILE
    s5_tt = S5_TIME_TILE
    peer_tb = EVAL_TOKENS

    cond = jnp.concatenate([c, c_ctx[None, :], jnp.zeros((7, D_MODEL), F32)], axis=0)
    mod = _modulation(cond, w_mod[layer], b_mod[layer]).reshape(cond.shape[0], N_MOD, 1, D_MODEL)
    sh_a, sc_a, g_a, sh_f, sc_f, g_f = (mod[:bsz, i] for i in range(N_MOD))
    csh_a, csc_a = mod[bsz:bsz + 1, 0], mod[bsz:bsz + 1, 1]

    wts = _layer_weights(w_in[layer], q_norm_g[layer], kv_norm_g[layer], w_uq[layer], w_ukv[layer])
    c_lat, s_lat = _rope_tables(length)
    c_id, s_id = _identity_tables(l_ctx)
    keys = peer_keys[layer].reshape(2 * PEER_HEADS, PEER_N_KEYS, PEER_HALF).astype(BF16)
    wglu, wout, wq = w_glu[layer].astype(BF16), w_out[layer].astype(BF16), peer_wq[layer].astype(BF16)
    u_pack = _pack_rows(peer_u[layer])
    v_pack = _pack_rows(peer_v[layer])
    l2g = ln2_g[layer].reshape(1, D_MODEL)
    l2b = ln2_b[layer].reshape(1, D_MODEL)
    t2, spread_mask = _spread_consts()

    n_groups = PEER_GROUPS if bsz % PEER_GROUPS == 0 else 1
    halves = 2 if (n_groups % 2 == 0 and bsz % 16 == 0) else 1
    per_half = n_groups // halves
    hb = bsz // halves
    gb = bsz // n_groups
    g_tok = gb * length
    mats = _s5_matrices(s5_a_re[layer], s5_a_im[layer], s5_log_dt[layer], s5_b_re[layer], s5_b_im[layer],
                        s5_c_re[layer], s5_c_im[layer], hb)
    def half_stages(h, ready):
        x_h, ctx_h, _ = lax.optimization_barrier((x, ctx, ready))
        k_ctx, v_ctx, su_ctx = _inproj(ctx_h, csh_a, csc_a, c_id, s_id, wts, is_ctx=True, tm=min(tm, l_ctx), b0=h * hb, bsz=hb)
        q, k_lat, v_lat, su_lat, gm, gs = _inproj(x_h, sh_a, sc_a, c_lat, s_lat, wts, is_ctx=False, tm=tm, b0=h * hb, bsz=hb)
        su_all = jnp.concatenate([su_ctx, su_lat], axis=0)
        y = _s5_scan(su_all.reshape(-1, S5_WIDTH), mats, bsz=hb, l_ctx=l_ctx, tt=s5_tt)
        return k_ctx, v_ctx, q, k_lat, v_lat, gm, gs, su_all, y.reshape(2, length, hb * S5_WIDTH)

    outs = []

    def finish_group(grp, after):
        vg, act, gate, f_sc, h1f, tok0 = grp
        gate, _ = lax.optimization_barrier((gate, (after, tuple(outs[-1:]))))
        return _peer_eval(vg, act, gate, t2, spread_mask, f_sc, h1f, g_f, l2g, l2b, tb=peer_tb, tok0=tok0, length=length)

    pending = None
    ready = (u_pack, v_pack)
    nxt_half = half_stages(0, ready)
    for g in range(n_groups):
        b0 = g * gb
        h0 = (g % per_half) * gb
        if g % per_half == 0:
            k_ctx, v_ctx, q, k_lat, v_lat, gm, gs, su_all, y2d = nxt_half
        q_g, _ = lax.optimization_barrier((q, ready))
        att = _attention(q_g, k_ctx, v_ctx, k_lat, v_lat, tq=min(ATT_Q_TILE, length), b0=h0, bsz=gb)
        h1, xm, st = _merge(x, att, gm, gs, su_all, y2d, s5_d[layer].reshape(1, S5_WIDTH), wglu, wout, g_a,
                            ln1_g[layer].reshape(1, D_MODEL), ln1_b[layer].reshape(1, D_MODEL), sh_f, sc_f, wq, keys,
                            tm=tm, ctx_tiles=l_ctx // tm, b0=b0, h0=h0)
        experts, gates = _peer_topk(st, lanes=TOPK_LANES)
        idx = experts.transpose(2, 0, 1).reshape(g_tok * PEER_PAIRS)
        gate = gates.transpose(2, 0, 1).reshape(g_tok, PEER_PAIRS)
        xmf = xm.reshape(g_tok, D_MODEL)
        acts, vg, f_sc = _sc_peer(u_pack, v_pack, idx, xmf, gate.reshape(g_tok * PEER_PAIRS))
        launched = (idx, gate, xmf)
        if g % per_half == max(per_half - 3, 0) and g // per_half + 1 < halves:
            nxt_half = half_stages(g // per_half + 1, launched)
            launched = launched + (nxt_half[-1],)
        if pending is not None:
            outs.append(finish_group(pending, launched))
        pending = (vg, acts.reshape(g_tok, PEER_PAIRS), gate, f_sc, h1.reshape(g_tok, D_MODEL), g * g_tok)
        ready = launched + ((outs[-1],) if outs else ())
    outs.append(finish_group(pending, ()))
    return jnp.concatenate(outs, axis=0).reshape(bsz, length, D_MODEL)
```

```python
import functools

import jax
import jax.numpy as jnp
from jax import lax
from jax.experimental import pallas as pl
from jax.experimental.pallas import tpu as pltpu
from jax.experimental.pallas import tpu_sc as plsc

F32 = jnp.float32
BF16 = jnp.bfloat16

D_MODEL = 1024
DEPTH = 1
GRID_W = 64
N_HEADS = 8
QK_NOPE_DIM = 128
QK_ROPE_DIM = 64
V_HEAD_DIM = 128
Q_RANK = 384
KV_RANK = 256
ROPE_AXIS_DIM = QK_ROPE_DIM // 2
ROPE_BASE = 10000.0
S5_WIDTH = D_MODEL // 2
S5_GROUP = 16
S5_GROUPS = S5_WIDTH // S5_GROUP
S5_STATE = 64
PEER_HEADS = 8
PEER_N_KEYS = 128
PEER_TOPK = 16
PEER_HALF = 128
PEER_PAIRS = PEER_HEADS * PEER_TOPK
DEEPNORM_ALPHA = (2.0 * DEPTH) ** 0.25
LN_EPS = 1e-6
N_MOD = 6
ATT_SCALE = (QK_NOPE_DIM + QK_ROPE_DIM) ** -0.5

HEAD_PAD = 256
S5_COLS = S5_GROUPS * S5_STATE
S5_CHUNKS = 2
SC_WORKERS = 32
SC_LANES = 16
U_RING = 4
DOT_ROWS = 8
SC_UNROLL = 1
STAGE_LAG = 2
SC_PAIRS = 64
STAGE_STEPS = (0, 2, 4, 6)
SUM_STEPS = (1, 3, 5, 7)
TC_PAIRS = PEER_PAIRS - SC_PAIRS
CODE_NONE = 1.0e9
PEER_GROUPS = 8
HALF_W = D_MODEL // 2
VMEM_LIMIT = 48 * 1024 * 1024
ROW_TILE = 256
ATT_Q_TILE = 512
S5_TIME_TILE = 64
TOPK_LANES = 128
EVAL_TOKENS = 32


def _dot(a, b):
    return jnp.dot(a, b, preferred_element_type=F32)


def _dot_nt(a, b):
    return lax.dot_general(a, b, (((1,), (1,)), ((), ())), preferred_element_type=F32)


def _gelu(x):
    return 0.5 * x * (1.0 + jnp.tanh(0.7978845608028654 * (x + 0.044715 * (x * x * x))))


def _ln_plain(x):
    mu = jnp.mean(x, axis=-1, keepdims=True)
    xc = x - mu
    var = jnp.mean(xc * xc, axis=-1, keepdims=True)
    return xc * lax.rsqrt(var + LN_EPS)


def _rms(x, g):
    return x * lax.rsqrt(jnp.mean(x * x, axis=-1, keepdims=True) + LN_EPS) * g


def _params(*sem):
    return pltpu.CompilerParams(dimension_semantics=sem, vmem_limit_bytes=VMEM_LIMIT)


def _mod_kernel(cond_ref, w_ref, b_ref, o_ref):
    a = cond_ref[...]
    a = a * jax.nn.sigmoid(a)
    a_hi = a.astype(BF16)
    a_lo = (a - a_hi.astype(F32)).astype(BF16)
    w = w_ref[...]
    w_hi = w.astype(BF16)
    w_lo = (w - w_hi.astype(F32)).astype(BF16)
    o_ref[...] = _dot(a_hi, w_hi) + _dot(a_lo, w_hi) + _dot(a_hi, w_lo) + b_ref[...]


def _modulation(cond, w_mod, b_mod):
    rows = cond.shape[0]
    n = w_mod.shape[1]
    blk = D_MODEL
    return pl.pallas_call(
        _mod_kernel,
        grid=(n // blk,),
        in_specs=[
            pl.BlockSpec((rows, D_MODEL), lambda j: (0, 0)),
            pl.BlockSpec((D_MODEL, blk), lambda j: (0, j)),
            pl.BlockSpec((1, blk), lambda j: (0, j)),
        ],
        out_specs=pl.BlockSpec((rows, blk), lambda j: (0, j)),
        out_shape=jax.ShapeDtypeStruct((rows, n), F32),
        compiler_params=_params("arbitrary"),
        name="modulation",
    )(cond, w_mod, b_mod.reshape(1, n))


def _rope128(t, c_tab, s_tab):
    return t * c_tab + pltpu.roll(t, 64, axis=1) * s_tab


def _inproj_kernel(x_ref, sh_ref, sc_ref, c_ref, s_ref, wcq_ref, wckv_ref, wsu_ref, wgm_ref, wgs_ref,
                   qg_ref, kvg_ref, wuq_ref, wukv_ref, *out_refs, is_ctx):
    if is_ctx:
        k_ref, v_ref, su_ref = out_refs
    else:
        q_ref, k_ref, v_ref, su_ref, gm_ref, gs_ref = out_refs
    xm = _ln_plain(x_ref[0]) * (1.0 + sc_ref[0]) + sh_ref[0]
    xb = xm.astype(BF16)
    c_tab = c_ref[...]
    s_tab = s_ref[...]

    ckvpe = _dot(xb, wckv_ref[...])
    ckv_n = _rms(ckvpe[:, :KV_RANK], kvg_ref[...]).astype(BF16)
    kpe = _rope128(ckvpe[:, KV_RANK:], c_tab, s_tab).astype(BF16)
    kv = _dot(ckv_n, wukv_ref[...])
    for h in range(N_HEADS):
        k_ref[0, h, :, 0:128] = kv[:, h * 256:h * 256 + 128].astype(BF16)
        k_ref[0, h, :, 128:256] = kpe
        v_ref[0, h] = kv[:, h * 256 + 128:(h + 1) * 256].astype(BF16)

    su_ref[...] = _dot(xb, wsu_ref[...])

    if not is_ctx:
        cq_n = _rms(_dot(xb, wcq_ref[...]), qg_ref[...]).astype(BF16)
        q = _dot(cq_n, wuq_ref[...])
        for h in range(N_HEADS):
            q_ref[0, h, :, 0:128] = (q[:, h * 256:h * 256 + 128] * ATT_SCALE).astype(BF16)
            q_ref[0, h, :, 128:256] = (_rope128(q[:, h * 256 + 128:(h + 1) * 256], c_tab, s_tab) * ATT_SCALE).astype(BF16)
        gm_ref[0] = jax.nn.sigmoid(_dot(xb, wgm_ref[...])).astype(BF16)
        gs_ref[0] = jax.nn.sigmoid(_dot(xb, wgs_ref[...])).astype(BF16)


def _inproj(x, shift, scale, c_tab, s_tab, wts, *, is_ctx, tm, b0, bsz):
    length = x.shape[1]
    mod_map = (lambda b, t: (0, 0, 0)) if shift.shape[0] == 1 else (lambda b, t: (b0 + b, 0, 0))
    const2 = lambda b, t: (0, 0)
    wcq, wckv, wsu, wgm, wgs, qg, kvg, wuq, wukv = wts
    in_specs = [
        pl.BlockSpec((1, tm, D_MODEL), lambda b, t: (b0 + b, t, 0)),
        pl.BlockSpec((1, 1, D_MODEL), mod_map),
        pl.BlockSpec((1, 1, D_MODEL), mod_map),
        pl.BlockSpec((tm, 128), lambda b, t: (t, 0)),
        pl.BlockSpec((tm, 128), lambda b, t: (t, 0)),
    ] + [pl.BlockSpec(w.shape, const2) for w in (wcq, wckv, wsu, wgm, wgs, qg, kvg, wuq, wukv)]
    head_spec = lambda width: pl.BlockSpec((1, N_HEADS, tm, width), lambda b, t: (b, 0, t, 0))
    row_spec = pl.BlockSpec((1, tm, D_MODEL), lambda b, t: (b, t, 0))
    su_spec = pl.BlockSpec((tm, S5_WIDTH), lambda b, t: (t, b))
    k_shape = jax.ShapeDtypeStruct((bsz, N_HEADS, length, HEAD_PAD), BF16)
    v_shape = jax.ShapeDtypeStruct((bsz, N_HEADS, length, V_HEAD_DIM), BF16)
    su_shape = jax.ShapeDtypeStruct((length, bsz * S5_WIDTH), F32)
    g_shape = jax.ShapeDtypeStruct((bsz, length, D_MODEL), BF16)
    if is_ctx:
        out_specs = [head_spec(HEAD_PAD), head_spec(V_HEAD_DIM), su_spec]
        out_shape = [k_shape, v_shape, su_shape]
    else:
        out_specs = [head_spec(HEAD_PAD), head_spec(HEAD_PAD), head_spec(V_HEAD_DIM), su_spec, row_spec, row_spec]
        out_shape = [k_shape, k_shape, v_shape, su_shape, g_shape, g_shape]
    return pl.pallas_call(
        functools.partial(_inproj_kernel, is_ctx=is_ctx),
        grid=(bsz, length // tm),
        in_specs=in_specs,
        out_specs=out_specs,
        out_shape=out_shape,
        compiler_params=_params("parallel", "parallel"),
        name="inproj_ctx" if is_ctx else "inproj_lat",
    )(x, shift, scale, c_tab, s_tab, wcq, wckv, wsu, wgm, wgs, qg, kvg, wuq, wukv)


def _attn_kernel(q_ref, kc_ref, vc_ref, kl_ref, vl_ref, o_ref):
    q = q_ref[0, 0]
    s_c = _dot_nt(q, kc_ref[0, 0])
    s_l = _dot_nt(q, kl_ref[0, 0])
    m = jnp.maximum(jnp.max(s_c, axis=-1, keepdims=True), jnp.max(s_l, axis=-1, keepdims=True))
    p_c = jnp.exp(s_c - m)
    p_l = jnp.exp(s_l - m)
    denom = jnp.sum(p_c, axis=-1, keepdims=True) + jnp.sum(p_l, axis=-1, keepdims=True)
    o = _dot(p_c.astype(BF16), vc_ref[0, 0]) + _dot(p_l.astype(BF16), vl_ref[0, 0])
    o_ref[0] = (o / denom).astype(BF16)


def _attention(q, k_ctx, v_ctx, k_lat, v_lat, *, tq, b0, bsz):
    _, heads, length, _ = q.shape
    l_ctx = k_ctx.shape[2]
    full = lambda b, h, i: (b0 + b, h, 0, 0)
    return pl.pallas_call(
        _attn_kernel,
        grid=(bsz, heads, length // tq),
        in_specs=[
            pl.BlockSpec((1, 1, tq, HEAD_PAD), lambda b, h, i: (b0 + b, h, i, 0)),
            pl.BlockSpec((1, 1, l_ctx, HEAD_PAD), full),
            pl.BlockSpec((1, 1, l_ctx, V_HEAD_DIM), full),
            pl.BlockSpec((1, 1, length, HEAD_PAD), full),
            pl.BlockSpec((1, 1, length, V_HEAD_DIM), full),
        ],
        out_specs=pl.BlockSpec((1, tq, V_HEAD_DIM), lambda b, h, i: (b, i, h)),
        out_shape=jax.ShapeDtypeStruct((bsz, length, heads * V_HEAD_DIM), BF16),
        compiler_params=_params("parallel", "parallel", "arbitrary"),
        name="attention",
    )(q, k_ctx, v_ctx, k_lat, v_lat)


def _s5_kernel(u_ref, are_ref, aim_ref, bre_ref, bim_ref, cre_ref, cim_ref, y_ref,
               bu_re, bu_im, h_re, h_im, *, tt, bsz, n_ctx_tiles):
    d = pl.program_id(0)
    i = pl.program_id(1)

    @pl.when(i == 0)
    def _():
        h_re[...] = jnp.zeros_like(h_re)
        h_im[...] = jnp.zeros_like(h_im)

    u = u_ref[...].astype(BF16)
    half = S5_COLS // S5_CHUNKS
    cw = S5_WIDTH // S5_CHUNKS
    for c in range(S5_CHUNKS):
        uc = u[:, c * cw:(c + 1) * cw]
        bu_re[:, c * half:(c + 1) * half] = _dot(uc, bre_ref[0, c])
        bu_im[:, c * half:(c + 1) * half] = _dot(uc, bim_ref[0, c])

    col_w = 512
    for cc in range(S5_COLS // col_w):
        cols = slice(cc * col_w, (cc + 1) * col_w)
        a_r = are_ref[0, :, cols]
        a_i = aim_ref[0, :, cols]

        def step(t, carry, cols=cols, a_r=a_r, a_i=a_i):
            hr, hi = carry
            pos = jnp.where(d == 0, t, tt - 1 - t)
            r = pl.multiple_of(pos * bsz, bsz)
            nr = a_r * hr - a_i * hi + bu_re[pl.ds(r, bsz), cols]
            ni = a_r * hi + a_i * hr + bu_im[pl.ds(r, bsz), cols]
            bu_re[pl.ds(r, bsz), cols] = nr
            bu_im[pl.ds(r, bsz), cols] = ni
            return nr, ni

        hr, hi = lax.fori_loop(0, tt, step, (h_re[:, cols], h_im[:, cols]))
        h_re[:, cols] = hr
        h_im[:, cols] = hi

    @pl.when(i >= n_ctx_tiles)
    def _():
        for c in range(S5_CHUNKS):
            sl = slice(c * half, (c + 1) * half)
            y = _dot(bu_re[:, sl].astype(BF16), cre_ref[0, c]) + _dot(bu_im[:, sl].astype(BF16), cim_ref[0, c])
            y_ref[0, :, c * cw:(c + 1) * cw] = y


def _s5_scan(u_all, mats, *, bsz, l_ctx, tt):
    a_re_b, a_im_b, b_re_bd, b_im_bd, c_re_bd, c_im_bd = mats
    rows = tt * bsz
    n_tiles = u_all.shape[0] // rows
    n_ctx_tiles = l_ctx // tt
    n_lat_tiles = n_tiles - n_ctx_tiles

    def u_map(d, i):
        bwd = jnp.where(i < n_ctx_tiles, n_ctx_tiles - 1 - i, n_tiles - 1 - (i - n_ctx_tiles))
        return (jnp.where(d == 0, i, bwd), 0)

    def y_map(d, i):
        j = jnp.maximum(i - n_ctx_tiles, 0)
        return (d, jnp.where(d == 0, j, n_lat_tiles - 1 - j), 0)

    dmap3 = lambda d, i: (d, 0, 0)
    dmap4 = lambda d, i: (d, 0, 0, 0)
    return pl.pallas_call(
        functools.partial(_s5_kernel, tt=tt, bsz=bsz, n_ctx_tiles=n_ctx_tiles),
        grid=(2, n_tiles),
        in_specs=[
            pl.BlockSpec((rows, S5_WIDTH), u_map),
            pl.BlockSpec((1,) + a_re_b.shape[1:], dmap3),
            pl.BlockSpec((1,) + a_im_b.shape[1:], dmap3),
            pl.BlockSpec((1,) + b_re_bd.shape[1:], dmap4),
            pl.BlockSpec((1,) + b_im_bd.shape[1:], dmap4),
            pl.BlockSpec((1,) + c_re_bd.shape[1:], dmap4),
            pl.BlockSpec((1,) + c_im_bd.shape[1:], dmap4),
        ],
        out_specs=pl.BlockSpec((1, rows, S5_WIDTH), y_map),
        out_shape=jax.ShapeDtypeStruct((2, n_lat_tiles * rows, S5_WIDTH), F32),
        scratch_shapes=[
            pltpu.VMEM((rows, S5_COLS), F32),
            pltpu.VMEM((rows, S5_COLS), F32),
            pltpu.VMEM((bsz, S5_COLS), F32),
            pltpu.VMEM((bsz, S5_COLS), F32),
        ],
        compiler_params=_params("arbitrary", "arbitrary"),
        name="s5_scan",
    )(u_all, a_re_b, a_im_b, b_re_bd, b_im_bd, c_re_bd, c_im_bd)


def _s5_matrices(a_re, a_im, log_dt, b_re, b_im, c_re, c_im, bsz):
    dt = jnp.exp(log_dt)[..., None]
    mag = jnp.exp(a_re * dt)
    ab_re = mag * jnp.cos(a_im * dt)
    ab_im = mag * jnp.sin(a_im * dt)
    den = a_re * a_re + a_im * a_im
    f_re = ((ab_re - 1.0) * a_re + ab_im * a_im) / den
    f_im = (ab_im * a_re - (ab_re - 1.0) * a_im) / den
    bb_re = f_re[..., None] * b_re - f_im[..., None] * b_im
    bb_im = f_re[..., None] * b_im + f_im[..., None] * b_re
    gl = S5_GROUPS // S5_CHUNKS
    eye = jnp.eye(gl, dtype=F32)

    def in_bd(bb):
        t = bb.reshape(2, S5_CHUNKS, gl, S5_STATE, S5_GROUP)
        t = jnp.einsum('dkgpc,gh->dkgchp', t, eye)
        return t.reshape(2, S5_CHUNKS, gl * S5_GROUP, gl * S5_STATE).astype(BF16)

    def out_bd(cc):
        t = cc.reshape(2, S5_CHUNKS, gl, S5_GROUP, S5_STATE)
        t = jnp.einsum('dkgcp,gh->dkgphc', t, eye)
        return t.reshape(2, S5_CHUNKS, gl * S5_STATE, gl * S5_GROUP).astype(BF16)

    bcast = lambda a: jnp.broadcast_to(a.reshape(2, 1, S5_COLS), (2, bsz, S5_COLS))
    return bcast(ab_re), bcast(ab_im), in_bd(bb_re), in_bd(bb_im), out_bd(c_re), out_bd(-c_im)


def _merge_kernel(x_ref, att_ref, gm_ref, gs_ref, su_ref, y_ref, dsk_ref, wglu_ref, wout_ref, ga_ref,
                  l1g_ref, l1b_ref, shf_ref, scf_ref, wq_ref, keys_ref, h1_ref, xm_ref, st_ref):
    y = su_ref[...] * dsk_ref[...] + y_ref[0] + y_ref[1]
    gl = _dot(_gelu(y).astype(BF16), wglu_ref[...])
    s5_out = gl[:, :D_MODEL] * jax.nn.sigmoid(gl[:, D_MODEL:])
    merged = gm_ref[0].astype(F32) * att_ref[0].astype(F32) + gs_ref[0].astype(F32) * s5_out
    out = _dot(merged.astype(BF16), wout_ref[...])
    h1 = _ln_plain(DEEPNORM_ALPHA * x_ref[0] + ga_ref[0] * out) * l1g_ref[...] + l1b_ref[...]
    h1_ref[0] = h1
    xm = _ln_plain(h1) * (1.0 + scf_ref[0]) + shf_ref[0]
    xm_ref[0] = xm
    qp = _dot(xm.astype(BF16), wq_ref[...]).astype(BF16)
    for j in range(2 * PEER_HEADS):
        st_ref[j] = _dot_nt(keys_ref[j], qp[:, j * PEER_HALF:(j + 1) * PEER_HALF])


def _merge(x, att, gm, gs, su_all2d, y2d, dsk, wglu, wout, g_a, l1g, l1b, sh_f, sc_f, wq, keys, *, tm, ctx_tiles, b0, h0):
    bsz, length, _ = att.shape
    nt = length // tm
    row = pl.BlockSpec((1, tm, D_MODEL), lambda b, t: (b, t, 0))
    xrow = pl.BlockSpec((1, tm, D_MODEL), lambda b, t: (b0 + b, t, 0))
    hrow = pl.BlockSpec((1, tm, D_MODEL), lambda b, t: (h0 + b, t, 0))
    modb = pl.BlockSpec((1, 1, D_MODEL), lambda b, t: (b0 + b, 0, 0))
    const = lambda a: pl.BlockSpec(a.shape, lambda b, t: (0,) * a.ndim)
    return pl.pallas_call(
        _merge_kernel,
        grid=(bsz, nt),
        in_specs=[
            xrow, row, hrow, hrow,
            pl.BlockSpec((tm, S5_WIDTH), lambda b, t: (ctx_tiles + t, h0 + b)),
            pl.BlockSpec((2, tm, S5_WIDTH), lambda b, t: (0, t, h0 + b)),
            const(dsk), const(wglu), const(wout), modb, const(l1g), const(l1b), modb, modb, const(wq), const(keys),
        ],
        out_specs=[row, row, pl.BlockSpec((2 * PEER_HEADS, PEER_N_KEYS, tm), lambda b, t: (0, 0, b * nt + t))],
        out_shape=[
            jax.ShapeDtypeStruct((bsz, length, D_MODEL), F32),
            jax.ShapeDtypeStruct((bsz, length, D_MODEL), F32),
            jax.ShapeDtypeStruct((2 * PEER_HEADS, PEER_N_KEYS, bsz * length), F32),
        ],
        compiler_params=_params("parallel", "parallel"),
        name="merge_peer_query",
    )(x, att, gm, gs, su_all2d, y2d, dsk, wglu, wout, g_a, l1g, l1b, sh_f, sc_f, wq, keys)


def _take_top(vals, codes, payload, k):
    rows = lax.broadcasted_iota(jnp.int32, (k, vals.shape[1]), 0)
    top_v = jnp.zeros((k, vals.shape[1]), F32)
    top_p = jnp.zeros((k, vals.shape[1]), F32)
    for r in range(k):
        m = jnp.max(vals, axis=0, keepdims=True)
        cm = jnp.min(jnp.where(vals == m, codes, CODE_NONE), axis=0, keepdims=True)
        sel = codes == cm
        if payload is None:
            p = cm
        else:
            p = jnp.max(jnp.where(sel, payload, -1.0), axis=0, keepdims=True)
        vals = jnp.where(sel, -jnp.inf, vals)
        top_v = jnp.where(rows == r, m, top_v)
        top_p = jnp.where(rows == r, p, top_p)
    return top_v, top_p


def _topk_kernel(st_ref, e_ref, g_ref):
    lanes = st_ref.shape[2]
    key_idx = lax.broadcasted_iota(jnp.int32, (PEER_N_KEYS, lanes), 0).astype(F32)
    sub_idx = lax.broadcasted_iota(jnp.int32, (PEER_TOPK, lanes), 0).astype(F32)
    half_k = PEER_TOPK // 2
    for h in range(PEER_HEADS):
        sv0, si0 = _take_top(st_ref[2 * h], key_idx, None, PEER_TOPK)
        sv1, si1 = _take_top(st_ref[2 * h + 1], key_idx, None, PEER_TOPK)
        cs, ce, cc = [], [], []
        for i in range(half_k):
            nj = PEER_TOPK if i == 0 else half_k
            cs.append(sv0[i:i + 1, :] + sv1[0:nj, :])
            ce.append(si0[i:i + 1, :] * PEER_N_KEYS + si1[0:nj, :])
            cc.append(sub_idx[0:nj, :] + i * PEER_TOPK)
        cs.append(sv0[half_k:, :] + sv1[0:1, :])
        ce.append(si0[half_k:, :] * PEER_N_KEYS + si1[0:1, :])
        cc.append((sub_idx[0:half_k, :] + half_k) * PEER_TOPK)
        top_s, top_e = _take_top(jnp.concatenate(cs, axis=0), jnp.concatenate(cc, axis=0),
                                 jnp.concatenate(ce, axis=0), PEER_TOPK)
        p = jnp.exp(top_s - jnp.max(top_s, axis=0, keepdims=True))
        g_ref[h] = p / jnp.sum(p, axis=0, keepdims=True)
        e_ref[h] = top_e.astype(jnp.int32)


def _peer_topk(st, *, lanes):
    n = st.shape[2]
    out_spec = pl.BlockSpec((PEER_HEADS, PEER_TOPK, lanes), lambda i: (0, 0, i))
    return pl.pallas_call(
        _topk_kernel,
        grid=(n // lanes,),
        in_specs=[pl.BlockSpec((2 * PEER_HEADS, PEER_N_KEYS, lanes), lambda i: (0, 0, i))],
        out_specs=[out_spec, out_spec],
        out_shape=[jax.ShapeDtypeStruct((PEER_HEADS, PEER_TOPK, n), jnp.int32),
                   jax.ShapeDtypeStruct((PEER_HEADS, PEER_TOPK, n), F32)],
        compiler_params=_params("parallel"),
        name="peer_topk",
    )(st)


def _sc_peer(u_tab, v_tab, idx, x, gates):
    n_tok = x.shape[0]
    per_worker = n_tok // SC_WORKERS
    assert per_worker % 2 == 0 and per_worker >= 4
    assert len(STAGE_STEPS) == (PEER_PAIRS - SC_PAIRS) // SC_LANES and len(SUM_STEPS) == SC_PAIRS // SC_LANES
    steps = PEER_PAIRS // SC_LANES
    assert steps % U_RING == 0
    n_vc = SC_PAIRS // SC_LANES
    n_vs = (PEER_PAIRS - SC_PAIRS) // SC_LANES
    tc_pairs = PEER_PAIRS - SC_PAIRS
    chunks = HALF_W // SC_LANES
    mesh = plsc.VectorSubcoreMesh(core_axis_name="c", subcore_axis_name="s")

    @functools.partial(
        pl.kernel, mesh=mesh,
        out_type=(jax.ShapeDtypeStruct((n_tok * PEER_PAIRS,), F32),
                  jax.ShapeDtypeStruct((n_tok * tc_pairs, HALF_W), jnp.int32),
                  jax.ShapeDtypeStruct((n_tok, 2 * HALF_W), F32)),
        scratch_types=[
            pltpu.VMEM((2, PEER_PAIRS), jnp.int32),
            pltpu.VMEM((2, PEER_PAIRS), jnp.int32),
            pltpu.VMEM((2, 2 * HALF_W), F32),
            pltpu.VMEM((2, SC_PAIRS), F32),
            pltpu.VMEM((2, SC_PAIRS), F32),
            pltpu.VMEM((2, PEER_PAIRS), F32),
            pltpu.VMEM((2, 2 * HALF_W), F32),
            pltpu.VMEM((U_RING, SC_LANES, HALF_W), jnp.int32),
            pltpu.VMEM((n_vc, SC_LANES, HALF_W), jnp.int32),
            pltpu.VMEM((n_vs, SC_LANES, HALF_W), jnp.int32),
            pltpu.SemaphoreType.DMA((U_RING,)),
            pltpu.SemaphoreType.DMA((n_vc,)),
            pltpu.SemaphoreType.DMA((n_vs,)),
            pltpu.SemaphoreType.DMA((n_vs,)),
            pltpu.SemaphoreType.DMA((2,)),
            pltpu.SemaphoreType.DMA((2,)),
            pltpu.SemaphoreType.DMA((2,)),
            pltpu.SemaphoreType.DMA((2,)),
            pltpu.SemaphoreType.DMA((2,)),
        ],
        compiler_params=pltpu.CompilerParams(needs_layout_passes=False),
    )
    def peer(u_hbm, v_hbm, idx_hbm, x_hbm, gate_hbm, act_hbm, vrows_hbm, f_hbm,
             idx_v, vidx_v, x_v, gate_v, coef_v, act_v, f_v, urows_v, vsum_v, vstage_v,
             u_sems, vc_sems, vg_sems, vw_sems, idx_sems, x_sems, gate_sems, act_sems, f_sems):
        wid = lax.axis_index("s") * 2 + lax.axis_index("c")
        base = wid * per_worker
        last = base + per_worker - 1
        lane = lax.iota(jnp.int32, SC_LANES)

        def pair_off(tok):
            return pl.multiple_of(tok * PEER_PAIRS, PEER_PAIRS)

        def idx_copy(tok, slot):
            return pltpu.make_async_copy(idx_hbm.at[pl.ds(pair_off(tok), PEER_PAIRS)], idx_v.at[slot], idx_sems.at[slot])

        def x_copy(tok, slot):
            return pltpu.make_async_copy(x_hbm.at[tok], x_v.at[slot], x_sems.at[slot])

        def gate_copy(tok, slot):
            return pltpu.make_async_copy(gate_hbm.at[pl.ds(pair_off(tok), SC_PAIRS)], gate_v.at[slot], gate_sems.at[slot])

        def u_gather(slot, g):
            return pltpu.make_async_copy(u_hbm.at[idx_v.at[slot, pl.ds(g * SC_LANES, SC_LANES)]], urows_v.at[g % U_RING],
                                         u_sems.at[g % U_RING])

        def vsum_gather(slot, j):
            return pltpu.make_async_copy(v_hbm.at[vidx_v.at[slot, pl.ds(j * SC_LANES, SC_LANES)]], vsum_v.at[j],
                                         vc_sems.at[j])

        def vstage_gather(slot, s):
            return pltpu.make_async_copy(v_hbm.at[vidx_v.at[slot, pl.ds(SC_PAIRS + s * SC_LANES, SC_LANES)]],
                                         vstage_v.at[s], vg_sems.at[s])

        def vstage_write(tok, s):
            row0 = pl.multiple_of(tok * tc_pairs + s * SC_LANES, SC_LANES)
            return pltpu.make_async_copy(vstage_v.at[s], vrows_hbm.at[pl.ds(row0, SC_LANES)], vw_sems.at[s])

        def act_copy(tok, slot):
            return pltpu.make_async_copy(act_v.at[slot], act_hbm.at[pl.ds(pair_off(tok), PEER_PAIRS)], act_sems.at[slot])

        def f_copy(tok, slot):
            return pltpu.make_async_copy(f_v.at[slot], f_hbm.at[tok], f_sems.at[slot])

        def unpack(wv):
            return (lax.bitcast_convert_type(wv << 16, F32), lax.bitcast_convert_type(wv & jnp.int32(-65536), F32))

        def step_dots(slot, g):
            v = jnp.zeros((SC_LANES,), F32)
            for r0 in range(0, SC_LANES, DOT_ROWS):
                def body(c, accs, r0=r0):
                    xlo = x_v[slot, pl.ds(c * SC_LANES, SC_LANES)]
                    xhi = x_v[slot, pl.ds(HALF_W + c * SC_LANES, SC_LANES)]
                    new = []
                    for r in range(DOT_ROWS):
                        lo, hi = unpack(urows_v[g % U_RING, r0 + r, pl.ds(c * SC_LANES, SC_LANES)])
                        new.append(accs[r] + lo * xlo + hi * xhi)
                    return tuple(new)

                accs = plsc.parallel_loop(0, chunks, unroll=SC_UNROLL,
                                          carry=tuple(jnp.zeros((SC_LANES,), F32) for _ in range(DOT_ROWS)))(body)
                for r in range(DOT_ROWS):
                    v = jnp.where(lane == r0 + r, jnp.sum(accs[r]), v)
            act_v[slot, pl.ds(g * SC_LANES, SC_LANES)] = v

        def window_sum(fslot, j):
            cvec = coef_v[fslot, pl.ds(j * SC_LANES, SC_LANES)]
            cb = [jnp.zeros((SC_LANES,), F32) + jnp.sum(jnp.where(lane == r, cvec, 0.0)) for r in range(SC_LANES)]

            def body(c):
                alo = f_v[fslot, pl.ds(c * SC_LANES, SC_LANES)]
                ahi = f_v[fslot, pl.ds(HALF_W + c * SC_LANES, SC_LANES)]
                for r in range(SC_LANES):
                    lo, hi = unpack(vsum_v[j, r, pl.ds(c * SC_LANES, SC_LANES)])
                    alo = alo + cb[r] * lo
                    ahi = ahi + cb[r] * hi
                f_v[fslot, pl.ds(c * SC_LANES, SC_LANES)] = alo
                f_v[fslot, pl.ds(HALF_W + c * SC_LANES, SC_LANES)] = ahi

            plsc.parallel_loop(0, chunks, unroll=SC_UNROLL)(body)

        def token_coefs(slot):
            for q in range(n_vc):
                a = act_v[slot, pl.ds(q * SC_LANES, SC_LANES)]
                z2 = 1.5957691216057308 * (a + 0.044715 * (a * a * a))
                coef_v[slot, pl.ds(q * SC_LANES, SC_LANES)] = gate_v[slot, pl.ds(q * SC_LANES, SC_LANES)] * a / (1.0 + jnp.exp(-z2))

        def zero_f(fslot):
            for q in range(2 * HALF_W // SC_LANES):
                f_v[fslot, pl.ds(q * SC_LANES, SC_LANES)] = jnp.zeros((SC_LANES,), F32)

        idx_copy(base, 0).start()
        x_copy(base, 0).start()
        gate_copy(base, 0).start()
        idx_copy(base, 0).wait()
        for g in range(U_RING):
            u_gather(0, g).start()

        @pl.loop(0, per_worker, step=2)
        def _(i):
            for slot in range(2):
                tok = base + i + slot
                nxt = jnp.minimum(tok + 1, last)
                other = 1 - slot
                first = slot == 0
                idx_copy(nxt, other).start()
                x_copy(nxt, other).start()
                gate_copy(nxt, other).start()
                for q in range(PEER_PAIRS // SC_LANES):
                    vidx_v[slot, pl.ds(q * SC_LANES, SC_LANES)] = idx_v[slot, pl.ds(q * SC_LANES, SC_LANES)]
                x_copy(tok, slot).wait()
                gate_copy(tok, slot).wait()

                @pl.when(i > 0)
                def _():
                    act_copy(tok, slot).wait()

                if first:
                    @pl.when(i > 2)
                    def _():
                        f_copy(tok, other).wait()
                else:
                    @pl.when(i > 0)
                    def _():
                        f_copy(tok, other).wait()
                zero_f(other)

                idx_copy(nxt, other).wait()
                for g in range(steps):
                    s = STAGE_STEPS.index(g) if g in STAGE_STEPS else None
                    if s is not None:
                        if first:
                            @pl.when(i > 0)
                            def _(s=s, tok=tok):
                                vstage_write(tok, s).wait()
                        else:
                            vstage_write(tok, s).wait()
                        vstage_gather(slot, s).start()
                    u_gather(slot, g).wait()
                    step_dots(slot, g)
                    if g + U_RING < steps:
                        u_gather(slot, g + U_RING).start()
                    else:
                        u_gather(other, g + U_RING - steps).start()
                    if s is not None:
                        sp = (s - STAGE_LAG) % n_vs
                        tokp, slotp = (tok, slot) if s >= STAGE_LAG else (tok - 1, other)

                        def finish(sp=sp, tokp=tokp, slotp=slotp):
                            vstage_gather(slotp, sp).wait()
                            vstage_write(tokp, sp).start()

                        if first and s < STAGE_LAG:
                            pl.when(i > 0)(finish)
                        else:
                            finish()
                    if g in SUM_STEPS:
                        j = SUM_STEPS.index(g)

                        def consume(j=j, other=other):
                            vsum_gather(other, j).wait()
                            window_sum(other, j)

                        if first:
                            pl.when(i > 0)(consume)
                        else:
                            consume()
                        vsum_gather(slot, j).start()
                token_coefs(slot)
                act_copy(tok, slot).start()

                def send_prev(tok=tok, other=other):
                    f_copy(tok - 1, other).start()

                if first:
                    pl.when(i > 0)(send_prev)
                else:
                    send_prev()

        for s in range(n_vs - STAGE_LAG, n_vs):
            vstage_gather(1, s).wait()
            vstage_write(last, s).start()
        f_copy(last, 1).wait()
        zero_f(1)
        for j in range(n_vc):
            vsum_gather(1, j).wait()
            window_sum(1, j)
        f_copy(last, 1).start()
        for s in range(n_vs):
            vstage_write(last, s).wait()
        for g in range(U_RING):
            u_gather(0, g).wait()
        x_copy(last, 0).wait()
        gate_copy(last, 0).wait()
        for slot in range(2):
            f_copy(last, slot).wait()
            act_copy(last, slot).wait()

    return peer(u_tab, v_tab, idx, x, gates)


def _peer_kernel(vg_ref, act_ref, gate_ref, t2_ref, mask_ref, fsc_ref, h1_ref, gf_ref, l2g_ref, l2b_ref, o_ref, f_ref, *, tb):
    sub = 8
    span = sub * TC_PAIRS
    half = D_MODEL // 2
    for s in range(tb // sub):
        rows = slice(s * span, (s + 1) * span)
        toks = slice(s * sub, (s + 1) * sub)
        coef = (gate_ref[toks, :] * _gelu(act_ref[toks, :])).astype(BF16)
        spread = _dot(coef, t2_ref[...])
        c2 = (jnp.concatenate([spread, spread], axis=0) * mask_ref[...]).astype(BF16)
        f2 = _dot(c2, pltpu.bitcast(vg_ref[rows, :], BF16))
        f_ref[toks, :half] = f2[:sub]
        f_ref[toks, half:] = f2[sub:]
    r = DEEPNORM_ALPHA * h1_ref[...] + gf_ref[0] * (f_ref[...] + fsc_ref[...])
    o_ref[...] = _ln_plain(r) * l2g_ref[...] + l2b_ref[...]


def _peer_eval(vg, act, gate, t2, mask, f_sc, h1, g_f, l2g, l2b, *, tb, tok0, length):
    n = h1.shape[0]
    tok_blk0 = tok0 // tb
    per_b = length // tb
    tok_spec = lambda width: pl.BlockSpec((tb, width), lambda i: (i, 0))
    return pl.pallas_call(
        functools.partial(_peer_kernel, tb=tb),
        grid=(n // tb,),
        in_specs=[
            pl.BlockSpec((tb * TC_PAIRS, D_MODEL // 2), lambda i: (i, 0)),
            tok_spec(PEER_PAIRS), tok_spec(PEER_PAIRS),
            pl.BlockSpec(t2.shape, lambda i: (0, 0)), pl.BlockSpec(mask.shape, lambda i: (0, 0)),
            tok_spec(D_MODEL), tok_spec(D_MODEL),
            pl.BlockSpec((1, 1, D_MODEL), lambda i: ((tok_blk0 + i) // per_b, 0, 0)),
            pl.BlockSpec((1, D_MODEL), lambda i: (0, 0)),
            pl.BlockSpec((1, D_MODEL), lambda i: (0, 0)),
        ],
        out_specs=tok_spec(D_MODEL),
        out_shape=jax.ShapeDtypeStruct((n, D_MODEL), F32),
        scratch_shapes=[pltpu.VMEM((tb, D_MODEL), F32)],
        compiler_params=_params("parallel"),
        name="peer_eval",
    )(vg, act, gate, t2, mask, f_sc, h1, g_f, l2g, l2b)


def _rope_perm():
    half = ROPE_AXIS_DIM // 2
    base = jnp.arange(QK_ROPE_DIM)
    return jnp.where((base % ROPE_AXIS_DIM) < half, base + half, base - half)


def _rope_tables(length):
    pos = jnp.arange(length)
    row = (pos // GRID_W).astype(F32)
    col = (pos % GRID_W).astype(F32)
    inv_freq = jnp.power(ROPE_BASE, -jnp.arange(0, ROPE_AXIS_DIM, 2, dtype=F32) / ROPE_AXIS_DIM)
    ang_r = row[:, None] * inv_freq
    ang_c = col[:, None] * inv_freq
    zeros = jnp.zeros((length, QK_ROPE_DIM), F32)
    c_tab = jnp.concatenate([jnp.cos(ang_r), jnp.cos(ang_r), jnp.cos(ang_c), jnp.cos(ang_c), zeros], axis=1)
    s_tab = jnp.concatenate([-jnp.sin(ang_r), jnp.sin(ang_r), -jnp.sin(ang_c), jnp.sin(ang_c), zeros], axis=1)
    return c_tab, s_tab


def _identity_tables(length):
    ones = jnp.ones((length, QK_ROPE_DIM), F32)
    zeros = jnp.zeros((length, QK_ROPE_DIM), F32)
    return jnp.concatenate([ones, zeros], axis=1), jnp.zeros((length, 2 * QK_ROPE_DIM), F32)


def _spread_consts():
    lanes = jnp.arange(16 * TC_PAIRS)
    pair = (lanes % (2 * TC_PAIRS)) // 2 + SC_PAIRS
    t2 = (jnp.arange(PEER_PAIRS)[:, None] == pair[None, :]).astype(BF16)
    r = jnp.arange(16)
    own = (lanes // (2 * TC_PAIRS))[None, :] == (r % 8)[:, None]
    parity = (lanes % 2 == 0)[None, :] == (r < 8)[:, None]
    return t2, (own & parity).astype(F32)


def _pack_rows(table):
    bits = lax.bitcast_convert_type(table.astype(BF16), jnp.uint16).astype(jnp.uint32)
    return lax.bitcast_convert_type(bits[:, :HALF_W] | (bits[:, HALF_W:] << 16), jnp.int32)


def _layer_weights(w_in, q_norm_g, kv_norm_g, w_uq, w_ukv):
    perm = _rope_perm()
    p0 = Q_RANK
    p1 = p0 + KV_RANK
    p2 = p1 + QK_ROPE_DIM
    p3 = p2 + S5_WIDTH
    p4 = p3 + D_MODEL
    w_kpe = w_in[:, p1:p2]
    wckv = jnp.concatenate([w_in[:, p0:p1], w_kpe, w_kpe[:, perm]], axis=1)
    uq = w_uq.reshape(Q_RANK, N_HEADS, QK_NOPE_DIM + QK_ROPE_DIM)
    pe = uq[:, :, QK_NOPE_DIM:]
    wuq = jnp.concatenate([uq[:, :, :QK_NOPE_DIM], pe, pe[:, :, perm]], axis=2).reshape(Q_RANK, N_HEADS * HEAD_PAD)
    cast = lambda a: a.astype(BF16)
    return (cast(w_in[:, :p0]), cast(wckv), cast(w_in[:, p2:p3]), cast(w_in[:, p3:p4]), cast(w_in[:, p4:]),
            q_norm_g.reshape(1, Q_RANK), kv_norm_g.reshape(1, KV_RANK), cast(wuq), cast(w_ukv))


def kernel(x, c, ctx, c_ctx, w_mod, b_mod, w_in, q_norm_g, kv_norm_g, w_uq, w_ukv, s5_a_re, s5_a_im, s5_log_dt, s5_b_re, s5_b_im, s5_c_re, s5_c_im, s5_d, w_glu, w_out, ln1_g, ln1_b, peer_wq, peer_keys, peer_u, peer_v, ln2_g, ln2_b):
    bsz, length, _ = x.shape
    l_ctx = ctx.shape[1]
    layer = 0
    tm = ROW_TILE
    s5_tt = S5_TIME_TILE
    peer_tb = EVAL_TOKENS

    cond = jnp.concatenate([c, c_ctx[None, :], jnp.zeros((7, D_MODEL), F32)], axis=0)
    mod = _modulation(cond, w_mod[layer], b_mod[layer]).reshape(cond.shape[0], N_MOD, 1, D_MODEL)
    sh_a, sc_a, g_a, sh_f, sc_f, g_f = (mod[:bsz, i] for i in range(N_MOD))
    csh_a, csc_a = mod[bsz:bsz + 1, 0], mod[bsz:bsz + 1, 1]

    wts = _layer_weights(w_in[layer], q_norm_g[layer], kv_norm_g[layer], w_uq[layer], w_ukv[layer])
    c_lat, s_lat = _rope_tables(length)
    c_id, s_id = _identity_tables(l_ctx)
    keys = peer_keys[layer].reshape(2 * PEER_HEADS, PEER_N_KEYS, PEER_HALF).astype(BF16)
    wglu, wout, wq = w_glu[layer].astype(BF16), w_out[layer].astype(BF16), peer_wq[layer].astype(BF16)
    u_pack = _pack_rows(peer_u[layer])
    v_pack = _pack_rows(peer_v[layer])
    l2g = ln2_g[layer].reshape(1, D_MODEL)
    l2b = ln2_b[layer].reshape(1, D_MODEL)
    t2, spread_mask = _spread_consts()

    n_groups = PEER_GROUPS if bsz % PEER_GROUPS == 0 else 1
    halves = 2 if (n_groups % 2 == 0 and bsz % 16 == 0) else 1
    per_half = n_groups // halves
    hb = bsz // halves
    gb = bsz // n_groups
    g_tok = gb * length
    mats = _s5_matrices(s5_a_re[layer], s5_a_im[layer], s5_log_dt[layer], s5_b_re[layer], s5_b_im[layer],
                        s5_c_re[layer], s5_c_im[layer], hb)
    def half_stages(h, ready):
        x_h, ctx_h, _ = lax.optimization_barrier((x, ctx, ready))
        k_ctx, v_ctx, su_ctx = _inproj(ctx_h, csh_a, csc_a, c_id, s_id, wts, is_ctx=True, tm=min(tm, l_ctx), b0=h * hb, bsz=hb)
        q, k_lat, v_lat, su_lat, gm, gs = _inproj(x_h, sh_a, sc_a, c_lat, s_lat, wts, is_ctx=False, tm=tm, b0=h * hb, bsz=hb)
        su_all = jnp.concatenate([su_ctx, su_lat], axis=0)
        y = _s5_scan(su_all.reshape(-1, S5_WIDTH), mats, bsz=hb, l_ctx=l_ctx, tt=s5_tt)
        return k_ctx, v_ctx, q, k_lat, v_lat, gm, gs, su_all, y.reshape(2, length, hb * S5_WIDTH)

    outs = []

    def finish_group(grp, after):
        vg, act, gate, f_sc, h1f, tok0 = grp
        gate, _ = lax.optimization_barrier((gate, (after, tuple(outs[-1:]))))
        return _peer_eval(vg, act, gate, t2, spread_mask, f_sc, h1f, g_f, l2g, l2b, tb=peer_tb, tok0=tok0, length=length)

    pending = None
    ready = (u_pack, v_pack)
    nxt_half = half_stages(0, ready)
    for g in range(n_groups):
        b0 = g * gb
        h0 = (g % per_half) * gb
        if g % per_half == 0:
            k_ctx, v_ctx, q, k_lat, v_lat, gm, gs, su_all, y2d = nxt_half
        q_g, _ = lax.optimization_barrier((q, ready))
        att = _attention(q_g, k_ctx, v_ctx, k_lat, v_lat, tq=min(ATT_Q_TILE, length), b0=h0, bsz=gb)
        h1, xm, st = _merge(x, att, gm, gs, su_all, y2d, s5_d[layer].reshape(1, S5_WIDTH), wglu, wout, g_a,
                            ln1_g[layer].reshape(1, D_MODEL), ln1_b[layer].reshape(1, D_MODEL), sh_f, sc_f, wq, keys,
                            tm=tm, ctx_tiles=l_ctx // tm, b0=b0, h0=h0)
        experts, gates = _peer_topk(st, lanes=TOPK_LANES)
        idx = experts.transpose(2, 0, 1).reshape(g_tok * PEER_PAIRS)
        gate = gates.transpose(2, 0, 1).reshape(g_tok, PEER_PAIRS)
        xmf = xm.reshape(g_tok, D_MODEL)
        acts, vg, f_sc = _sc_peer(u_pack, v_pack, idx, xmf, gate.reshape(g_tok * PEER_PAIRS))
        launched = (idx, gate, xmf)
        if g % per_half == max(per_half - 3, 0) and g // per_half + 1 < halves:
            nxt_half = half_stages(g // per_half + 1, launched)
            launched = launched + (nxt_half[-1],)
        if pending is not None:
            outs.append(finish_group(pending, launched))
        pending = (vg, acts.reshape(g_tok, PEER_PAIRS), gate, f_sc, h1.reshape(g_tok, D_MODEL), g * g_tok)
        ready = launched + ((outs[-1],) if outs else ())
    outs.append(finish_group(pending, ()))
    return jnp.concatenate(outs, axis=0).reshape(bsz, length, D_MODEL)
```

```python
import functools

import jax
import jax.numpy as jnp
from jax import lax
from jax.experimental import pallas as pl
from jax.experimental.pallas import tpu as pltpu
from jax.experimental.pallas import tpu_sc as plsc

F32 = jnp.float32
BF16 = jnp.bfloat16

D_MODEL = 1024
DEPTH = 1
GRID_W = 64
N_HEADS = 8
QK_NOPE_DIM = 128
QK_ROPE_DIM = 64
V_HEAD_DIM = 128
Q_RANK = 384
KV_RANK = 256
ROPE_AXIS_DIM = QK_ROPE_DIM // 2
ROPE_BASE = 10000.0
S5_WIDTH = D_MODEL // 2
S5_GROUP = 16
S5_GROUPS = S5_WIDTH // S5_GROUP
S5_STATE = 64
PEER_HEADS = 8
PEER_N_KEYS = 128
PEER_TOPK = 16
PEER_HALF = 128
PEER_PAIRS = PEER_HEADS * PEER_TOPK
DEEPNORM_ALPHA = (2.0 * DEPTH) ** 0.25
LN_EPS = 1e-6
N_MOD = 6
ATT_SCALE = (QK_NOPE_DIM + QK_ROPE_DIM) ** -0.5

HEAD_PAD = 256
S5_COLS = S5_GROUPS * S5_STATE
S5_CHUNKS = 2
SC_WORKERS = 32
SC_LANES = 16
U_RING = 4
DOT_ROWS = 8
SC_UNROLL = 1
STAGE_LAG = 2
SC_PAIRS = 64
STAGE_STEPS = (0, 2, 4, 6)
SUM_STEPS = (1, 3, 5, 7)
TC_PAIRS = PEER_PAIRS - SC_PAIRS
CODE_NONE = 1.0e9
PEER_GROUPS = 8
HALF_W = D_MODEL // 2
VMEM_LIMIT = 48 * 1024 * 1024
ROW_TILE = 256
ATT_Q_TILE = 512
S5_TIME_TILE = 64
TOPK_LANES = 128
EVAL_TOKENS = 32
PACK_ROWS = 1024


def _dot(a, b):
    return jnp.dot(a, b, preferred_element_type=F32)


def _dot_nt(a, b):
    return lax.dot_general(a, b, (((1,), (1,)), ((), ())), preferred_element_type=F32)


def _gelu(x):
    return 0.5 * x * (1.0 + jnp.tanh(0.7978845608028654 * (x + 0.044715 * (x * x * x))))


def _ln_plain(x):
    mu = jnp.mean(x, axis=-1, keepdims=True)
    xc = x - mu
    var = jnp.mean(xc * xc, axis=-1, keepdims=True)
    return xc * lax.rsqrt(var + LN_EPS)


def _rms(x, g):
    return x * lax.rsqrt(jnp.mean(x * x, axis=-1, keepdims=True) + LN_EPS) * g


def _params(*sem):
    return pltpu.CompilerParams(dimension_semantics=sem, vmem_limit_bytes=VMEM_LIMIT)


def _mod_kernel(cond_ref, w_ref, b_ref, o_ref):
    a = cond_ref[...]
    a = a * jax.nn.sigmoid(a)
    a_hi = a.astype(BF16)
    a_lo = (a - a_hi.astype(F32)).astype(BF16)
    w = w_ref[...]
    w_hi = w.astype(BF16)
    w_lo = (w - w_hi.astype(F32)).astype(BF16)
    o_ref[...] = _dot(a_hi, w_hi) + _dot(a_lo, w_hi) + _dot(a_hi, w_lo) + b_ref[...]


def _modulation(cond, w_mod, b_mod):
    rows = cond.shape[0]
    n = w_mod.shape[1]
    blk = D_MODEL
    return pl.pallas_call(
        _mod_kernel,
        grid=(n // blk,),
        in_specs=[
            pl.BlockSpec((rows, D_MODEL), lambda j: (0, 0)),
            pl.BlockSpec((D_MODEL, blk), lambda j: (0, j)),
            pl.BlockSpec((1, blk), lambda j: (0, j)),
        ],
        out_specs=pl.BlockSpec((rows, blk), lambda j: (0, j)),
        out_shape=jax.ShapeDtypeStruct((rows, n), F32),
        compiler_params=_params("arbitrary"),
        name="modulation",
    )(cond, w_mod, b_mod.reshape(1, n))


def _rope128(t, c_tab, s_tab):
    return t * c_tab + pltpu.roll(t, 64, axis=1) * s_tab


def _inproj_kernel(x_ref, sh_ref, sc_ref, c_ref, s_ref, wcq_ref, wckv_ref, wsu_ref, wgm_ref, wgs_ref,
                   qg_ref, kvg_ref, wuq_ref, wukv_ref, *out_refs, is_ctx):
    if is_ctx:
        k_ref, v_ref, su_ref = out_refs
    else:
        q_ref, k_ref, v_ref, su_ref, gm_ref, gs_ref = out_refs
    xm = _ln_plain(x_ref[0]) * (1.0 + sc_ref[0]) + sh_ref[0]
    xb = xm.astype(BF16)
    c_tab = c_ref[...]
    s_tab = s_ref[...]

    ckvpe = _dot(xb, wckv_ref[...])
    ckv_n = _rms(ckvpe[:, :KV_RANK], kvg_ref[...]).astype(BF16)
    kpe = _rope128(ckvpe[:, KV_RANK:], c_tab, s_tab).astype(BF16)
    kv = _dot(ckv_n, wukv_ref[...])
    for h in range(N_HEADS):
        k_ref[0, h, :, 0:128] = kv[:, h * 256:h * 256 + 128].astype(BF16)
        k_ref[0, h, :, 128:256] = kpe
        v_ref[0, h] = kv[:, h * 256 + 128:(h + 1) * 256].astype(BF16)

    su_ref[...] = _dot(xb, wsu_ref[...])

    if not is_ctx:
        cq_n = _rms(_dot(xb, wcq_ref[...]), qg_ref[...]).astype(BF16)
        q = _dot(cq_n, wuq_ref[...])
        for h in range(N_HEADS):
            q_ref[0, h, :, 0:128] = (q[:, h * 256:h * 256 + 128] * ATT_SCALE).astype(BF16)
            q_ref[0, h, :, 128:256] = (_rope128(q[:, h * 256 + 128:(h + 1) * 256], c_tab, s_tab) * ATT_SCALE).astype(BF16)
        gm_ref[0] = jax.nn.sigmoid(_dot(xb, wgm_ref[...])).astype(BF16)
        gs_ref[0] = jax.nn.sigmoid(_dot(xb, wgs_ref[...])).astype(BF16)


def _inproj(x, shift, scale, c_tab, s_tab, wts, *, is_ctx, tm, b0, bsz):
    length = x.shape[1]
    mod_map = (lambda b, t: (0, 0, 0)) if shift.shape[0] == 1 else (lambda b, t: (b0 + b, 0, 0))
    const2 = lambda b, t: (0, 0)
    wcq, wckv, wsu, wgm, wgs, qg, kvg, wuq, wukv = wts
    in_specs = [
        pl.BlockSpec((1, tm, D_MODEL), lambda b, t: (b0 + b, t, 0)),
        pl.BlockSpec((1, 1, D_MODEL), mod_map),
        pl.BlockSpec((1, 1, D_MODEL), mod_map),
        pl.BlockSpec((tm, 128), lambda b, t: (t, 0)),
        pl.BlockSpec((tm, 128), lambda b, t: (t, 0)),
    ] + [pl.BlockSpec(w.shape, const2) for w in (wcq, wckv, wsu, wgm, wgs, qg, kvg, wuq, wukv)]
    head_spec = lambda width: pl.BlockSpec((1, N_HEADS, tm, width), lambda b, t: (b, 0, t, 0))
    row_spec = pl.BlockSpec((1, tm, D_MODEL), lambda b, t: (b, t, 0))
    su_spec = pl.BlockSpec((tm, S5_WIDTH), lambda b, t: (t, b))
    k_shape = jax.ShapeDtypeStruct((bsz, N_HEADS, length, HEAD_PAD), BF16)
    v_shape = jax.ShapeDtypeStruct((bsz, N_HEADS, length, V_HEAD_DIM), BF16)
    su_shape = jax.ShapeDtypeStruct((length, bsz * S5_WIDTH), F32)
    g_shape = jax.ShapeDtypeStruct((bsz, length, D_MODEL), BF16)
    if is_ctx:
        out_specs = [head_spec(HEAD_PAD), head_spec(V_HEAD_DIM), su_spec]
        out_shape = [k_shape, v_shape, su_shape]
    else:
        out_specs = [head_spec(HEAD_PAD), head_spec(HEAD_PAD), head_spec(V_HEAD_DIM), su_spec, row_spec, row_spec]
        out_shape = [k_shape, k_shape, v_shape, su_shape, g_shape, g_shape]
    return pl.pallas_call(
        functools.partial(_inproj_kernel, is_ctx=is_ctx),
        grid=(bsz, length // tm),
        in_specs=in_specs,
        out_specs=out_specs,
        out_shape=out_shape,
        compiler_params=_params("parallel", "parallel"),
        name="inproj_ctx" if is_ctx else "inproj_lat",
    )(x, shift, scale, c_tab, s_tab, wcq, wckv, wsu, wgm, wgs, qg, kvg, wuq, wukv)


def _attn_kernel(q_ref, kc_ref, vc_ref, kl_ref, vl_ref, o_ref):
    q = q_ref[0, 0]
    s_c = _dot_nt(q, kc_ref[0, 0])
    s_l = _dot_nt(q, kl_ref[0, 0])
    m = jnp.maximum(jnp.max(s_c, axis=-1, keepdims=True), jnp.max(s_l, axis=-1, keepdims=True))
    p_c = jnp.exp(s_c - m)
    p_l = jnp.exp(s_l - m)
    denom = jnp.sum(p_c, axis=-1, keepdims=True) + jnp.sum(p_l, axis=-1, keepdims=True)
    o = _dot(p_c.astype(BF16), vc_ref[0, 0]) + _dot(p_l.astype(BF16), vl_ref[0, 0])
    o_ref[0] = (o / denom).astype(BF16)


def _attention(q, k_ctx, v_ctx, k_lat, v_lat, *, tq, b0, bsz):
    _, heads, length, _ = q.shape
    l_ctx = k_ctx.shape[2]
    full = lambda b, h, i: (b0 + b, h, 0, 0)
    return pl.pallas_call(
        _attn_kernel,
        grid=(bsz, heads, length // tq),
        in_specs=[
            pl.BlockSpec((1, 1, tq, HEAD_PAD), lambda b, h, i: (b0 + b, h, i, 0)),
            pl.BlockSpec((1, 1, l_ctx, HEAD_PAD), full),
            pl.BlockSpec((1, 1, l_ctx, V_HEAD_DIM), full),
            pl.BlockSpec((1, 1, length, HEAD_PAD), full),
            pl.BlockSpec((1, 1, length, V_HEAD_DIM), full),
        ],
        out_specs=pl.BlockSpec((1, tq, V_HEAD_DIM), lambda b, h, i: (b, i, h)),
        out_shape=jax.ShapeDtypeStruct((bsz, length, heads * V_HEAD_DIM), BF16),
        compiler_params=_params("parallel", "parallel", "arbitrary"),
        name="attention",
    )(q, k_ctx, v_ctx, k_lat, v_lat)


def _s5_kernel(u_ref, are_ref, aim_ref, bre_ref, bim_ref, cre_ref, cim_ref, y_ref,
               bu_re, bu_im, h_re, h_im, *, tt, bsz, n_ctx_tiles):
    d = pl.program_id(0)
    i = pl.program_id(1)

    @pl.when(i == 0)
    def _():
        h_re[...] = jnp.zeros_like(h_re)
        h_im[...] = jnp.zeros_like(h_im)

    u = u_ref[...].astype(BF16)
    half = S5_COLS // S5_CHUNKS
    cw = S5_WIDTH // S5_CHUNKS
    for c in range(S5_CHUNKS):
        uc = u[:, c * cw:(c + 1) * cw]
        bu_re[:, c * half:(c + 1) * half] = _dot(uc, bre_ref[0, c])
        bu_im[:, c * half:(c + 1) * half] = _dot(uc, bim_ref[0, c])

    col_w = 512
    for cc in range(S5_COLS // col_w):
        cols = slice(cc * col_w, (cc + 1) * col_w)
        a_r = are_ref[0, :, cols]
        a_i = aim_ref[0, :, cols]

        def step(t, carry, cols=cols, a_r=a_r, a_i=a_i):
            hr, hi = carry
            pos = jnp.where(d == 0, t, tt - 1 - t)
            r = pl.multiple_of(pos * bsz, bsz)
            nr = a_r * hr - a_i * hi + bu_re[pl.ds(r, bsz), cols]
            ni = a_r * hi + a_i * hr + bu_im[pl.ds(r, bsz), cols]
            bu_re[pl.ds(r, bsz), cols] = nr
            bu_im[pl.ds(r, bsz), cols] = ni
            return nr, ni

        hr, hi = lax.fori_loop(0, tt, step, (h_re[:, cols], h_im[:, cols]))
        h_re[:, cols] = hr
        h_im[:, cols] = hi

    @pl.when(i >= n_ctx_tiles)
    def _():
        for c in range(S5_CHUNKS):
            sl = slice(c * half, (c + 1) * half)
            y = _dot(bu_re[:, sl].astype(BF16), cre_ref[0, c]) + _dot(bu_im[:, sl].astype(BF16), cim_ref[0, c])
            y_ref[0, :, c * cw:(c + 1) * cw] = y


def _s5_scan(u_all, mats, *, bsz, l_ctx, tt):
    a_re_b, a_im_b, b_re_bd, b_im_bd, c_re_bd, c_im_bd = mats
    rows = tt * bsz
    n_tiles = u_all.shape[0] // rows
    n_ctx_tiles = l_ctx // tt
    n_lat_tiles = n_tiles - n_ctx_tiles

    def u_map(d, i):
        bwd = jnp.where(i < n_ctx_tiles, n_ctx_tiles - 1 - i, n_tiles - 1 - (i - n_ctx_tiles))
        return (jnp.where(d == 0, i, bwd), 0)

    def y_map(d, i):
        j = jnp.maximum(i - n_ctx_tiles, 0)
        return (d, jnp.where(d == 0, j, n_lat_tiles - 1 - j), 0)

    dmap3 = lambda d, i: (d, 0, 0)
    dmap4 = lambda d, i: (d, 0, 0, 0)
    return pl.pallas_call(
        functools.partial(_s5_kernel, tt=tt, bsz=bsz, n_ctx_tiles=n_ctx_tiles),
        grid=(2, n_tiles),
        in_specs=[
            pl.BlockSpec((rows, S5_WIDTH), u_map),
            pl.BlockSpec((1,) + a_re_b.shape[1:], dmap3),
            pl.BlockSpec((1,) + a_im_b.shape[1:], dmap3),
            pl.BlockSpec((1,) + b_re_bd.shape[1:], dmap4),
            pl.BlockSpec((1,) + b_im_bd.shape[1:], dmap4),
            pl.BlockSpec((1,) + c_re_bd.shape[1:], dmap4),
            pl.BlockSpec((1,) + c_im_bd.shape[1:], dmap4),
        ],
        out_specs=pl.BlockSpec((1, rows, S5_WIDTH), y_map),
        out_shape=jax.ShapeDtypeStruct((2, n_lat_tiles * rows, S5_WIDTH), F32),
        scratch_shapes=[
            pltpu.VMEM((rows, S5_COLS), F32),
            pltpu.VMEM((rows, S5_COLS), F32),
            pltpu.VMEM((bsz, S5_COLS), F32),
            pltpu.VMEM((bsz, S5_COLS), F32),
        ],
        compiler_params=_params("arbitrary", "arbitrary"),
        name="s5_scan",
    )(u_all, a_re_b, a_im_b, b_re_bd, b_im_bd, c_re_bd, c_im_bd)


def _s5_matrices(a_re, a_im, log_dt, b_re, b_im, c_re, c_im, bsz):
    dt = jnp.exp(log_dt)[..., None]
    mag = jnp.exp(a_re * dt)
    ab_re = mag * jnp.cos(a_im * dt)
    ab_im = mag * jnp.sin(a_im * dt)
    den = a_re * a_re + a_im * a_im
    f_re = ((ab_re - 1.0) * a_re + ab_im * a_im) / den
    f_im = (ab_im * a_re - (ab_re - 1.0) * a_im) / den
    bb_re = f_re[..., None] * b_re - f_im[..., None] * b_im
    bb_im = f_re[..., None] * b_im + f_im[..., None] * b_re
    gl = S5_GROUPS // S5_CHUNKS
    eye = jnp.eye(gl, dtype=F32)

    def in_bd(bb):
        t = bb.reshape(2, S5_CHUNKS, gl, S5_STATE, S5_GROUP)
        t = jnp.einsum('dkgpc,gh->dkgchp', t, eye)
        return t.reshape(2, S5_CHUNKS, gl * S5_GROUP, gl * S5_STATE).astype(BF16)

    def out_bd(cc):
        t = cc.reshape(2, S5_CHUNKS, gl, S5_GROUP, S5_STATE)
        t = jnp.einsum('dkgcp,gh->dkgphc', t, eye)
        return t.reshape(2, S5_CHUNKS, gl * S5_STATE, gl * S5_GROUP).astype(BF16)

    bcast = lambda a: jnp.broadcast_to(a.reshape(2, 1, S5_COLS), (2, bsz, S5_COLS))
    return bcast(ab_re), bcast(ab_im), in_bd(bb_re), in_bd(bb_im), out_bd(c_re), out_bd(-c_im)


def _merge_kernel(x_ref, att_ref, gm_ref, gs_ref, su_ref, y_ref, dsk_ref, wglu_ref, wout_ref, ga_ref,
                  l1g_ref, l1b_ref, shf_ref, scf_ref, wq_ref, keys_ref, h1_ref, xm_ref, st_ref):
    y = su_ref[...] * dsk_ref[...] + y_ref[0] + y_ref[1]
    gl = _dot(_gelu(y).astype(BF16), wglu_ref[...])
    s5_out = gl[:, :D_MODEL] * jax.nn.sigmoid(gl[:, D_MODEL:])
    merged = gm_ref[0].astype(F32) * att_ref[0].astype(F32) + gs_ref[0].astype(F32) * s5_out
    out = _dot(merged.astype(BF16), wout_ref[...])
    h1 = _ln_plain(DEEPNORM_ALPHA * x_ref[0] + ga_ref[0] * out) * l1g_ref[...] + l1b_ref[...]
    h1_ref[0] = h1
    xm = _ln_plain(h1) * (1.0 + scf_ref[0]) + shf_ref[0]
    xm_ref[0] = xm
    qp = _dot(xm.astype(BF16), wq_ref[...]).astype(BF16)
    for j in range(2 * PEER_HEADS):
        st_ref[j] = _dot_nt(keys_ref[j], qp[:, j * PEER_HALF:(j + 1) * PEER_HALF])


def _merge(x, att, gm, gs, su_all2d, y2d, dsk, wglu, wout, g_a, l1g, l1b, sh_f, sc_f, wq, keys, *, tm, ctx_tiles, b0, h0):
    bsz, length, _ = att.shape
    nt = length // tm
    row = pl.BlockSpec((1, tm, D_MODEL), lambda b, t: (b, t, 0))
    xrow = pl.BlockSpec((1, tm, D_MODEL), lambda b, t: (b0 + b, t, 0))
    hrow = pl.BlockSpec((1, tm, D_MODEL), lambda b, t: (h0 + b, t, 0))
    modb = pl.BlockSpec((1, 1, D_MODEL), lambda b, t: (b0 + b, 0, 0))
    const = lambda a: pl.BlockSpec(a.shape, lambda b, t: (0,) * a.ndim)
    return pl.pallas_call(
        _merge_kernel,
        grid=(bsz, nt),
        in_specs=[
            xrow, row, hrow, hrow,
            pl.BlockSpec((tm, S5_WIDTH), lambda b, t: (ctx_tiles + t, h0 + b)),
            pl.BlockSpec((2, tm, S5_WIDTH), lambda b, t: (0, t, h0 + b)),
            const(dsk), const(wglu), const(wout), modb, const(l1g), const(l1b), modb, modb, const(wq), const(keys),
        ],
        out_specs=[row, row, pl.BlockSpec((2 * PEER_HEADS, PEER_N_KEYS, tm), lambda b, t: (0, 0, b * nt + t))],
        out_shape=[
            jax.ShapeDtypeStruct((bsz, length, D_MODEL), F32),
            jax.ShapeDtypeStruct((bsz, length, D_MODEL), F32),
            jax.ShapeDtypeStruct((2 * PEER_HEADS, PEER_N_KEYS, bsz * length), F32),
        ],
        compiler_params=_params("parallel", "parallel"),
        name="merge_peer_query",
    )(x, att, gm, gs, su_all2d, y2d, dsk, wglu, wout, g_a, l1g, l1b, sh_f, sc_f, wq, keys)


def _take_top(vals, codes, payload, k):
    rows = lax.broadcasted_iota(jnp.int32, (k, vals.shape[1]), 0)
    top_v = jnp.zeros((k, vals.shape[1]), F32)
    top_p = jnp.zeros((k, vals.shape[1]), F32)
    for r in range(k):
        m = jnp.max(vals, axis=0, keepdims=True)
        cm = jnp.min(jnp.where(vals == m, codes, CODE_NONE), axis=0, keepdims=True)
        sel = codes == cm
        if payload is None:
            p = cm
        else:
            p = jnp.max(jnp.where(sel, payload, -1.0), axis=0, keepdims=True)
        vals = jnp.where(sel, -jnp.inf, vals)
        top_v = jnp.where(rows == r, m, top_v)
        top_p = jnp.where(rows == r, p, top_p)
    return top_v, top_p


def _topk_kernel(st_ref, e_ref, g_ref):
    lanes = st_ref.shape[2]
    key_idx = lax.broadcasted_iota(jnp.int32, (PEER_N_KEYS, lanes), 0).astype(F32)
    sub_idx = lax.broadcasted_iota(jnp.int32, (PEER_TOPK, lanes), 0).astype(F32)
    half_k = PEER_TOPK // 2
    for h in range(PEER_HEADS):
        sv0, si0 = _take_top(st_ref[2 * h], key_idx, None, PEER_TOPK)
        sv1, si1 = _take_top(st_ref[2 * h + 1], key_idx, None, PEER_TOPK)
        cs, ce, cc = [], [], []
        for i in range(half_k):
            nj = PEER_TOPK if i == 0 else half_k
            cs.append(sv0[i:i + 1, :] + sv1[0:nj, :])
            ce.append(si0[i:i + 1, :] * PEER_N_KEYS + si1[0:nj, :])
            cc.append(sub_idx[0:nj, :] + i * PEER_TOPK)
        cs.append(sv0[half_k:, :] + sv1[0:1, :])
        ce.append(si0[half_k:, :] * PEER_N_KEYS + si1[0:1, :])
        cc.append((sub_idx[0:half_k, :] + half_k) * PEER_TOPK)
        top_s, top_e = _take_top(jnp.concatenate(cs, axis=0), jnp.concatenate(cc, axis=0),
                                 jnp.concatenate(ce, axis=0), PEER_TOPK)
        p = jnp.exp(top_s - jnp.max(top_s, axis=0, keepdims=True))
        g_ref[h] = p / jnp.sum(p, axis=0, keepdims=True)
        e_ref[h] = top_e.astype(jnp.int32)


def _peer_topk(st, *, lanes):
    n = st.shape[2]
    out_spec = pl.BlockSpec((PEER_HEADS, PEER_TOPK, lanes), lambda i: (0, 0, i))
    return pl.pallas_call(
        _topk_kernel,
        grid=(n // lanes,),
        in_specs=[pl.BlockSpec((2 * PEER_HEADS, PEER_N_KEYS, lanes), lambda i: (0, 0, i))],
        out_specs=[out_spec, out_spec],
        out_shape=[jax.ShapeDtypeStruct((PEER_HEADS, PEER_TOPK, n), jnp.int32),
                   jax.ShapeDtypeStruct((PEER_HEADS, PEER_TOPK, n), F32)],
        compiler_params=_params("parallel"),
        name="peer_topk",
    )(st)


def _sc_peer(u_tab, v_tab, idx, x, gates):
    n_tok = x.shape[0]
    per_worker = n_tok // SC_WORKERS
    assert per_worker % 2 == 0 and per_worker >= 4
    assert len(STAGE_STEPS) == (PEER_PAIRS - SC_PAIRS) // SC_LANES and len(SUM_STEPS) == SC_PAIRS // SC_LANES
    steps = PEER_PAIRS // SC_LANES
    assert steps % U_RING == 0
    n_vc = SC_PAIRS // SC_LANES
    n_vs = (PEER_PAIRS - SC_PAIRS) // SC_LANES
    tc_pairs = PEER_PAIRS - SC_PAIRS
    chunks = HALF_W // SC_LANES
    mesh = plsc.VectorSubcoreMesh(core_axis_name="c", subcore_axis_name="s")

    @functools.partial(
        pl.kernel, mesh=mesh,
        out_type=(jax.ShapeDtypeStruct((n_tok * PEER_PAIRS,), F32),
                  jax.ShapeDtypeStruct((n_tok * tc_pairs, HALF_W), jnp.int32),
                  jax.ShapeDtypeStruct((n_tok, 2 * HALF_W), F32)),
        scratch_types=[
            pltpu.VMEM((2, PEER_PAIRS), jnp.int32),
            pltpu.VMEM((2, PEER_PAIRS), jnp.int32),
            pltpu.VMEM((2, 2 * HALF_W), F32),
            pltpu.VMEM((2, SC_PAIRS), F32),
            pltpu.VMEM((2, SC_PAIRS), F32),
            pltpu.VMEM((2, PEER_PAIRS), F32),
            pltpu.VMEM((2, 2 * HALF_W), F32),
            pltpu.VMEM((U_RING, SC_LANES, HALF_W), jnp.int32),
            pltpu.VMEM((n_vc, SC_LANES, HALF_W), jnp.int32),
            pltpu.VMEM((n_vs, SC_LANES, HALF_W), jnp.int32),
            pltpu.SemaphoreType.DMA((U_RING,)),
            pltpu.SemaphoreType.DMA((n_vc,)),
            pltpu.SemaphoreType.DMA((n_vs,)),
            pltpu.SemaphoreType.DMA((n_vs,)),
            pltpu.SemaphoreType.DMA((2,)),
            pltpu.SemaphoreType.DMA((2,)),
            pltpu.SemaphoreType.DMA((2,)),
            pltpu.SemaphoreType.DMA((2,)),
            pltpu.SemaphoreType.DMA((2,)),
        ],
        compiler_params=pltpu.CompilerParams(needs_layout_passes=False),
    )
    def peer(u_hbm, v_hbm, idx_hbm, x_hbm, gate_hbm, act_hbm, vrows_hbm, f_hbm,
             idx_v, vidx_v, x_v, gate_v, coef_v, act_v, f_v, urows_v, vsum_v, vstage_v,
             u_sems, vc_sems, vg_sems, vw_sems, idx_sems, x_sems, gate_sems, act_sems, f_sems):
        wid = lax.axis_index("s") * 2 + lax.axis_index("c")
        base = wid * per_worker
        last = base + per_worker - 1
        lane = lax.iota(jnp.int32, SC_LANES)

        def pair_off(tok):
            return pl.multiple_of(tok * PEER_PAIRS, PEER_PAIRS)

        def idx_copy(tok, slot):
            return pltpu.make_async_copy(idx_hbm.at[pl.ds(pair_off(tok), PEER_PAIRS)], idx_v.at[slot], idx_sems.at[slot])

        def x_copy(tok, slot):
            return pltpu.make_async_copy(x_hbm.at[tok], x_v.at[slot], x_sems.at[slot])

        def gate_copy(tok, slot):
            return pltpu.make_async_copy(gate_hbm.at[pl.ds(pair_off(tok), SC_PAIRS)], gate_v.at[slot], gate_sems.at[slot])

        def u_gather(slot, g):
            return pltpu.make_async_copy(u_hbm.at[idx_v.at[slot, pl.ds(g * SC_LANES, SC_LANES)]], urows_v.at[g % U_RING],
                                         u_sems.at[g % U_RING])

        def vsum_gather(slot, j):
            return pltpu.make_async_copy(v_hbm.at[vidx_v.at[slot, pl.ds(j * SC_LANES, SC_LANES)]], vsum_v.at[j],
                                         vc_sems.at[j])

        def vstage_gather(slot, s):
            return pltpu.make_async_copy(v_hbm.at[vidx_v.at[slot, pl.ds(SC_PAIRS + s * SC_LANES, SC_LANES)]],
                                         vstage_v.at[s], vg_sems.at[s])

        def vstage_write(tok, s):
            row0 = pl.multiple_of(tok * tc_pairs + s * SC_LANES, SC_LANES)
            return pltpu.make_async_copy(vstage_v.at[s], vrows_hbm.at[pl.ds(row0, SC_LANES)], vw_sems.at[s])

        def act_copy(tok, slot):
            return pltpu.make_async_copy(act_v.at[slot], act_hbm.at[pl.ds(pair_off(tok), PEER_PAIRS)], act_sems.at[slot])

        def f_copy(tok, slot):
            return pltpu.make_async_copy(f_v.at[slot], f_hbm.at[tok], f_sems.at[slot])

        def unpack(wv):
            return (lax.bitcast_convert_type(wv << 16, F32), lax.bitcast_convert_type(wv & jnp.int32(-65536), F32))

        def step_dots(slot, g):
            v = jnp.zeros((SC_LANES,), F32)
            for r0 in range(0, SC_LANES, DOT_ROWS):
                def body(c, accs, r0=r0):
                    xlo = x_v[slot, pl.ds(c * SC_LANES, SC_LANES)]
                    xhi = x_v[slot, pl.ds(HALF_W + c * SC_LANES, SC_LANES)]
                    new = []
                    for r in range(DOT_ROWS):
                        lo, hi = unpack(urows_v[g % U_RING, r0 + r, pl.ds(c * SC_LANES, SC_LANES)])
                        new.append(accs[r] + lo * xlo + hi * xhi)
                    return tuple(new)

                accs = plsc.parallel_loop(0, chunks, unroll=SC_UNROLL,
                                          carry=tuple(jnp.zeros((SC_LANES,), F32) for _ in range(DOT_ROWS)))(body)
                for r in range(DOT_ROWS):
                    v = jnp.where(lane == r0 + r, jnp.sum(accs[r]), v)
            act_v[slot, pl.ds(g * SC_LANES, SC_LANES)] = v

        def window_sum(fslot, j):
            cvec = coef_v[fslot, pl.ds(j * SC_LANES, SC_LANES)]
            cb = [jnp.zeros((SC_LANES,), F32) + jnp.sum(jnp.where(lane == r, cvec, 0.0)) for r in range(SC_LANES)]

            def body(c):
                alo = f_v[fslot, pl.ds(c * SC_LANES, SC_LANES)]
                ahi = f_v[fslot, pl.ds(HALF_W + c * SC_LANES, SC_LANES)]
                for r in range(SC_LANES):
                    lo, hi = unpack(vsum_v[j, r, pl.ds(c * SC_LANES, SC_LANES)])
                    alo = alo + cb[r] * lo
                    ahi = ahi + cb[r] * hi
                f_v[fslot, pl.ds(c * SC_LANES, SC_LANES)] = alo
                f_v[fslot, pl.ds(HALF_W + c * SC_LANES, SC_LANES)] = ahi

            plsc.parallel_loop(0, chunks, unroll=SC_UNROLL)(body)

        def token_coefs(slot):
            for q in range(n_vc):
                a = act_v[slot, pl.ds(q * SC_LANES, SC_LANES)]
                z2 = 1.5957691216057308 * (a + 0.044715 * (a * a * a))
                coef_v[slot, pl.ds(q * SC_LANES, SC_LANES)] = gate_v[slot, pl.ds(q * SC_LANES, SC_LANES)] * a / (1.0 + jnp.exp(-z2))

        def zero_f(fslot):
            for q in range(2 * HALF_W // SC_LANES):
                f_v[fslot, pl.ds(q * SC_LANES, SC_LANES)] = jnp.zeros((SC_LANES,), F32)

        idx_copy(base, 0).start()
        x_copy(base, 0).start()
        gate_copy(base, 0).start()
        idx_copy(base, 0).wait()
        for g in range(U_RING):
            u_gather(0, g).start()

        @pl.loop(0, per_worker, step=2)
        def _(i):
            for slot in range(2):
                tok = base + i + slot
                nxt = jnp.minimum(tok + 1, last)
                other = 1 - slot
                first = slot == 0
                idx_copy(nxt, other).start()
                x_copy(nxt, other).start()
                gate_copy(nxt, other).start()
                for q in range(PEER_PAIRS // SC_LANES):
                    vidx_v[slot, pl.ds(q * SC_LANES, SC_LANES)] = idx_v[slot, pl.ds(q * SC_LANES, SC_LANES)]
                x_copy(tok, slot).wait()
                gate_copy(tok, slot).wait()

                @pl.when(i > 0)
                def _():
                    act_copy(tok, slot).wait()

                if first:
                    @pl.when(i > 2)
                    def _():
                        f_copy(tok, other).wait()
                else:
                    @pl.when(i > 0)
                    def _():
                        f_copy(tok, other).wait()
                zero_f(other)

                idx_copy(nxt, other).wait()
                for g in range(steps):
                    s = STAGE_STEPS.index(g) if g in STAGE_STEPS else None
                    if s is not None:
                        if first:
                            @pl.when(i > 0)
                            def _(s=s, tok=tok):
                                vstage_write(tok, s).wait()
                        else:
                            vstage_write(tok, s).wait()
                        vstage_gather(slot, s).start()
                    u_gather(slot, g).wait()
                    step_dots(slot, g)
                    if g + U_RING < steps:
                        u_gather(slot, g + U_RING).start()
                    else:
                        u_gather(other, g + U_RING - steps).start()
                    if s is not None:
                        sp = (s - STAGE_LAG) % n_vs
                        tokp, slotp = (tok, slot) if s >= STAGE_LAG else (tok - 1, other)

                        def finish(sp=sp, tokp=tokp, slotp=slotp):
                            vstage_gather(slotp, sp).wait()
                            vstage_write(tokp, sp).start()

                        if first and s < STAGE_LAG:
                            pl.when(i > 0)(finish)
                        else:
                            finish()
                    if g in SUM_STEPS:
                        j = SUM_STEPS.index(g)

                        def consume(j=j, other=other):
                            vsum_gather(other, j).wait()
                            window_sum(other, j)

                        if first:
                            pl.when(i > 0)(consume)
                        else:
                            consume()
                        vsum_gather(slot, j).start()
                token_coefs(slot)
                act_copy(tok, slot).start()

                def send_prev(tok=tok, other=other):
                    f_copy(tok - 1, other).start()

                if first:
                    pl.when(i > 0)(send_prev)
                else:
                    send_prev()

        for s in range(n_vs - STAGE_LAG, n_vs):
            vstage_gather(1, s).wait()
            vstage_write(last, s).start()
        f_copy(last, 1).wait()
        zero_f(1)
        for j in range(n_vc):
            vsum_gather(1, j).wait()
            window_sum(1, j)
        f_copy(last, 1).start()
        for s in range(n_vs):
            vstage_write(last, s).wait()
        for g in range(U_RING):
            u_gather(0, g).wait()
        x_copy(last, 0).wait()
        gate_copy(last, 0).wait()
        for slot in range(2):
            f_copy(last, slot).wait()
            act_copy(last, slot).wait()

    return peer(u_tab, v_tab, idx, x, gates)


def _peer_kernel(vg_ref, act_ref, gate_ref, t2_ref, mask_ref, fsc_ref, h1_ref, gf_ref, l2g_ref, l2b_ref, o_ref, f_ref, *, tb):
    sub = 8
    span = sub * TC_PAIRS
    half = D_MODEL // 2
    for s in range(tb // sub):
        rows = slice(s * span, (s + 1) * span)
        toks = slice(s * sub, (s + 1) * sub)
        coef = (gate_ref[toks, :] * _gelu(act_ref[toks, :])).astype(BF16)
        spread = _dot(coef, t2_ref[...])
        c2 = (jnp.concatenate([spread, spread], axis=0) * mask_ref[...]).astype(BF16)
        f2 = _dot(c2, pltpu.bitcast(vg_ref[rows, :], BF16))
        f_ref[toks, :half] = f2[:sub]
        f_ref[toks, half:] = f2[sub:]
    r = DEEPNORM_ALPHA * h1_ref[...] + gf_ref[0] * (f_ref[...] + fsc_ref[...])
    o_ref[...] = _ln_plain(r) * l2g_ref[...] + l2b_ref[...]


def _peer_eval(vg, act, gate, t2, mask, f_sc, h1, g_f, l2g, l2b, *, tb, tok0, length):
    n = h1.shape[0]
    tok_blk0 = tok0 // tb
    per_b = length // tb
    tok_spec = lambda width: pl.BlockSpec((tb, width), lambda i: (i, 0))
    return pl.pallas_call(
        functools.partial(_peer_kernel, tb=tb),
        grid=(n // tb,),
        in_specs=[
            pl.BlockSpec((tb * TC_PAIRS, D_MODEL // 2), lambda i: (i, 0)),
            tok_spec(PEER_PAIRS), tok_spec(PEER_PAIRS),
            pl.BlockSpec(t2.shape, lambda i: (0, 0)), pl.BlockSpec(mask.shape, lambda i: (0, 0)),
            tok_spec(D_MODEL), tok_spec(D_MODEL),
            pl.BlockSpec((1, 1, D_MODEL), lambda i: ((tok_blk0 + i) // per_b, 0, 0)),
            pl.BlockSpec((1, D_MODEL), lambda i: (0, 0)),
            pl.BlockSpec((1, D_MODEL), lambda i: (0, 0)),
        ],
        out_specs=tok_spec(D_MODEL),
        out_shape=jax.ShapeDtypeStruct((n, D_MODEL), F32),
        scratch_shapes=[pltpu.VMEM((tb, D_MODEL), F32)],
        compiler_params=_params("parallel"),
        name="peer_eval",
    )(vg, act, gate, t2, mask, f_sc, h1, g_f, l2g, l2b)


def _rope_perm():
    half = ROPE_AXIS_DIM // 2
    base = jnp.arange(QK_ROPE_DIM)
    return jnp.where((base % ROPE_AXIS_DIM) < half, base + half, base - half)


def _rope_tables(length):
    pos = jnp.arange(length)
    row = (pos // GRID_W).astype(F32)
    col = (pos % GRID_W).astype(F32)
    inv_freq = jnp.power(ROPE_BASE, -jnp.arange(0, ROPE_AXIS_DIM, 2, dtype=F32) / ROPE_AXIS_DIM)
    ang_r = row[:, None] * inv_freq
    ang_c = col[:, None] * inv_freq
    zeros = jnp.zeros((length, QK_ROPE_DIM), F32)
    c_tab = jnp.concatenate([jnp.cos(ang_r), jnp.cos(ang_r), jnp.cos(ang_c), jnp.cos(ang_c), zeros], axis=1)
    s_tab = jnp.concatenate([-jnp.sin(ang_r), jnp.sin(ang_r), -jnp.sin(ang_c), jnp.sin(ang_c), zeros], axis=1)
    return c_tab, s_tab


def _identity_tables(length):
    ones = jnp.ones((length, QK_ROPE_DIM), F32)
    zeros = jnp.zeros((length, QK_ROPE_DIM), F32)
    return jnp.concatenate([ones, zeros], axis=1), jnp.zeros((length, 2 * QK_ROPE_DIM), F32)


def _spread_consts():
    lanes = jnp.arange(16 * TC_PAIRS)
    pair = (lanes % (2 * TC_PAIRS)) // 2 + SC_PAIRS
    t2 = (jnp.arange(PEER_PAIRS)[:, None] == pair[None, :]).astype(BF16)
    r = jnp.arange(16)
    own = (lanes // (2 * TC_PAIRS))[None, :] == (r % 8)[:, None]
    parity = (lanes % 2 == 0)[None, :] == (r < 8)[:, None]
    return t2, (own & parity).astype(F32)


def _pack_kernel(t_ref, o_ref):
    lo = pltpu.bitcast(t_ref[:, :HALF_W].astype(BF16).astype(F32), jnp.uint32)
    hi = pltpu.bitcast(t_ref[:, HALF_W:].astype(BF16).astype(F32), jnp.uint32)
    o_ref[...] = pltpu.bitcast((hi & jnp.uint32(0xFFFF0000)) | (lo >> 16), jnp.int32)


def _pack_rows(table):
    rows = table.shape[0]
    return pl.pallas_call(
        _pack_kernel,
        grid=(rows // PACK_ROWS,),
        in_specs=[pl.BlockSpec((PACK_ROWS, 2 * HALF_W), lambda i: (i, 0))],
        out_specs=pl.BlockSpec((PACK_ROWS, HALF_W), lambda i: (i, 0)),
        out_shape=jax.ShapeDtypeStruct((rows, HALF_W), jnp.int32),
        compiler_params=_params("parallel"),
        name="pack_table",
    )(table)


def _layer_weights(w_in, q_norm_g, kv_norm_g, w_uq, w_ukv):
    perm = _rope_perm()
    p0 = Q_RANK
    p1 = p0 + KV_RANK
    p2 = p1 + QK_ROPE_DIM
    p3 = p2 + S5_WIDTH
    p4 = p3 + D_MODEL
    w_kpe = w_in[:, p1:p2]
    wckv = jnp.concatenate([w_in[:, p0:p1], w_kpe, w_kpe[:, perm]], axis=1)
    uq = w_uq.reshape(Q_RANK, N_HEADS, QK_NOPE_DIM + QK_ROPE_DIM)
    pe = uq[:, :, QK_NOPE_DIM:]
    wuq = jnp.concatenate([uq[:, :, :QK_NOPE_DIM], pe, pe[:, :, perm]], axis=2).reshape(Q_RANK, N_HEADS * HEAD_PAD)
    cast = lambda a: a.astype(BF16)
    return (cast(w_in[:, :p0]), cast(wckv), cast(w_in[:, p2:p3]), cast(w_in[:, p3:p4]), cast(w_in[:, p4:]),
            q_norm_g.reshape(1, Q_RANK), kv_norm_g.reshape(1, KV_RANK), cast(wuq), cast(w_ukv))


def kernel(x, c, ctx, c_ctx, w_mod, b_mod, w_in, q_norm_g, kv_norm_g, w_uq, w_ukv, s5_a_re, s5_a_im, s5_log_dt, s5_b_re, s5_b_im, s5_c_re, s5_c_im, s5_d, w_glu, w_out, ln1_g, ln1_b, peer_wq, peer_keys, peer_u, peer_v, ln2_g, ln2_b):
    bsz, length, _ = x.shape
    l_ctx = ctx.shape[1]
    layer = 0
    tm = ROW_TILE
    s5_tt = S5_TIME_TILE
    peer_tb = EVAL_TOKENS

    cond = jnp.concatenate([c, c_ctx[None, :], jnp.zeros((7, D_MODEL), F32)], axis=0)
    mod = _modulation(cond, w_mod[layer], b_mod[layer]).reshape(cond.shape[0], N_MOD, 1, D_MODEL)
    sh_a, sc_a, g_a, sh_f, sc_f, g_f = (mod[:bsz, i] for i in range(N_MOD))
    csh_a, csc_a = mod[bsz:bsz + 1, 0], mod[bsz:bsz + 1, 1]

    wts = _layer_weights(w_in[layer], q_norm_g[layer], kv_norm_g[layer], w_uq[layer], w_ukv[layer])
    c_lat, s_lat = _rope_tables(length)
    c_id, s_id = _identity_tables(l_ctx)
    keys = peer_keys[layer].reshape(2 * PEER_HEADS, PEER_N_KEYS, PEER_HALF).astype(BF16)
    wglu, wout, wq = w_glu[layer].astype(BF16), w_out[layer].astype(BF16), peer_wq[layer].astype(BF16)
    u_pack = _pack_rows(peer_u[layer])
    v_pack = _pack_rows(peer_v[layer])
    l2g = ln2_g[layer].reshape(1, D_MODEL)
    l2b = ln2_b[layer].reshape(1, D_MODEL)
    t2, spread_mask = _spread_consts()

    n_groups = PEER_GROUPS if bsz % PEER_GROUPS == 0 else 1
    halves = 2 if (n_groups % 2 == 0 and bsz % 16 == 0) else 1
    per_half = n_groups // halves
    hb = bsz // halves
    gb = bsz // n_groups
    g_tok = gb * length
    mats = _s5_matrices(s5_a_re[layer], s5_a_im[layer], s5_log_dt[layer], s5_b_re[layer], s5_b_im[layer],
                        s5_c_re[layer], s5_c_im[layer], hb)
    def half_stages(h, ready):
        x_h, ctx_h, _ = lax.optimization_barrier((x, ctx, ready))
        k_ctx, v_ctx, su_ctx = _inproj(ctx_h, csh_a, csc_a, c_id, s_id, wts, is_ctx=True, tm=min(tm, l_ctx), b0=h * hb, bsz=hb)
        q, k_lat, v_lat, su_lat, gm, gs = _inproj(x_h, sh_a, sc_a, c_lat, s_lat, wts, is_ctx=False, tm=tm, b0=h * hb, bsz=hb)
        su_all = jnp.concatenate([su_ctx, su_lat], axis=0)
        y = _s5_scan(su_all.reshape(-1, S5_WIDTH), mats, bsz=hb, l_ctx=l_ctx, tt=s5_tt)
        return k_ctx, v_ctx, q, k_lat, v_lat, gm, gs, su_all, y.reshape(2, length, hb * S5_WIDTH)

    outs = []

    def finish_group(grp, after):
        vg, act, gate, f_sc, h1f, tok0 = grp
        gate, _ = lax.optimization_barrier((gate, (after, tuple(outs[-1:]))))
        return _peer_eval(vg, act, gate, t2, spread_mask, f_sc, h1f, g_f, l2g, l2b, tb=peer_tb, tok0=tok0, length=length)

    pending = None
    ready = (u_pack, v_pack)
    nxt_half = half_stages(0, ready)
    for g in range(n_groups):
        b0 = g * gb
        h0 = (g % per_half) * gb
        if g % per_half == 0:
            k_ctx, v_ctx, q, k_lat, v_lat, gm, gs, su_all, y2d = nxt_half
        q_g, _ = lax.optimization_barrier((q, ready))
        att = _attention(q_g, k_ctx, v_ctx, k_lat, v_lat, tq=min(ATT_Q_TILE, length), b0=h0, bsz=gb)
        h1, xm, st = _merge(x, att, gm, gs, su_all, y2d, s5_d[layer].reshape(1, S5_WIDTH), wglu, wout, g_a,
                            ln1_g[layer].reshape(1, D_MODEL), ln1_b[layer].reshape(1, D_MODEL), sh_f, sc_f, wq, keys,
                            tm=tm, ctx_tiles=l_ctx // tm, b0=b0, h0=h0)
        experts, gates = _peer_topk(st, lanes=TOPK_LANES)
        idx = experts.transpose(2, 0, 1).reshape(g_tok * PEER_PAIRS)
        gate = gates.transpose(2, 0, 1).reshape(g_tok, PEER_PAIRS)
        xmf = xm.reshape(g_tok, D_MODEL)
        acts, vg, f_sc = _sc_peer(u_pack, v_pack, idx, xmf, gate.reshape(g_tok * PEER_PAIRS))
        launched = (idx, gate, xmf)
        if g % per_half == max(per_half - 3, 0) and g // per_half + 1 < halves:
            nxt_half = half_stages(g // per_half + 1, launched)
            launched = launched + (nxt_half[-1],)
        if pending is not None:
            outs.append(finish_group(pending, launched))
        pending = (vg, acts.reshape(g_tok, PEER_PAIRS), gate, f_sc, h1.reshape(g_tok, D_MODEL), g * g_tok)
        ready = launched + ((outs[-1],) if outs else ())
    outs.append(finish_group(pending, ()))
    return jnp.concatenate(outs, axis=0).reshape(bsz, length, D_MODEL)
```

```python
import functools

import jax
import jax.numpy as jnp
from jax import lax
from jax.experimental import pallas as pl
from jax.experimental.pallas import tpu as pltpu
from jax.experimental.pallas import tpu_sc as plsc

F32 = jnp.float32
BF16 = jnp.bfloat16

D_MODEL = 1024
DEPTH = 1
GRID_W = 64
N_HEADS = 8
QK_NOPE_DIM = 128
QK_ROPE_DIM = 64
V_HEAD_DIM = 128
Q_RANK = 384
KV_RANK = 256
ROPE_AXIS_DIM = QK_ROPE_DIM // 2
ROPE_BASE = 10000.0
S5_WIDTH = D_MODEL // 2
S5_GROUP = 16
S5_GROUPS = S5_WIDTH // S5_GROUP
S5_STATE = 64
PEER_HEADS = 8
PEER_N_KEYS = 128
PEER_TOPK = 16
PEER_HALF = 128
PEER_PAIRS = PEER_HEADS * PEER_TOPK
DEEPNORM_ALPHA = (2.0 * DEPTH) ** 0.25
LN_EPS = 1e-6
N_MOD = 6
ATT_SCALE = (QK_NOPE_DIM + QK_ROPE_DIM) ** -0.5

HEAD_PAD = 256
S5_COLS = S5_GROUPS * S5_STATE
S5_CHUNKS = 2
SC_WORKERS = 32
SC_LANES = 16
U_RING = 4
DOT_ROWS = 8
SC_UNROLL = 1
STAGE_LAG = 2
SC_PAIRS = 64
STAGE_STEPS = (0, 2, 4, 6)
SUM_STEPS = (1, 3, 5, 7)
TC_PAIRS = PEER_PAIRS - SC_PAIRS
CODE_NONE = 1.0e9
PEER_GROUPS = 8
HALF_W = D_MODEL // 2
VMEM_LIMIT = 48 * 1024 * 1024
ROW_TILE = 256
ATT_Q_TILE = 512
S5_TIME_TILE = 64
TOPK_LANES = 128
EVAL_TOKENS = 32
PACK_ROWS = 1024


def _dot(a, b):
    return jnp.dot(a, b, preferred_element_type=F32)


def _dot_nt(a, b):
    return lax.dot_general(a, b, (((1,), (1,)), ((), ())), preferred_element_type=F32)


def _gelu(x):
    return 0.5 * x * (1.0 + jnp.tanh(0.7978845608028654 * (x + 0.044715 * (x * x * x))))


def _ln_plain(x):
    mu = jnp.mean(x, axis=-1, keepdims=True)
    xc = x - mu
    var = jnp.mean(xc * xc, axis=-1, keepdims=True)
    return xc * lax.rsqrt(var + LN_EPS)


def _rms(x, g):
    return x * lax.rsqrt(jnp.mean(x * x, axis=-1, keepdims=True) + LN_EPS) * g


def _params(*sem):
    return pltpu.CompilerParams(dimension_semantics=sem, vmem_limit_bytes=VMEM_LIMIT)


def _mod_kernel(cond_ref, w_ref, b_ref, o_ref):
    a = cond_ref[...]
    a = a * jax.nn.sigmoid(a)
    a_hi = a.astype(BF16)
    a_lo = (a - a_hi.astype(F32)).astype(BF16)
    w = w_ref[...]
    w_hi = w.astype(BF16)
    w_lo = (w - w_hi.astype(F32)).astype(BF16)
    o_ref[...] = _dot(a_hi, w_hi) + _dot(a_lo, w_hi) + _dot(a_hi, w_lo) + b_ref[...]


def _modulation(cond, w_mod, b_mod):
    rows = cond.shape[0]
    n = w_mod.shape[1]
    blk = D_MODEL
    return pl.pallas_call(
        _mod_kernel,
        grid=(n // blk,),
        in_specs=[
            pl.BlockSpec((rows, D_MODEL), lambda j: (0, 0)),
            pl.BlockSpec((D_MODEL, blk), lambda j: (0, j)),
            pl.BlockSpec((1, blk), lambda j: (0, j)),
        ],
        out_specs=pl.BlockSpec((rows, blk), lambda j: (0, j)),
        out_shape=jax.ShapeDtypeStruct((rows, n), F32),
        compiler_params=_params("arbitrary"),
        name="modulation",
    )(cond, w_mod, b_mod.reshape(1, n))


def _rope128(t, c_tab, s_tab):
    return t * c_tab + pltpu.roll(t, 64, axis=1) * s_tab


def _inproj_kernel(x_ref, sh_ref, sc_ref, c_ref, s_ref, wcq_ref, wckv_ref, wsu_ref, wgm_ref, wgs_ref,
                   qg_ref, kvg_ref, wuq_ref, wukv_ref, *out_refs, is_ctx):
    if is_ctx:
        k_ref, v_ref, su_ref = out_refs
    else:
        q_ref, k_ref, v_ref, su_ref, gm_ref, gs_ref = out_refs
    xm = _ln_plain(x_ref[0]) * (1.0 + sc_ref[0]) + sh_ref[0]
    xb = xm.astype(BF16)
    c_tab = c_ref[...]
    s_tab = s_ref[...]

    ckvpe = _dot(xb, wckv_ref[...])
    ckv_n = _rms(ckvpe[:, :KV_RANK], kvg_ref[...]).astype(BF16)
    kpe = _rope128(ckvpe[:, KV_RANK:], c_tab, s_tab).astype(BF16)
    kv = _dot(ckv_n, wukv_ref[...])
    for h in range(N_HEADS):
        k_ref[0, h, :, 0:128] = kv[:, h * 256:h * 256 + 128].astype(BF16)
        k_ref[0, h, :, 128:256] = kpe
        v_ref[0, h] = kv[:, h * 256 + 128:(h + 1) * 256].astype(BF16)

    su_ref[...] = _dot(xb, wsu_ref[...])

    if not is_ctx:
        cq_n = _rms(_dot(xb, wcq_ref[...]), qg_ref[...]).astype(BF16)
        q = _dot(cq_n, wuq_ref[...])
        for h in range(N_HEADS):
            q_ref[0, h, :, 0:128] = (q[:, h * 256:h * 256 + 128] * ATT_SCALE).astype(BF16)
            q_ref[0, h, :, 128:256] = (_rope128(q[:, h * 256 + 128:(h + 1) * 256], c_tab, s_tab) * ATT_SCALE).astype(BF16)
        gm_ref[0] = jax.nn.sigmoid(_dot(xb, wgm_ref[...])).astype(BF16)
        gs_ref[0] = jax.nn.sigmoid(_dot(xb, wgs_ref[...])).astype(BF16)


def _inproj(x, shift, scale, c_tab, s_tab, wts, *, is_ctx, tm, b0, bsz):
    length = x.shape[1]
    mod_map = (lambda b, t: (0, 0, 0)) if shift.shape[0] == 1 else (lambda b, t: (b0 + b, 0, 0))
    const2 = lambda b, t: (0, 0)
    wcq, wckv, wsu, wgm, wgs, qg, kvg, wuq, wukv = wts
    in_specs = [
        pl.BlockSpec((1, tm, D_MODEL), lambda b, t: (b0 + b, t, 0)),
        pl.BlockSpec((1, 1, D_MODEL), mod_map),
        pl.BlockSpec((1, 1, D_MODEL), mod_map),
        pl.BlockSpec((tm, 128), lambda b, t: (t, 0)),
        pl.BlockSpec((tm, 128), lambda b, t: (t, 0)),
    ] + [pl.BlockSpec(w.shape, const2) for w in (wcq, wckv, wsu, wgm, wgs, qg, kvg, wuq, wukv)]
    head_spec = lambda width: pl.BlockSpec((1, N_HEADS, tm, width), lambda b, t: (b, 0, t, 0))
    row_spec = pl.BlockSpec((1, tm, D_MODEL), lambda b, t: (b, t, 0))
    su_spec = pl.BlockSpec((tm, S5_WIDTH), lambda b, t: (t, b))
    k_shape = jax.ShapeDtypeStruct((bsz, N_HEADS, length, HEAD_PAD), BF16)
    v_shape = jax.ShapeDtypeStruct((bsz, N_HEADS, length, V_HEAD_DIM), BF16)
    su_shape = jax.ShapeDtypeStruct((length, bsz * S5_WIDTH), F32)
    g_shape = jax.ShapeDtypeStruct((bsz, length, D_MODEL), BF16)
    if is_ctx:
        out_specs = [head_spec(HEAD_PAD), head_spec(V_HEAD_DIM), su_spec]
        out_shape = [k_shape, v_shape, su_shape]
    else:
        out_specs = [head_spec(HEAD_PAD), head_spec(HEAD_PAD), head_spec(V_HEAD_DIM), su_spec, row_spec, row_spec]
        out_shape = [k_shape, k_shape, v_shape, su_shape, g_shape, g_shape]
    return pl.pallas_call(
        functools.partial(_inproj_kernel, is_ctx=is_ctx),
        grid=(bsz, length // tm),
        in_specs=in_specs,
        out_specs=out_specs,
        out_shape=out_shape,
        compiler_params=_params("parallel", "parallel"),
        name="inproj_ctx" if is_ctx else "inproj_lat",
    )(x, shift, scale, c_tab, s_tab, wcq, wckv, wsu, wgm, wgs, qg, kvg, wuq, wukv)


def _attn_kernel(q_ref, kc_ref, vc_ref, kl_ref, vl_ref, o_ref):
    q = q_ref[0, 0]
    s_c = _dot_nt(q, kc_ref[0, 0])
    s_l = _dot_nt(q, kl_ref[0, 0])
    m = jnp.maximum(jnp.max(s_c, axis=-1, keepdims=True), jnp.max(s_l, axis=-1, keepdims=True))
    p_c = jnp.exp(s_c - m)
    p_l = jnp.exp(s_l - m)
    denom = jnp.sum(p_c, axis=-1, keepdims=True) + jnp.sum(p_l, axis=-1, keepdims=True)
    o = _dot(p_c.astype(BF16), vc_ref[0, 0]) + _dot(p_l.astype(BF16), vl_ref[0, 0])
    o_ref[0] = (o / denom).astype(BF16)


def _attention(q, k_ctx, v_ctx, k_lat, v_lat, *, tq, b0, bsz):
    _, heads, length, _ = q.shape
    l_ctx = k_ctx.shape[2]
    full = lambda b, h, i: (b0 + b, h, 0, 0)
    return pl.pallas_call(
        _attn_kernel,
        grid=(bsz, heads, length // tq),
        in_specs=[
            pl.BlockSpec((1, 1, tq, HEAD_PAD), lambda b, h, i: (b0 + b, h, i, 0)),
            pl.BlockSpec((1, 1, l_ctx, HEAD_PAD), full),
            pl.BlockSpec((1, 1, l_ctx, V_HEAD_DIM), full),
            pl.BlockSpec((1, 1, length, HEAD_PAD), full),
            pl.BlockSpec((1, 1, length, V_HEAD_DIM), full),
        ],
        out_specs=pl.BlockSpec((1, tq, V_HEAD_DIM), lambda b, h, i: (b, i, h)),
        out_shape=jax.ShapeDtypeStruct((bsz, length, heads * V_HEAD_DIM), BF16),
        compiler_params=_params("parallel", "parallel", "arbitrary"),
        name="attention",
    )(q, k_ctx, v_ctx, k_lat, v_lat)


def _s5_kernel(u_ref, are_ref, aim_ref, bre_ref, bim_ref, cre_ref, cim_ref, y_ref,
               bu_re, bu_im, h_re, h_im, *, tt, bsz, n_ctx_tiles):
    d = pl.program_id(0)
    i = pl.program_id(1)

    @pl.when(i == 0)
    def _():
        h_re[...] = jnp.zeros_like(h_re)
        h_im[...] = jnp.zeros_like(h_im)

    u = u_ref[...].astype(BF16)
    half = S5_COLS // S5_CHUNKS
    cw = S5_WIDTH // S5_CHUNKS
    for c in range(S5_CHUNKS):
        uc = u[:, c * cw:(c + 1) * cw]
        bu_re[:, c * half:(c + 1) * half] = _dot(uc, bre_ref[0, c])
        bu_im[:, c * half:(c + 1) * half] = _dot(uc, bim_ref[0, c])

    col_w = 512
    for cc in range(S5_COLS // col_w):
        cols = slice(cc * col_w, (cc + 1) * col_w)
        a_r = are_ref[0, :, cols]
        a_i = aim_ref[0, :, cols]

        def step(t, carry, cols=cols, a_r=a_r, a_i=a_i):
            hr, hi = carry
            pos = jnp.where(d == 0, t, tt - 1 - t)
            r = pl.multiple_of(pos * bsz, bsz)
            nr = a_r * hr - a_i * hi + bu_re[pl.ds(r, bsz), cols]
            ni = a_r * hi + a_i * hr + bu_im[pl.ds(r, bsz), cols]
            bu_re[pl.ds(r, bsz), cols] = nr
            bu_im[pl.ds(r, bsz), cols] = ni
            return nr, ni

        hr, hi = lax.fori_loop(0, tt, step, (h_re[:, cols], h_im[:, cols]))
        h_re[:, cols] = hr
        h_im[:, cols] = hi

    @pl.when(i >= n_ctx_tiles)
    def _():
        for c in range(S5_CHUNKS):
            sl = slice(c * half, (c + 1) * half)
            y = _dot(bu_re[:, sl].astype(BF16), cre_ref[0, c]) + _dot(bu_im[:, sl].astype(BF16), cim_ref[0, c])
            y_ref[0, :, c * cw:(c + 1) * cw] = y


def _s5_scan(u_all, mats, *, bsz, l_ctx, tt):
    a_re_b, a_im_b, b_re_bd, b_im_bd, c_re_bd, c_im_bd = mats
    rows = tt * bsz
    n_tiles = u_all.shape[0] // rows
    n_ctx_tiles = l_ctx // tt
    n_lat_tiles = n_tiles - n_ctx_tiles

    def u_map(d, i):
        bwd = jnp.where(i < n_ctx_tiles, n_ctx_tiles - 1 - i, n_tiles - 1 - (i - n_ctx_tiles))
        return (jnp.where(d == 0, i, bwd), 0)

    def y_map(d, i):
        j = jnp.maximum(i - n_ctx_tiles, 0)
        return (d, jnp.where(d == 0, j, n_lat_tiles - 1 - j), 0)

    dmap3 = lambda d, i: (d, 0, 0)
    dmap4 = lambda d, i: (d, 0, 0, 0)
    return pl.pallas_call(
        functools.partial(_s5_kernel, tt=tt, bsz=bsz, n_ctx_tiles=n_ctx_tiles),
        grid=(2, n_tiles),
        in_specs=[
            pl.BlockSpec((rows, S5_WIDTH), u_map),
            pl.BlockSpec((1,) + a_re_b.shape[1:], dmap3),
            pl.BlockSpec((1,) + a_im_b.shape[1:], dmap3),
            pl.BlockSpec((1,) + b_re_bd.shape[1:], dmap4),
            pl.BlockSpec((1,) + b_im_bd.shape[1:], dmap4),
            pl.BlockSpec((1,) + c_re_bd.shape[1:], dmap4),
            pl.BlockSpec((1,) + c_im_bd.shape[1:], dmap4),
        ],
        out_specs=pl.BlockSpec((1, rows, S5_WIDTH), y_map),
        out_shape=jax.ShapeDtypeStruct((2, n_lat_tiles * rows, S5_WIDTH), F32),
        scratch_shapes=[
            pltpu.VMEM((rows, S5_COLS), F32),
            pltpu.VMEM((rows, S5_COLS), F32),
            pltpu.VMEM((bsz, S5_COLS), F32),
            pltpu.VMEM((bsz, S5_COLS), F32),
        ],
        compiler_params=_params("arbitrary", "arbitrary"),
        name="s5_scan",
    )(u_all, a_re_b, a_im_b, b_re_bd, b_im_bd, c_re_bd, c_im_bd)


def _s5_matrices(a_re, a_im, log_dt, b_re, b_im, c_re, c_im, bsz):
    dt = jnp.exp(log_dt)[..., None]
    mag = jnp.exp(a_re * dt)
    ab_re = mag * jnp.cos(a_im * dt)
    ab_im = mag * jnp.sin(a_im * dt)
    den = a_re * a_re + a_im * a_im
    f_re = ((ab_re - 1.0) * a_re + ab_im * a_im) / den
    f_im = (ab_im * a_re - (ab_re - 1.0) * a_im) / den
    bb_re = f_re[..., None] * b_re - f_im[..., None] * b_im
    bb_im = f_re[..., None] * b_im + f_im[..., None] * b_re
    gl = S5_GROUPS // S5_CHUNKS
    eye = jnp.eye(gl, dtype=F32)

    def in_bd(bb):
        t = bb.reshape(2, S5_CHUNKS, gl, S5_STATE, S5_GROUP)
        t = jnp.einsum('dkgpc,gh->dkgchp', t, eye)
        return t.reshape(2, S5_CHUNKS, gl * S5_GROUP, gl * S5_STATE).astype(BF16)

    def out_bd(cc):
        t = cc.reshape(2, S5_CHUNKS, gl, S5_GROUP, S5_STATE)
        t = jnp.einsum('dkgcp,gh->dkgphc', t, eye)
        return t.reshape(2, S5_CHUNKS, gl * S5_STATE, gl * S5_GROUP).astype(BF16)

    bcast = lambda a: jnp.broadcast_to(a.reshape(2, 1, S5_COLS), (2, bsz, S5_COLS))
    return bcast(ab_re), bcast(ab_im), in_bd(bb_re), in_bd(bb_im), out_bd(c_re), out_bd(-c_im)


def _merge_kernel(x_ref, att_ref, gm_ref, gs_ref, su_ref, y_ref, dsk_ref, wglu_ref, wout_ref, ga_ref,
                  l1g_ref, l1b_ref, shf_ref, scf_ref, wq_ref, keys_ref, h1_ref, xm_ref, e_ref, g_ref, st_ref):
    y = su_ref[...] * dsk_ref[...] + y_ref[0] + y_ref[1]
    gl = _dot(_gelu(y).astype(BF16), wglu_ref[...])
    s5_out = gl[:, :D_MODEL] * jax.nn.sigmoid(gl[:, D_MODEL:])
    merged = gm_ref[0].astype(F32) * att_ref[0].astype(F32) + gs_ref[0].astype(F32) * s5_out
    out = _dot(merged.astype(BF16), wout_ref[...])
    h1 = _ln_plain(DEEPNORM_ALPHA * x_ref[0] + ga_ref[0] * out) * l1g_ref[...] + l1b_ref[...]
    h1_ref[0] = h1
    xm = _ln_plain(h1) * (1.0 + scf_ref[0]) + shf_ref[0]
    xm_ref[0] = xm
    qp = _dot(xm.astype(BF16), wq_ref[...]).astype(BF16)
    for j in range(2 * PEER_HEADS):
        st_ref[j] = _dot_nt(keys_ref[j], qp[:, j * PEER_HALF:(j + 1) * PEER_HALF])
    for c in range(st_ref.shape[2] // TOPK_LANES):
        sl = pl.ds(c * TOPK_LANES, TOPK_LANES)
        _topk_kernel(st_ref.at[:, :, sl], e_ref.at[:, :, sl], g_ref.at[:, :, sl])


def _merge(x, att, gm, gs, su_all2d, y2d, dsk, wglu, wout, g_a, l1g, l1b, sh_f, sc_f, wq, keys, *, tm, ctx_tiles, b0, h0):
    bsz, length, _ = att.shape
    nt = length // tm
    row = pl.BlockSpec((1, tm, D_MODEL), lambda b, t: (b, t, 0))
    xrow = pl.BlockSpec((1, tm, D_MODEL), lambda b, t: (b0 + b, t, 0))
    hrow = pl.BlockSpec((1, tm, D_MODEL), lambda b, t: (h0 + b, t, 0))
    modb = pl.BlockSpec((1, 1, D_MODEL), lambda b, t: (b0 + b, 0, 0))
    const = lambda a: pl.BlockSpec(a.shape, lambda b, t: (0,) * a.ndim)
    return pl.pallas_call(
        _merge_kernel,
        grid=(bsz, nt),
        in_specs=[
            xrow, row, hrow, hrow,
            pl.BlockSpec((tm, S5_WIDTH), lambda b, t: (ctx_tiles + t, h0 + b)),
            pl.BlockSpec((2, tm, S5_WIDTH), lambda b, t: (0, t, h0 + b)),
            const(dsk), const(wglu), const(wout), modb, const(l1g), const(l1b), modb, modb, const(wq), const(keys),
        ],
        out_specs=[row, row,
                   pl.BlockSpec((PEER_HEADS, PEER_TOPK, tm), lambda b, t: (0, 0, b * nt + t)),
                   pl.BlockSpec((PEER_HEADS, PEER_TOPK, tm), lambda b, t: (0, 0, b * nt + t))],
        out_shape=[
            jax.ShapeDtypeStruct((bsz, length, D_MODEL), F32),
            jax.ShapeDtypeStruct((bsz, length, D_MODEL), F32),
            jax.ShapeDtypeStruct((PEER_HEADS, PEER_TOPK, bsz * length), jnp.int32),
            jax.ShapeDtypeStruct((PEER_HEADS, PEER_TOPK, bsz * length), F32),
        ],
        scratch_shapes=[pltpu.VMEM((2 * PEER_HEADS, PEER_N_KEYS, tm), F32)],
        compiler_params=_params("parallel", "parallel"),
        name="merge_peer_query",
    )(x, att, gm, gs, su_all2d, y2d, dsk, wglu, wout, g_a, l1g, l1b, sh_f, sc_f, wq, keys)


def _take_top(vals, codes, payload, k):
    rows = lax.broadcasted_iota(jnp.int32, (k, vals.shape[1]), 0)
    top_v = jnp.zeros((k, vals.shape[1]), F32)
    top_p = jnp.zeros((k, vals.shape[1]), F32)
    for r in range(k):
        m = jnp.max(vals, axis=0, keepdims=True)
        cm = jnp.min(jnp.where(vals == m, codes, CODE_NONE), axis=0, keepdims=True)
        sel = codes == cm
        if payload is None:
            p = cm
        else:
            p = jnp.max(jnp.where(sel, payload, -1.0), axis=0, keepdims=True)
        vals = jnp.where(sel, -jnp.inf, vals)
        top_v = jnp.where(rows == r, m, top_v)
        top_p = jnp.where(rows == r, p, top_p)
    return top_v, top_p


def _topk_kernel(st_ref, e_ref, g_ref):
    lanes = st_ref.shape[2]
    key_idx = lax.broadcasted_iota(jnp.int32, (PEER_N_KEYS, lanes), 0).astype(F32)
    sub_idx = lax.broadcasted_iota(jnp.int32, (PEER_TOPK, lanes), 0).astype(F32)
    half_k = PEER_TOPK // 2
    for h in range(PEER_HEADS):
        sv0, si0 = _take_top(st_ref[2 * h], key_idx, None, PEER_TOPK)
        sv1, si1 = _take_top(st_ref[2 * h + 1], key_idx, None, PEER_TOPK)
        cs, ce, cc = [], [], []
        for i in range(half_k):
            nj = PEER_TOPK if i == 0 else half_k
            cs.append(sv0[i:i + 1, :] + sv1[0:nj, :])
            ce.append(si0[i:i + 1, :] * PEER_N_KEYS + si1[0:nj, :])
            cc.append(sub_idx[0:nj, :] + i * PEER_TOPK)
        cs.append(sv0[half_k:, :] + sv1[0:1, :])
        ce.append(si0[half_k:, :] * PEER_N_KEYS + si1[0:1, :])
        cc.append((sub_idx[0:half_k, :] + half_k) * PEER_TOPK)
        top_s, top_e = _take_top(jnp.concatenate(cs, axis=0), jnp.concatenate(cc, axis=0),
                                 jnp.concatenate(ce, axis=0), PEER_TOPK)
        p = jnp.exp(top_s - jnp.max(top_s, axis=0, keepdims=True))
        g_ref[h] = p / jnp.sum(p, axis=0, keepdims=True)
        e_ref[h] = top_e.astype(jnp.int32)


def _sc_peer(u_tab, v_tab, idx, x, gates):
    n_tok = x.shape[0]
    per_worker = n_tok // SC_WORKERS
    assert per_worker % 2 == 0 and per_worker >= 4
    assert len(STAGE_STEPS) == (PEER_PAIRS - SC_PAIRS) // SC_LANES and len(SUM_STEPS) == SC_PAIRS // SC_LANES
    steps = PEER_PAIRS // SC_LANES
    assert steps % U_RING == 0
    n_vc = SC_PAIRS // SC_LANES
    n_vs = (PEER_PAIRS - SC_PAIRS) // SC_LANES
    tc_pairs = PEER_PAIRS - SC_PAIRS
    chunks = HALF_W // SC_LANES
    mesh = plsc.VectorSubcoreMesh(core_axis_name="c", subcore_axis_name="s")

    @functools.partial(
        pl.kernel, mesh=mesh,
        out_type=(jax.ShapeDtypeStruct((n_tok * PEER_PAIRS,), F32),
                  jax.ShapeDtypeStruct((n_tok * tc_pairs, HALF_W), jnp.int32),
                  jax.ShapeDtypeStruct((n_tok, 2 * HALF_W), F32)),
        scratch_types=[
            pltpu.VMEM((2, PEER_PAIRS), jnp.int32),
            pltpu.VMEM((2, PEER_PAIRS), jnp.int32),
            pltpu.VMEM((2, 2 * HALF_W), F32),
            pltpu.VMEM((2, SC_PAIRS), F32),
            pltpu.VMEM((2, SC_PAIRS), F32),
            pltpu.VMEM((2, PEER_PAIRS), F32),
            pltpu.VMEM((2, 2 * HALF_W), F32),
            pltpu.VMEM((U_RING, SC_LANES, HALF_W), jnp.int32),
            pltpu.VMEM((n_vc, SC_LANES, HALF_W), jnp.int32),
            pltpu.VMEM((n_vs, SC_LANES, HALF_W), jnp.int32),
            pltpu.SemaphoreType.DMA((U_RING,)),
            pltpu.SemaphoreType.DMA((n_vc,)),
            pltpu.SemaphoreType.DMA((n_vs,)),
            pltpu.SemaphoreType.DMA((n_vs,)),
            pltpu.SemaphoreType.DMA((2,)),
            pltpu.SemaphoreType.DMA((2,)),
            pltpu.SemaphoreType.DMA((2,)),
            pltpu.SemaphoreType.DMA((2,)),
            pltpu.SemaphoreType.DMA((2,)),
        ],
        compiler_params=pltpu.CompilerParams(needs_layout_passes=False),
    )
    def peer(u_hbm, v_hbm, idx_hbm, x_hbm, gate_hbm, act_hbm, vrows_hbm, f_hbm,
             idx_v, vidx_v, x_v, gate_v, coef_v, act_v, f_v, urows_v, vsum_v, vstage_v,
             u_sems, vc_sems, vg_sems, vw_sems, idx_sems, x_sems, gate_sems, act_sems, f_sems):
        wid = lax.axis_index("s") * 2 + lax.axis_index("c")
        base = wid * per_worker
        last = base + per_worker - 1
        lane = lax.iota(jnp.int32, SC_LANES)

        def pair_off(tok):
            return pl.multiple_of(tok * PEER_PAIRS, PEER_PAIRS)

        def idx_copy(tok, slot):
            return pltpu.make_async_copy(idx_hbm.at[pl.ds(pair_off(tok), PEER_PAIRS)], idx_v.at[slot], idx_sems.at[slot])

        def x_copy(tok, slot):
            return pltpu.make_async_copy(x_hbm.at[tok], x_v.at[slot], x_sems.at[slot])

        def gate_copy(tok, slot):
            return pltpu.make_async_copy(gate_hbm.at[pl.ds(pair_off(tok), SC_PAIRS)], gate_v.at[slot], gate_sems.at[slot])

        def u_gather(slot, g):
            return pltpu.make_async_copy(u_hbm.at[idx_v.at[slot, pl.ds(g * SC_LANES, SC_LANES)]], urows_v.at[g % U_RING],
                                         u_sems.at[g % U_RING])

        def vsum_gather(slot, j):
            return pltpu.make_async_copy(v_hbm.at[vidx_v.at[slot, pl.ds(j * SC_LANES, SC_LANES)]], vsum_v.at[j],
                                         vc_sems.at[j])

        def vstage_gather(slot, s):
            return pltpu.make_async_copy(v_hbm.at[vidx_v.at[slot, pl.ds(SC_PAIRS + s * SC_LANES, SC_LANES)]],
                                         vstage_v.at[s], vg_sems.at[s])

        def vstage_write(tok, s):
            row0 = pl.multiple_of(tok * tc_pairs + s * SC_LANES, SC_LANES)
            return pltpu.make_async_copy(vstage_v.at[s], vrows_hbm.at[pl.ds(row0, SC_LANES)], vw_sems.at[s])

        def act_copy(tok, slot):
            return pltpu.make_async_copy(act_v.at[slot], act_hbm.at[pl.ds(pair_off(tok), PEER_PAIRS)], act_sems.at[slot])

        def f_copy(tok, slot):
            return pltpu.make_async_copy(f_v.at[slot], f_hbm.at[tok], f_sems.at[slot])

        def unpack(wv):
            return (lax.bitcast_convert_type(wv << 16, F32), lax.bitcast_convert_type(wv & jnp.int32(-65536), F32))

        def step_dots(slot, g):
            v = jnp.zeros((SC_LANES,), F32)
            for r0 in range(0, SC_LANES, DOT_ROWS):
                def body(c, accs, r0=r0):
                    xlo = x_v[slot, pl.ds(c * SC_LANES, SC_LANES)]
                    xhi = x_v[slot, pl.ds(HALF_W + c * SC_LANES, SC_LANES)]
                    new = []
                    for r in range(DOT_ROWS):
                        lo, hi = unpack(urows_v[g % U_RING, r0 + r, pl.ds(c * SC_LANES, SC_LANES)])
                        new.append(accs[r] + lo * xlo + hi * xhi)
                    return tuple(new)

                accs = plsc.parallel_loop(0, chunks, unroll=SC_UNROLL,
                                          carry=tuple(jnp.zeros((SC_LANES,), F32) for _ in range(DOT_ROWS)))(body)
                for r in range(DOT_ROWS):
                    v = jnp.where(lane == r0 + r, jnp.sum(accs[r]), v)
            act_v[slot, pl.ds(g * SC_LANES, SC_LANES)] = v

        def window_sum(fslot, j):
            cvec = coef_v[fslot, pl.ds(j * SC_LANES, SC_LANES)]
            cb = [jnp.zeros((SC_LANES,), F32) + jnp.sum(jnp.where(lane == r, cvec, 0.0)) for r in range(SC_LANES)]

            def body(c):
                alo = f_v[fslot, pl.ds(c * SC_LANES, SC_LANES)]
                ahi = f_v[fslot, pl.ds(HALF_W + c * SC_LANES, SC_LANES)]
                for r in range(SC_LANES):
                    lo, hi = unpack(vsum_v[j, r, pl.ds(c * SC_LANES, SC_LANES)])
                    alo = alo + cb[r] * lo
                    ahi = ahi + cb[r] * hi
                f_v[fslot, pl.ds(c * SC_LANES, SC_LANES)] = alo
                f_v[fslot, pl.ds(HALF_W + c * SC_LANES, SC_LANES)] = ahi

            plsc.parallel_loop(0, chunks, unroll=SC_UNROLL)(body)

        def token_coefs(slot):
            for q in range(n_vc):
                a = act_v[slot, pl.ds(q * SC_LANES, SC_LANES)]
                z2 = 1.5957691216057308 * (a + 0.044715 * (a * a * a))
                coef_v[slot, pl.ds(q * SC_LANES, SC_LANES)] = gate_v[slot, pl.ds(q * SC_LANES, SC_LANES)] * a / (1.0 + jnp.exp(-z2))

        def zero_f(fslot):
            for q in range(2 * HALF_W // SC_LANES):
                f_v[fslot, pl.ds(q * SC_LANES, SC_LANES)] = jnp.zeros((SC_LANES,), F32)

        idx_copy(base, 0).start()
        x_copy(base, 0).start()
        gate_copy(base, 0).start()
        idx_copy(base, 0).wait()
        for g in range(U_RING):
            u_gather(0, g).start()

        @pl.loop(0, per_worker, step=2)
        def _(i):
            for slot in range(2):
                tok = base + i + slot
                nxt = jnp.minimum(tok + 1, last)
                other = 1 - slot
                first = slot == 0
                idx_copy(nxt, other).start()
                x_copy(nxt, other).start()
                gate_copy(nxt, other).start()
                for q in range(PEER_PAIRS // SC_LANES):
                    vidx_v[slot, pl.ds(q * SC_LANES, SC_LANES)] = idx_v[slot, pl.ds(q * SC_LANES, SC_LANES)]
                x_copy(tok, slot).wait()
                gate_copy(tok, slot).wait()

                @pl.when(i > 0)
                def _():
                    act_copy(tok, slot).wait()

                if first:
                    @pl.when(i > 2)
                    def _():
                        f_copy(tok, other).wait()
                else:
                    @pl.when(i > 0)
                    def _():
                        f_copy(tok, other).wait()
                zero_f(other)

                idx_copy(nxt, other).wait()
                for g in range(steps):
                    s = STAGE_STEPS.index(g) if g in STAGE_STEPS else None
                    if s is not None:
                        if first:
                            @pl.when(i > 0)
                            def _(s=s, tok=tok):
                                vstage_write(tok, s).wait()
                        else:
                            vstage_write(tok, s).wait()
                        vstage_gather(slot, s).start()
                    u_gather(slot, g).wait()
                    step_dots(slot, g)
                    if g + U_RING < steps:
                        u_gather(slot, g + U_RING).start()
                    else:
                        u_gather(other, g + U_RING - steps).start()
                    if s is not None:
                        sp = (s - STAGE_LAG) % n_vs
                        tokp, slotp = (tok, slot) if s >= STAGE_LAG else (tok - 1, other)

                        def finish(sp=sp, tokp=tokp, slotp=slotp):
                            vstage_gather(slotp, sp).wait()
                            vstage_write(tokp, sp).start()

                        if first and s < STAGE_LAG:
                            pl.when(i > 0)(finish)
                        else:
                            finish()
                    if g in SUM_STEPS:
                        j = SUM_STEPS.index(g)

                        def consume(j=j, other=other):
                            vsum_gather(other, j).wait()
                            window_sum(other, j)

                        if first:
                            pl.when(i > 0)(consume)
                        else:
                            consume()
                        vsum_gather(slot, j).start()
                token_coefs(slot)
                act_copy(tok, slot).start()

                def send_prev(tok=tok, other=other):
                    f_copy(tok - 1, other).start()

                if first:
                    pl.when(i > 0)(send_prev)
                else:
                    send_prev()

        for s in range(n_vs - STAGE_LAG, n_vs):
            vstage_gather(1, s).wait()
            vstage_write(last, s).start()
        f_copy(last, 1).wait()
        zero_f(1)
        for j in range(n_vc):
            vsum_gather(1, j).wait()
            window_sum(1, j)
        f_copy(last, 1).start()
        for s in range(n_vs):
            vstage_write(last, s).wait()
        for g in range(U_RING):
            u_gather(0, g).wait()
        x_copy(last, 0).wait()
        gate_copy(last, 0).wait()
        for slot in range(2):
            f_copy(last, slot).wait()
            act_copy(last, slot).wait()

    return peer(u_tab, v_tab, idx, x, gates)


def _peer_kernel(vg_ref, act_ref, gate_ref, t2_ref, mask_ref, fsc_ref, h1_ref, gf_ref, l2g_ref, l2b_ref, o_ref, f_ref, *, tb):
    sub = 8
    span = sub * TC_PAIRS
    half = D_MODEL // 2
    for s in range(tb // sub):
        rows = slice(s * span, (s + 1) * span)
        toks = slice(s * sub, (s + 1) * sub)
        coef = (gate_ref[toks, :] * _gelu(act_ref[toks, :])).astype(BF16)
        spread = _dot(coef, t2_ref[...])
        c2 = (jnp.concatenate([spread, spread], axis=0) * mask_ref[...]).astype(BF16)
        f2 = _dot(c2, pltpu.bitcast(vg_ref[rows, :], BF16))
        f_ref[toks, :half] = f2[:sub]
        f_ref[toks, half:] = f2[sub:]
    r = DEEPNORM_ALPHA * h1_ref[...] + gf_ref[0] * (f_ref[...] + fsc_ref[...])
    o_ref[...] = _ln_plain(r) * l2g_ref[...] + l2b_ref[...]


def _peer_eval(vg, act, gate, t2, mask, f_sc, h1, g_f, l2g, l2b, *, tb, tok0, length):
    n = h1.shape[0]
    tok_blk0 = tok0 // tb
    per_b = length // tb
    tok_spec = lambda width: pl.BlockSpec((tb, width), lambda i: (i, 0))
    return pl.pallas_call(
        functools.partial(_peer_kernel, tb=tb),
        grid=(n // tb,),
        in_specs=[
            pl.BlockSpec((tb * TC_PAIRS, D_MODEL // 2), lambda i: (i, 0)),
            tok_spec(PEER_PAIRS), tok_spec(PEER_PAIRS),
            pl.BlockSpec(t2.shape, lambda i: (0, 0)), pl.BlockSpec(mask.shape, lambda i: (0, 0)),
            tok_spec(D_MODEL), tok_spec(D_MODEL),
            pl.BlockSpec((1, 1, D_MODEL), lambda i: ((tok_blk0 + i) // per_b, 0, 0)),
            pl.BlockSpec((1, D_MODEL), lambda i: (0, 0)),
            pl.BlockSpec((1, D_MODEL), lambda i: (0, 0)),
        ],
        out_specs=tok_spec(D_MODEL),
        out_shape=jax.ShapeDtypeStruct((n, D_MODEL), F32),
        scratch_shapes=[pltpu.VMEM((tb, D_MODEL), F32)],
        compiler_params=_params("parallel"),
        name="peer_eval",
    )(vg, act, gate, t2, mask, f_sc, h1, g_f, l2g, l2b)


def _rope_perm():
    half = ROPE_AXIS_DIM // 2
    base = jnp.arange(QK_ROPE_DIM)
    return jnp.where((base % ROPE_AXIS_DIM) < half, base + half, base - half)


def _rope_tables(length):
    pos = jnp.arange(length)
    row = (pos // GRID_W).astype(F32)
    col = (pos % GRID_W).astype(F32)
    inv_freq = jnp.power(ROPE_BASE, -jnp.arange(0, ROPE_AXIS_DIM, 2, dtype=F32) / ROPE_AXIS_DIM)
    ang_r = row[:, None] * inv_freq
    ang_c = col[:, None] * inv_freq
    zeros = jnp.zeros((length, QK_ROPE_DIM), F32)
    c_tab = jnp.concatenate([jnp.cos(ang_r), jnp.cos(ang_r), jnp.cos(ang_c), jnp.cos(ang_c), zeros], axis=1)
    s_tab = jnp.concatenate([-jnp.sin(ang_r), jnp.sin(ang_r), -jnp.sin(ang_c), jnp.sin(ang_c), zeros], axis=1)
    return c_tab, s_tab


def _identity_tables(length):
    ones = jnp.ones((length, QK_ROPE_DIM), F32)
    zeros = jnp.zeros((length, QK_ROPE_DIM), F32)
    return jnp.concatenate([ones, zeros], axis=1), jnp.zeros((length, 2 * QK_ROPE_DIM), F32)


def _spread_consts():
    lanes = jnp.arange(16 * TC_PAIRS)
    pair = (lanes % (2 * TC_PAIRS)) // 2 + SC_PAIRS
    t2 = (jnp.arange(PEER_PAIRS)[:, None] == pair[None, :]).astype(BF16)
    r = jnp.arange(16)
    own = (lanes // (2 * TC_PAIRS))[None, :] == (r % 8)[:, None]
    parity = (lanes % 2 == 0)[None, :] == (r < 8)[:, None]
    return t2, (own & parity).astype(F32)


def _pack_kernel(t_ref, o_ref):
    lo = pltpu.bitcast(t_ref[:, :HALF_W].astype(BF16).astype(F32), jnp.uint32)
    hi = pltpu.bitcast(t_ref[:, HALF_W:].astype(BF16).astype(F32), jnp.uint32)
    o_ref[...] = pltpu.bitcast((hi & jnp.uint32(0xFFFF0000)) | (lo >> 16), jnp.int32)


def _pack_rows(table):
    rows = table.shape[0]
    return pl.pallas_call(
        _pack_kernel,
        grid=(rows // PACK_ROWS,),
        in_specs=[pl.BlockSpec((PACK_ROWS, 2 * HALF_W), lambda i: (i, 0))],
        out_specs=pl.BlockSpec((PACK_ROWS, HALF_W), lambda i: (i, 0)),
        out_shape=jax.ShapeDtypeStruct((rows, HALF_W), jnp.int32),
        compiler_params=_params("parallel"),
        name="pack_table",
    )(table)


def _layer_weights(w_in, q_norm_g, kv_norm_g, w_uq, w_ukv):
    perm = _rope_perm()
    p0 = Q_RANK
    p1 = p0 + KV_RANK
    p2 = p1 + QK_ROPE_DIM
    p3 = p2 + S5_WIDTH
    p4 = p3 + D_MODEL
    w_kpe = w_in[:, p1:p2]
    wckv = jnp.concatenate([w_in[:, p0:p1], w_kpe, w_kpe[:, perm]], axis=1)
    uq = w_uq.reshape(Q_RANK, N_HEADS, QK_NOPE_DIM + QK_ROPE_DIM)
    pe = uq[:, :, QK_NOPE_DIM:]
    wuq = jnp.concatenate([uq[:, :, :QK_NOPE_DIM], pe, pe[:, :, perm]], axis=2).reshape(Q_RANK, N_HEADS * HEAD_PAD)
    cast = lambda a: a.astype(BF16)
    return (cast(w_in[:, :p0]), cast(wckv), cast(w_in[:, p2:p3]), cast(w_in[:, p3:p4]), cast(w_in[:, p4:]),
            q_norm_g.reshape(1, Q_RANK), kv_norm_g.reshape(1, KV_RANK), cast(wuq), cast(w_ukv))


def kernel(x, c, ctx, c_ctx, w_mod, b_mod, w_in, q_norm_g, kv_norm_g, w_uq, w_ukv, s5_a_re, s5_a_im, s5_log_dt, s5_b_re, s5_b_im, s5_c_re, s5_c_im, s5_d, w_glu, w_out, ln1_g, ln1_b, peer_wq, peer_keys, peer_u, peer_v, ln2_g, ln2_b):
    bsz, length, _ = x.shape
    l_ctx = ctx.shape[1]
    assert w_mod.shape[0] == DEPTH == 1
    assert length % ROW_TILE == 0 and l_ctx % ROW_TILE == 0 and length % GRID_W == 0
    layer = 0
    tm = ROW_TILE
    s5_tt = S5_TIME_TILE
    peer_tb = EVAL_TOKENS

    cond = jnp.concatenate([c, c_ctx[None, :], jnp.zeros((7, D_MODEL), F32)], axis=0)
    mod = _modulation(cond, w_mod[layer], b_mod[layer]).reshape(cond.shape[0], N_MOD, 1, D_MODEL)
    sh_a, sc_a, g_a, sh_f, sc_f, g_f = (mod[:bsz, i] for i in range(N_MOD))
    csh_a, csc_a = mod[bsz:bsz + 1, 0], mod[bsz:bsz + 1, 1]

    wts = _layer_weights(w_in[layer], q_norm_g[layer], kv_norm_g[layer], w_uq[layer], w_ukv[layer])
    c_lat, s_lat = _rope_tables(length)
    c_id, s_id = _identity_tables(l_ctx)
    keys = peer_keys[layer].reshape(2 * PEER_HEADS, PEER_N_KEYS, PEER_HALF).astype(BF16)
    wglu, wout, wq = w_glu[layer].astype(BF16), w_out[layer].astype(BF16), peer_wq[layer].astype(BF16)
    u_pack = _pack_rows(peer_u[layer])
    v_pack = _pack_rows(peer_v[layer])
    l2g = ln2_g[layer].reshape(1, D_MODEL)
    l2b = ln2_b[layer].reshape(1, D_MODEL)
    t2, spread_mask = _spread_consts()

    n_groups = PEER_GROUPS if bsz % PEER_GROUPS == 0 else 1
    halves = 2 if (n_groups % 2 == 0 and bsz % 16 == 0) else 1
    per_half = n_groups // halves
    hb = bsz // halves
    gb = bsz // n_groups
    g_tok = gb * length
    mats = _s5_matrices(s5_a_re[layer], s5_a_im[layer], s5_log_dt[layer], s5_b_re[layer], s5_b_im[layer],
                        s5_c_re[layer], s5_c_im[layer], hb)
    def half_stages(h, ready):
        x_h, ctx_h, _ = lax.optimization_barrier((x, ctx, ready))
        k_ctx, v_ctx, su_ctx = _inproj(ctx_h, csh_a, csc_a, c_id, s_id, wts, is_ctx=True, tm=min(tm, l_ctx), b0=h * hb, bsz=hb)
        q, k_lat, v_lat, su_lat, gm, gs = _inproj(x_h, sh_a, sc_a, c_lat, s_lat, wts, is_ctx=False, tm=tm, b0=h * hb, bsz=hb)
        su_all = jnp.concatenate([su_ctx, su_lat], axis=0)
        y = _s5_scan(su_all.reshape(-1, S5_WIDTH), mats, bsz=hb, l_ctx=l_ctx, tt=s5_tt)
        return k_ctx, v_ctx, q, k_lat, v_lat, gm, gs, su_all, y.reshape(2, length, hb * S5_WIDTH)

    outs = []

    def finish_group(grp, after):
        vg, act, gate, f_sc, h1f, tok0 = grp
        gate, _ = lax.optimization_barrier((gate, (after, tuple(outs[-1:]))))
        return _peer_eval(vg, act, gate, t2, spread_mask, f_sc, h1f, g_f, l2g, l2b, tb=peer_tb, tok0=tok0, length=length)

    pending = None
    ready = (u_pack, v_pack)
    nxt_half = half_stages(0, ready)
    for g in range(n_groups):
        b0 = g * gb
        h0 = (g % per_half) * gb
        if g % per_half == 0:
            k_ctx, v_ctx, q, k_lat, v_lat, gm, gs, su_all, y2d = nxt_half
        q_g, _ = lax.optimization_barrier((q, ready))
        att = _attention(q_g, k_ctx, v_ctx, k_lat, v_lat, tq=min(ATT_Q_TILE, length), b0=h0, bsz=gb)
        h1, xm, experts, gates = _merge(x, att, gm, gs, su_all, y2d, s5_d[layer].reshape(1, S5_WIDTH), wglu, wout, g_a,
                                        ln1_g[layer].reshape(1, D_MODEL), ln1_b[layer].reshape(1, D_MODEL), sh_f, sc_f,
                                        wq, keys, tm=tm, ctx_tiles=l_ctx // tm, b0=b0, h0=h0)
        idx = experts.transpose(2, 0, 1).reshape(g_tok * PEER_PAIRS)
        gate = gates.transpose(2, 0, 1).reshape(g_tok, PEER_PAIRS)
        xmf = xm.reshape(g_tok, D_MODEL)
        acts, vg, f_sc = _sc_peer(u_pack, v_pack, idx, xmf, gate.reshape(g_tok * PEER_PAIRS))
        launched = (idx, gate, xmf)
        if g % per_half == max(per_half - 3, 0) and g // per_half + 1 < halves:
            nxt_half = half_stages(g // per_half + 1, launched)
            launched = launched + (nxt_half[-1],)
        if pending is not None:
            outs.append(finish_group(pending, launched))
        pending = (vg, acts.reshape(g_tok, PEER_PAIRS), gate, f_sc, h1.reshape(g_tok, D_MODEL), g * g_tok)
        ready = launched + ((outs[-1],) if outs else ())
    outs.append(finish_group(pending, ()))
    return jnp.concatenate(outs, axis=0).reshape(bsz, length, D_MODEL)
```
